```python
import math
import jax, jax.numpy as jnp
from jax import lax
import numpy as np

D_MODEL = 2048
BATCH = 8
SEQ = 2048
DEPTH = 2
DEC_BATCH = 128
DEC_SEQ = 8
PAST_LEN = 2048
PAGE_SIZE = 128

N_EVEN = (DEPTH + 1) // 2
N_ODD = DEPTH // 2
H_A = 4
DK_A = D_MODEL // (2 * H_A)
DV_A = D_MODEL // (2 * H_A)
QK_W = H_A * DK_A
A_W = H_A * DV_A
B_W = D_MODEL // 2
CONV_W = 31
MLSTM_CHUNK = 64
EVEN_IN = 2 * QK_W + 2 * A_W + 2 * H_A + 2 * B_W
H_C = 16
DH_C = D_MODEL // H_C
MOBA_BLOCK = 256
MOBA_TOPK = 3
MOBA_QCHUNK = 64
REL_BUCKETS = 32
REL_MAX_DIST = 128
D_FF = 256 * ((8 * D_MODEL // 3 + 255) // 256)
FFN_CONV_W = 3
EPS = 1e-6

kernel_name = 'hybrid_mlstm_conformer_moba_step'


def _rms(x):
    xf = x.astype(jnp.float32)
    return (xf * lax.rsqrt(jnp.mean(xf * xf, axis=-1, keepdims=True) + EPS)).astype(x.dtype)


def _layernorm(x, g, b):
    xf = x.astype(jnp.float32)
    mu = jnp.mean(xf, axis=-1, keepdims=True)
    var = jnp.mean(jnp.square(xf - mu), axis=-1, keepdims=True)
    return ((xf - mu) * lax.rsqrt(var + 1e-5) * g + b).astype(x.dtype)


def _causal_dwconv(u, buf, w, b):
    full = jnp.concatenate([buf.astype(u.dtype), u], axis=1)
    out = lax.conv_general_dilated(full, w[:, None, :].astype(u.dtype), window_strides=(1,), padding='VALID',
                                   dimension_numbers=('NWC', 'WIO', 'NWC'), feature_group_count=u.shape[-1])
    return out + b, full[:, full.shape[1] - (w.shape[0] - 1):]


def _mlstm(q, k, v, i_pre, f_pre, C0, n0, m0):
    B, T, H, dk = q.shape
    L = MLSTM_CHUNK if T % MLSTM_CHUNK == 0 else T
    nC = T // L
    f32 = jnp.float32
    q = q.astype(f32) * (dk ** -0.5)
    k = k.astype(f32)
    v = v.astype(f32)
    li = i_pre.astype(f32)
    lf = jax.nn.log_sigmoid(f_pre.astype(f32))

    def chunks(a):
        return jnp.moveaxis(a.reshape((B, nC, L) + a.shape[2:]), 1, 0)

    causal = jnp.tril(jnp.ones((L, L), bool))[None, :, :, None]

    def step(carry, xs):
        C, n, m = carry
        qc, kc, vc, lic, lfc = xs
        b = jnp.cumsum(lfc, axis=1)
        a = b + m[:, None, :]
        D = jnp.where(causal, b[:, :, None, :] - b[:, None, :, :] + lic[:, None, :, :], -jnp.inf)
        mt = jnp.maximum(a, jnp.max(D, axis=2))
        w_inter = jnp.exp(a - mt)
        s = jnp.einsum('bthd,bshd->btsh', qc, kc) * jnp.exp(D - mt[:, :, None, :])
        num = w_inter[..., None] * jnp.einsum('bthd,bhde->bthe', qc, C) + jnp.einsum('btsh,bshe->bthe', s, vc)
        den = w_inter * jnp.einsum('bthd,bhd->bth', qc, n) + jnp.sum(s, axis=2)
        h = num / jnp.maximum(jnp.abs(den), jnp.exp(-mt))[..., None]
        mL = mt[:, -1]
        g_inter = jnp.exp(a[:, -1] - mL)
        g_s = jnp.exp(b[:, -1:] - b + lic - mL[:, None])
        C_new = g_inter[..., None, None] * C + jnp.einsum('bsh,bshd,bshe->bhde', g_s, kc, vc)
        n_new = g_inter[..., None] * n + jnp.einsum('bsh,bshd->bhd', g_s, kc)
        return (C_new, n_new, mL), h

    carry0 = (C0.astype(f32), n0.astype(f32), m0.astype(f32))
    (C, n, m), h = lax.scan(step, carry0, (chunks(q), chunks(k), chunks(v), chunks(li), chunks(lf)))
    h = jnp.moveaxis(h, 0, 1).reshape(B, T, H, -1)
    return h, C, n, m


def _t5_bucket(dist):
    n = jnp.maximum(dist, 0)
    exact = REL_BUCKETS // 2
    large = exact + (jnp.log(jnp.maximum(n, 1).astype(jnp.float32) / exact)
                     / math.log(REL_MAX_DIST / exact) * (REL_BUCKETS - exact)).astype(jnp.int32)
    return jnp.where(n < exact, n, jnp.minimum(large, REL_BUCKETS - 1))


def _blocks(a):
    T, H, dh = a.shape
    nb = -(-T // MOBA_BLOCK)
    a = jnp.pad(a, ((0, nb * MOBA_BLOCK - T), (0, 0), (0, 0)))
    return a.reshape(nb, MOBA_BLOCK, H, dh).transpose(2, 0, 1, 3)


def _moba_core(q, pos_q, kb, vb, kmean, rel_bias):
    Tq = q.shape[0]
    H, NB, BLK, dh = kb.shape
    own = pos_q // BLK
    h_idx = jnp.arange(H)[None, :, None]
    own_idx = jnp.broadcast_to(own[:, None, None], (Tq, H, 1))
    k_sel = min(MOBA_TOPK, NB - 1)
    if k_sel > 0:
        scores = jnp.einsum('qhd,hbd->qhb', q.astype(jnp.float32), kmean)
        fully_past = jnp.arange(NB)[None, None, :] < own[:, None, None]
        scores = jnp.where(fully_past, scores, -jnp.inf)
        top_s, top_i = lax.top_k(scores, k_sel)
        blk_idx = jnp.concatenate([top_i, own_idx], axis=-1)
        blk_ok = jnp.concatenate([top_s > -jnp.inf, jnp.ones((Tq, H, 1), bool)], axis=-1)
    else:
        blk_idx = own_idx
        blk_ok = jnp.ones((Tq, H, 1), bool)
    k_g = kb[h_idx, blk_idx]
    v_g = vb[h_idx, blk_idx]
    pos_k = blk_idx[..., None] * BLK + jnp.arange(BLK)
    dist = pos_q[:, None, None, None] - pos_k
    ok = blk_ok[..., None] & (dist >= 0)
    bias = rel_bias[_t5_bucket(dist), jnp.arange(H)[None, :, None, None]]
    logits = (jnp.einsum('qhd,qhjsd->qhjs', q, k_g).astype(jnp.float32) * (dh ** -0.5)
              + bias.astype(jnp.float32))
    logits = jnp.where(ok, logits, -jnp.inf)
    p = jax.nn.softmax(logits.reshape(Tq, H, -1), axis=-1).reshape(logits.shape)
    return jnp.einsum('qhjs,qhjsd->qhd', p.astype(v_g.dtype), v_g)


def setup_inputs(seed: int = 0) -> dict:
    key = jax.random.key(seed)
    ks = iter(jax.random.split(key, 40))
    f32 = jnp.float32

    def nrm(shape, scale=1.0):
        return jax.random.normal(next(ks), shape, f32) * scale

    n_pages = PAST_LEN // PAGE_SIZE
    used = DEC_BATCH * n_pages
    n_pool = used + max(1, used // 4)
    page_table = jax.random.permutation(next(ks), n_pool)[:used].reshape(DEC_BATCH, n_pages).astype(jnp.int32)
    b_gates = jnp.concatenate([nrm((N_EVEN, H_A), 0.1),
                               jnp.linspace(3.0, 6.0, H_A, dtype=f32)[None] + nrm((N_EVEN, H_A), 0.1)], axis=-1)
    return {
        'x_prompt': nrm((BATCH, SEQ, D_MODEL)),
        'x_sample': nrm((DEC_BATCH, DEC_SEQ, D_MODEL)),
        'state_mlstm_c': nrm((N_EVEN, DEC_BATCH, H_A, DK_A, DV_A), 0.5),
        'state_mlstm_n': nrm((N_EVEN, DEC_BATCH, H_A, DK_A), 0.5),
        'state_mlstm_m': nrm((N_EVEN, DEC_BATCH, H_A), 0.5),
        'state_conv': nrm((N_EVEN, DEC_BATCH, CONV_W - 1, B_W), 0.5),
        'cache_k': nrm((N_ODD, n_pool, PAGE_SIZE, H_C, DH_C)),
        'cache_v': nrm((N_ODD, n_pool, PAGE_SIZE, H_C, DH_C)),
        'page_table': page_table,
        'state_ffn': nrm((DEPTH, DEC_BATCH, FFN_CONV_W - 1, D_FF)),
        'c_prompt': nrm((BATCH, D_MODEL)),
        'c_sample': nrm((DEC_BATCH, D_MODEL)),
        'ada_w': nrm((DEPTH, 2, D_MODEL, 3 * D_MODEL), 0.5 * D_MODEL ** -0.5),
        'ada_b': nrm((DEPTH, 2, 3 * D_MODEL), 0.02),
        'w_in_even': nrm((N_EVEN, D_MODEL, EVEN_IN), D_MODEL ** -0.5),
        'b_gates': b_gates,
        'mlstm_norm_g': 1.0 + nrm((N_EVEN, A_W), 0.02),
        'conv_w': nrm((N_EVEN, CONV_W, B_W), CONV_W ** -0.5),
        'conv_b': nrm((N_EVEN, B_W), 0.02),
        'conv_ln_g': 1.0 + nrm((N_EVEN, B_W), 0.02),
        'conv_ln_b': nrm((N_EVEN, B_W), 0.02),
        'w_out_even': nrm((N_EVEN, A_W + B_W, D_MODEL), (A_W + B_W) ** -0.5),
        'w_in_odd': nrm((N_ODD, D_MODEL, 3 * H_C * DH_C), D_MODEL ** -0.5),
        'q_norm_g': 1.0 + nrm((N_ODD, DH_C), 0.02),
        'k_norm_g': 1.0 + nrm((N_ODD, DH_C), 0.02),
        'rel_bias': nrm((REL_BUCKETS, H_C), 0.1),
        'w_out_odd': nrm((N_ODD, H_C * DH_C, D_MODEL), (H_C * DH_C) ** -0.5),
        'ffn_w_up': nrm((DEPTH, D_MODEL, 2 * D_FF), D_MODEL ** -0.5),
        'ffn_conv_w': nrm((DEPTH, FFN_CONV_W, D_FF), FFN_CONV_W ** -0.5),
        'ffn_conv_b': nrm((DEPTH, D_FF), 0.02),
        'ffn_w_down': nrm((DEPTH, D_FF, D_MODEL), D_FF ** -0.5),
    }


def reference(x_prompt, x_sample, state_mlstm_c, state_mlstm_n, state_mlstm_m, state_conv, cache_k, cache_v,
              page_table, state_ffn, c_prompt, c_sample, ada_w, ada_b, w_in_even, b_gates, mlstm_norm_g,
              conv_w, conv_b, conv_ln_g, conv_ln_b, w_out_even, w_in_odd, q_norm_g, k_norm_g, rel_bias,
              w_out_odd, ffn_w_up, ffn_conv_w, ffn_conv_b, ffn_w_down):

    def modulated_norm(x, c, l, j):
        mod = jax.nn.silu(c) @ ada_w[l, j] + ada_b[l, j]
        shift, scale, gate = jnp.split(mod, 3, axis=-1)
        return _rms(x) * (1.0 + scale[:, None]) + shift[:, None], gate[:, None]

    def even_mixer(h, e, C0, n0, m0, cbuf):
        B, T, _ = h.shape
        z = h @ w_in_even[e]
        o1 = QK_W; o2 = 2 * QK_W; o3 = o2 + A_W; o4 = o3 + A_W; o5 = o4 + 2 * H_A
        zq, zk, zv, zo, zg, zb = jnp.split(z, [o1, o2, o3, o4, o5], axis=-1)
        gates = zg + b_gates[e]
        hA, C, n, m = _mlstm(zq.reshape(B, T, H_A, DK_A), zk.reshape(B, T, H_A, DK_A),
                             zv.reshape(B, T, H_A, DV_A), gates[..., :H_A], gates[..., H_A:], C0, n0, m0)
        hA = (_rms(hA).reshape(B, T, A_W) * mlstm_norm_g[e] * jax.nn.sigmoid(zo)).astype(h.dtype)
        a, g = jnp.split(zb, 2, axis=-1)
        u = a * jax.nn.sigmoid(g)
        uc, cbuf_new = _causal_dwconv(u, cbuf, conv_w[e], conv_b[e])
        yB = jax.nn.silu(_layernorm(uc, conv_ln_g[e], conv_ln_b[e]))
        y = jnp.concatenate([hA, yB], axis=-1) @ w_out_even[e]
        return y, C.astype(C0.dtype), n.astype(n0.dtype), m.astype(m0.dtype), cbuf_new

    def attn_qkv(h, o):
        B, T, _ = h.shape
        q, k, v = jnp.split(h @ w_in_odd[o], 3, axis=-1)
        q = _rms(q.reshape(B, T, H_C, DH_C)) * q_norm_g[o]
        k = _rms(k.reshape(B, T, H_C, DH_C)) * k_norm_g[o]
        return q, k, v.reshape(B, T, H_C, DH_C)

    def prompt_attend(o, q, k, v):
        B, S, H, dh = q.shape
        qc = MOBA_QCHUNK if S % MOBA_QCHUNK == 0 else S
        nq = S // qc
        pos = jnp.arange(S, dtype=jnp.int32).reshape(nq, qc)

        def per_seq(args):
            qs, ks_, vs = args
            kb = _blocks(ks_); vb = _blocks(vs)
            kmean = jnp.mean(kb.astype(jnp.float32), axis=2)
            out = lax.map(lambda a: _moba_core(a[0], a[1], kb, vb, kmean, rel_bias),
                          (qs.reshape(nq, qc, H, dh), pos))
            return out.reshape(S, H, dh)

        return lax.map(per_seq, (q, k, v))

    def sample_attend(o, q, k, v):
        DB, T, H, dh = q.shape
        past_len = page_table.shape[1] * cache_k.shape[2]
        pos = past_len + jnp.arange(T, dtype=jnp.int32)

        def per_seq(args):
            qs, ks_, vs, pt = args
            kp = cache_k[o, pt].reshape(-1, H, dh)
            vp = cache_v[o, pt].reshape(-1, H, dh)
            kb = _blocks(jnp.concatenate([kp.astype(ks_.dtype), ks_], axis=0))
            vb = _blocks(jnp.concatenate([vp.astype(vs.dtype), vs], axis=0))
            kmean = jnp.mean(kb.astype(jnp.float32), axis=2)
            return _moba_core(qs, pos, kb, vb, kmean, rel_bias)

        return lax.map(per_seq, (q, k, v, page_table))

    def conv_ffn(h, l, buf):
        g, u = jnp.split(h @ ffn_w_up[l], 2, axis=-1)
        gc, buf_new = _causal_dwconv(g, buf, ffn_conv_w[l], ffn_conv_b[l])
        return (jax.nn.silu(gc) * u) @ ffn_w_down[l], buf_new

    def trunk(x, c, mc, mn, mm, cb, fb, attend):
        out_c, out_n, out_m, out_conv, out_k, out_v, out_ffn = [], [], [], [], [], [], []
        for l in range(DEPTH):
            h, gate = modulated_norm(x, c, l, 0)
            if l % 2 == 0:
                e = l // 2
                y, C, n, m, cbn = even_mixer(h, e, mc[e], mn[e], mm[e], cb[e])
                out_c.append(C); out_n.append(n); out_m.append(m); out_conv.append(cbn)
            else:
                o = l // 2
                q, k, v = attn_qkv(h, o)
                a = attend(o, q, k, v)
                y = a.reshape(h.shape[0], h.shape[1], -1) @ w_out_odd[o]
                out_k.append(k); out_v.append(v)
            x = x + gate * y
            h, gate = modulated_norm(x, c, l, 1)
            y, fbn = conv_ffn(h, l, fb[l])
            out_ffn.append(fbn)
            x = x + gate * y
        return (x, jnp.stack(out_c), jnp.stack(out_n), jnp.stack(out_m), jnp.stack(out_conv),
                jnp.stack(out_k), jnp.stack(out_v), jnp.stack(out_ffn))

    B = x_prompt.shape[0]
    dt = x_prompt.dtype
    zc = jnp.zeros((N_EVEN, B, H_A, DK_A, DV_A), dt)
    zn = jnp.zeros((N_EVEN, B, H_A, DK_A), dt)
    zm = jnp.zeros((N_EVEN, B, H_A), dt)
    zconv = jnp.zeros((N_EVEN, B, CONV_W - 1, B_W), dt)
    zffn = jnp.zeros((DEPTH, B, FFN_CONV_W - 1, D_FF), dt)

    y_prompt, p_c, p_n, p_m, p_conv, p_k, p_v, p_ffn = trunk(
        x_prompt, c_prompt, zc, zn, zm, zconv, zffn, prompt_attend)
    y_sample, s_c, s_n, s_m, s_conv, s_k, s_v, s_ffn = trunk(
        x_sample, c_sample, state_mlstm_c, state_mlstm_n, state_mlstm_m, state_conv, state_ffn, sample_attend)

    return (y_prompt, y_sample, p_c, p_n, p_m, p_conv, p_k, p_v, p_ffn,
            s_c, s_n, s_m, s_conv, s_k, s_v, s_ffn)
```

```python
import functools
import math

import jax
import jax.numpy as jnp
from jax import lax
from jax.experimental import pallas as pl
from jax.experimental.pallas import tpu as pltpu

F32 = jnp.float32
BF16 = jnp.bfloat16
HIGHEST = lax.Precision.HIGHEST

LANES = 128
SUBLANES = 8
VMEM_LIMIT_BYTES = 56 * 1024 * 1024

EPS = 1e-6
LN_EPS = 1e-5
MLSTM_CHUNK = 256
MLSTM_MIN_CHUNK = 128
MOBA_BLOCK = 256
MOBA_TOPK = 3
REL_BUCKETS = 32
REL_MAX_DIST = 128
NEG = -1e30
CONV_HALO = 32

_NT = (((1,), (1,)), ((), ()))
_TN = (((0,), (0,)), ((), ()))


def _cparams(n_axes):
    return pltpu.CompilerParams(dimension_semantics=("arbitrary",) * n_axes, vmem_limit_bytes=VMEM_LIMIT_BYTES)


def _sigmoid(x):
    return jax.nn.sigmoid(x)


def _col_tile(n, pref):
    tile = min(pref, n)
    while n % tile:
        tile -= LANES
    return tile


def _row_tiling(bn, t, tm):
    if t >= tm:
        assert t % tm == 0
        tpb = t // tm
        return 1, tm, bn * tpb, tpb
    gb = min(bn, tm // t)
    assert bn % gb == 0 and t % SUBLANES == 0
    return gb, t, bn // gb, 1


def _mod_kernel(c_ref, w_ref, b_ref, o_ref):
    c = c_ref[...]
    s = (c * _sigmoid(c)).astype(BF16)
    o_ref[...] = jnp.dot(s, w_ref[...].astype(BF16), preferred_element_type=F32) + b_ref[...]


def _adaln_mods(c_all, ada_w, ada_b, tn=512):
    depth, two, d, n3 = ada_w.shape
    nlj = depth * two
    rows = c_all.shape[0]
    tn = _col_tile(n3, tn)
    return pl.pallas_call(
        _mod_kernel,
        grid=(nlj, n3 // tn),
        in_specs=[pl.BlockSpec((rows, d), lambda l, j: (0, 0)),
                  pl.BlockSpec((None, d, tn), lambda l, j: (l, 0, j)),
                  pl.BlockSpec((None, 1, tn), lambda l, j: (l, 0, j))],
        out_specs=pl.BlockSpec((None, rows, tn), lambda l, j: (l, 0, j)),
        out_shape=jax.ShapeDtypeStruct((nlj, rows, n3), F32),
        compiler_params=_cparams(2), name="adaln_mods",
    )(c_all, ada_w.reshape(nlj, d, n3), ada_b.reshape(nlj, 1, n3))


def _nm_matmul_kernel(x_ref, sh_ref, sc_ref, w_ref, *rest, headnorm):
    if headnorm:
        g_ref, o_ref, h_ref = rest
    else:
        o_ref, h_ref = rest

    @pl.when(pl.program_id(1) == 0)
    def _():
        x = x_ref[...]
        r = lax.rsqrt(jnp.mean(x * x, axis=-1, keepdims=True) + EPS)
        h = (x * r) * (1.0 + sc_ref[...]) + sh_ref[...]
        h_ref[...] = h.reshape(h_ref.shape).astype(BF16)

    acc = jnp.dot(h_ref[...], w_ref[...], preferred_element_type=F32)
    if headnorm:
        g = g_ref[...]
        for s in range(acc.shape[1] // LANES):
            y = acc[:, s * LANES:(s + 1) * LANES]
            y = y * lax.rsqrt(jnp.mean(y * y, axis=-1, keepdims=True) + EPS) * g
            o_ref[:, s * LANES:(s + 1) * LANES] = y.astype(o_ref.dtype)
    else:
        o_ref[...] = acc.astype(o_ref.dtype)


def _nm_matmul(x3, shift, scale, w, *, n_out, col_off=0, tm=512, tn=512, headnorm_g=None, name="nm_matmul"):
    bn, t, d = x3.shape
    gb, rows, n_i, tpb = _row_tiling(bn, t, tm)
    tmm = gb * rows
    tn = _col_tile(math.gcd(n_out, col_off) if col_off else n_out, tn)
    joff = col_off // tn
    in_specs = [pl.BlockSpec((gb, rows, d), lambda i, j: (i // tpb, i % tpb, 0)),
                pl.BlockSpec((gb, 1, d), lambda i, j: (i // tpb, 0, 0)),
                pl.BlockSpec((gb, 1, d), lambda i, j: (i // tpb, 0, 0)),
                pl.BlockSpec((d, tn), lambda i, j: (0, j + joff))]
    args = [x3, shift, scale, w]
    if headnorm_g is not None:
        in_specs.append(pl.BlockSpec((1, LANES), lambda i, j: (0, 0)))
        args.append(headnorm_g.reshape(1, LANES))
    return pl.pallas_call(
        functools.partial(_nm_matmul_kernel, headnorm=headnorm_g is not None),
        grid=(n_i, n_out // tn),
        in_specs=in_specs,
        out_specs=pl.BlockSpec((tmm, tn), lambda i, j: (i, j)),
        out_shape=jax.ShapeDtypeStruct((bn * t, n_out), F32),
        scratch_shapes=[pltpu.VMEM((tmm, d), BF16)],
        compiler_params=_cparams(2), name=name,
    )(*args)


def _mm_res_kernel(*refs, n_in):
    a_refs, w_refs = refs[:n_in], refs[n_in:2 * n_in]
    x_ref, gate_ref, o_ref = refs[2 * n_in:]
    acc = None
    for a_ref, w_ref in zip(a_refs, w_refs):
        part = jnp.dot(a_ref[...].astype(BF16), w_ref[...], preferred_element_type=F32)
        acc = part if acc is None else acc + part
    x = x_ref[...]
    o_ref[...] = x + gate_ref[...] * acc.reshape(x.shape)


def _mm_res(a_list, w_list, x3, gate, *, tm=512, tn=512, name="mm_res"):
    bn, t, d = x3.shape
    gb, rows, n_i, tpb = _row_tiling(bn, t, tm)
    tmm = gb * rows
    tn = _col_tile(d, tn)
    n_in = len(a_list)
    in_specs = ([pl.BlockSpec((tmm, a.shape[1]), lambda i, j: (i, 0)) for a in a_list]
                + [pl.BlockSpec((w.shape[0], tn), lambda i, j: (0, j)) for w in w_list]
                + [pl.BlockSpec((gb, rows, tn), lambda i, j: (i // tpb, i % tpb, j)),
                   pl.BlockSpec((gb, 1, tn), lambda i, j: (i // tpb, 0, j))])
    return pl.pallas_call(
        functools.partial(_mm_res_kernel, n_in=n_in),
        grid=(n_i, d // tn),
        in_specs=in_specs,
        out_specs=pl.BlockSpec((gb, rows, tn), lambda i, j: (i // tpb, i % tpb, j)),
        out_shape=jax.ShapeDtypeStruct((bn, t, d), F32),
        compiler_params=_cparams(2), name=name,
    )(*a_list, *w_list, x3, gate)


def _mlstm_kernel(q_ref, k_ref, v_ref, zo_ref, g_ref, bg_ref, gn_ref, c0_ref, n0_ref, m0_ref,
                  h_ref, co_ref, no_ref, mo_ref, c_s, n_s, m_s, *, tb, chunk, n_heads):
    hd = pl.program_id(1)
    ck = pl.program_id(2)
    L = chunk
    dk = q_ref.shape[-1]

    @pl.when(ck == 0)
    def _():
        c_s[...] = c0_ref[0, 0]
        n_s[...] = n0_ref[0, 0]
        m_s[...] = m0_ref[0, 0]

    def pad_rows(x):
        if tb == L:
            return x
        return jnp.concatenate([x, jnp.zeros((L - tb, x.shape[1]), x.dtype)], axis=0)

    q = pad_rows(q_ref[0]) * (dk ** -0.5)
    k = pad_rows(k_ref[0])
    v = pad_rows(v_ref[0])
    gates = pad_rows(g_ref[0] + bg_ref[...])
    lane = lax.broadcasted_iota(jnp.int32, (L, LANES), 1)
    li = jnp.sum(jnp.where(lane == hd, gates, 0.0), axis=-1, keepdims=True)
    fp = jnp.sum(jnp.where(lane == hd + n_heads, gates, 0.0), axis=-1, keepdims=True)
    lf = -(jnp.maximum(-fp, 0.0) + jnp.log1p(jnp.exp(-jnp.abs(fp))))
    if tb != L:
        rowc = lax.broadcasted_iota(jnp.int32, (L, 1), 0)
        li = jnp.where(rowc < tb, li, NEG)
        lf = jnp.where(rowc < tb, lf, 0.0)

    ri = lax.broadcasted_iota(jnp.int32, (L, L), 0)
    ci = lax.broadcasted_iota(jnp.int32, (L, L), 1)
    tri = ri >= ci
    bmat = jnp.dot(tri.astype(F32), jnp.broadcast_to(lf, (L, LANES)), precision=HIGHEST, preferred_element_type=F32)
    b = bmat[:, :1]
    e0 = (lax.broadcasted_iota(jnp.int32, (SUBLANES, LANES), 1) == 0).astype(F32)
    r_row = lax.dot_general(e0, bmat - li, _NT, precision=HIGHEST, preferred_element_type=F32)[:1, :]

    a = b + m_s[:, :1]
    dmat = jnp.where(tri, b - r_row, NEG)
    mt = jnp.maximum(a, jnp.max(dmat, axis=-1, keepdims=True))
    w_inter = jnp.exp(a - mt)
    qb, kb, vb = q.astype(BF16), k.astype(BF16), v.astype(BF16)
    s = lax.dot_general(qb, kb, _NT, preferred_element_type=F32) * jnp.exp(dmat - mt)
    c_old = c_s[...]
    n_old = n_s[...]
    num = (w_inter * jnp.dot(qb, c_old.astype(BF16), preferred_element_type=F32)
           + jnp.dot(s.astype(BF16), vb, preferred_element_type=F32))
    den = w_inter * jnp.sum(q * n_old, axis=-1, keepdims=True) + jnp.sum(s, axis=-1, keepdims=True)
    hh = (num / jnp.maximum(jnp.abs(den), jnp.exp(-mt)))[:tb]
    hn = hh * lax.rsqrt(jnp.mean(hh * hh, axis=-1, keepdims=True) + EPS)
    h_ref[0] = (hn * gn_ref[...] * _sigmoid(zo_ref[0])).astype(h_ref.dtype)

    m_l = mt[L - 1:L]
    g_inter = jnp.exp(a[L - 1:L] - m_l)
    g_s = jnp.exp(b[L - 1:L] - b + li - m_l)
    c_new = g_inter * c_old + lax.dot_general(kb, (g_s * v).astype(BF16), _TN, preferred_element_type=F32)
    n_new = g_inter * n_old + jnp.sum(g_s * k, axis=0, keepdims=True)
    m_new = jnp.broadcast_to(m_l, m_s.shape)
    c_s[...] = c_new
    n_s[...] = n_new
    m_s[...] = m_new

    @pl.when(ck == pl.num_programs(2) - 1)
    def _():
        co_ref[0, 0] = c_new
        no_ref[0, 0] = n_new
        mo_ref[0, 0] = m_new


def _mlstm(z3, g3, bg, gn, c0, n0, m0, *, out_dtype, name):
    bn, t, _ = z3.shape
    _, n_heads, dk, dv = c0.shape
    assert dk == dv and dk % LANES == 0
    if t % MLSTM_CHUNK == 0:
        tb = chunk = MLSTM_CHUNK
    else:
        assert t <= MLSTM_MIN_CHUNK
        tb, chunk = t, MLSTM_MIN_CHUNK
    n_c = t // tb
    col = lambda sec: (lambda b, h, c: (b, c, sec * n_heads + h))
    state = lambda b, h, c: (b, h, 0, 0)
    h, c, n, m = pl.pallas_call(
        functools.partial(_mlstm_kernel, tb=tb, chunk=chunk, n_heads=n_heads),
        grid=(bn, n_heads, n_c),
        in_specs=[pl.BlockSpec((1, tb, dk), col(0)), pl.BlockSpec((1, tb, dk), col(1)),
                  pl.BlockSpec((1, tb, dv), col(2)), pl.BlockSpec((1, tb, dv), col(3)),
                  pl.BlockSpec((1, tb, LANES), lambda b, h, c: (b, c, 0)),
                  pl.BlockSpec((1, LANES), lambda b, h, c: (0, 0)),
                  pl.BlockSpec((1, dv), lambda b, h, c: (0, h)),
                  pl.BlockSpec((1, 1, dk, dv), state), pl.BlockSpec((1, 1, 1, dk), state),
                  pl.BlockSpec((1, 1, 1, LANES), state)],
        out_specs=[pl.BlockSpec((1, tb, dv), lambda b, h, c: (b, c, h)),
                   pl.BlockSpec((1, 1, dk, dv), state), pl.BlockSpec((1, 1, 1, dk), state),
                   pl.BlockSpec((1, 1, 1, LANES), state)],
        out_shape=[jax.ShapeDtypeStruct((bn, t, n_heads * dv), out_dtype),
                   jax.ShapeDtypeStruct((bn, n_heads, dk, dv), F32),
                   jax.ShapeDtypeStruct((bn, n_heads, 1, dk), F32),
                   jax.ShapeDtypeStruct((bn, n_heads, 1, LANES), F32)],
        scratch_shapes=[pltpu.VMEM((dk, dv), F32), pltpu.VMEM((1, dk), F32), pltpu.VMEM((1, LANES), F32)],
        compiler_params=_cparams(3), name=name,
    )(z3, z3, z3, z3, g3, bg, gn, c0, n0.reshape(bn, n_heads, 1, dk),
      jnp.broadcast_to(m0[:, :, None, None], (bn, n_heads, 1, LANES)))
    return h, c, n.reshape(bn, n_heads, dk), m[:, :, 0, 0]


def _convmod_kernel(a_ref, g_ref, buf_ref, w_ref, b_ref, lg_ref, lb_ref, y_ref, nb_ref, u_s, *, tt, width, rc):
    t = pl.program_id(1)
    hist = width - 1
    lo = CONV_HALO - hist

    @pl.when(t == 0)
    def _():
        u_s[lo:CONV_HALO, :] = buf_ref[0]

    @pl.when(t > 0)
    def _():
        u_s[0:CONV_HALO, :] = u_s[tt:tt + CONV_HALO, :]

    u_s[CONV_HALO:CONV_HALO + tt, :] = a_ref[0] * _sigmoid(g_ref[0])
    bias = b_ref[...]
    lg = lg_ref[...]
    lb = lb_ref[...]
    for r0 in range(0, tt, rc):
        acc = jnp.broadcast_to(bias, (rc, bias.shape[1]))
        for j in range(width):
            acc = acc + w_ref[j:j + 1, :] * u_s[lo + r0 + j:lo + r0 + j + rc, :]
        mu = jnp.mean(acc, axis=-1, keepdims=True)
        xc = acc - mu
        var = jnp.mean(xc * xc, axis=-1, keepdims=True)
        y = xc * lax.rsqrt(var + LN_EPS) * lg + lb
        y_ref[0, r0:r0 + rc, :] = (y * _sigmoid(y)).astype(y_ref.dtype)

    @pl.when(t == pl.num_programs(1) - 1)
    def _():
        nb_ref[0] = u_s[tt + lo:tt + CONV_HALO, :]


def _conv_module(z3, col_a, col_g, cbuf, conv_w, conv_b, ln_g, ln_b, *, out_dtype, name):
    bn, t, _ = z3.shape
    width, ch = conv_w.shape
    assert width - 1 <= CONV_HALO
    tt = 128 if t % 128 == 0 else t
    n_t = t // tt
    assert n_t == 1 or tt >= CONV_HALO
    y, nb = pl.pallas_call(
        functools.partial(_convmod_kernel, tt=tt, width=width,
                          rc=2 * SUBLANES if tt % (2 * SUBLANES) == 0 else SUBLANES),
        grid=(bn, n_t),
        in_specs=[pl.BlockSpec((1, tt, ch), lambda b, i: (b, i, col_a)),
                  pl.BlockSpec((1, tt, ch), lambda b, i: (b, i, col_g)),
                  pl.BlockSpec((1, width - 1, ch), lambda b, i: (b, 0, 0)),
                  pl.BlockSpec((width, ch), lambda b, i: (0, 0)),
                  pl.BlockSpec((1, ch), lambda b, i: (0, 0)),
                  pl.BlockSpec((1, ch), lambda b, i: (0, 0)),
                  pl.BlockSpec((1, ch), lambda b, i: (0, 0))],
        out_specs=[pl.BlockSpec((1, tt, ch), lambda b, i: (b, i, 0)),
                   pl.BlockSpec((1, width - 1, ch), lambda b, i: (b, 0, 0))],
        out_shape=[jax.ShapeDtypeStruct((bn, t, ch), out_dtype),
                   jax.ShapeDtypeStruct((bn, width - 1, ch), F32)],
        scratch_shapes=[pltpu.VMEM((CONV_HALO + tt, ch), F32)],
        compiler_params=_cparams(2), name=name,
    )(z3, z3, cbuf, conv_w, conv_b.reshape(1, ch), ln_g.reshape(1, ch), ln_b.reshape(1, ch))
    return y, nb


def _ffn_down_kernel(g_ref, u_ref, halo_ref, buf_ref, cw_ref, cb_ref, wd_ref, x_ref, gate_ref, o_ref, acc_s,
                     *, tpb, rows):
    ti = pl.program_id(0) % tpb
    f = pl.program_id(1)
    g3 = g_ref[...]
    gb, _, tf = g3.shape
    g = g3.reshape(gb * rows, tf)
    n = gb * rows
    prev = buf_ref[...].reshape(gb * SUBLANES, tf)
    if tpb > 1:
        prev = jnp.where(ti == 0, prev, pltpu.roll(halo_ref[...].reshape(gb * SUBLANES, tf), 2, axis=0))
    r8 = lax.broadcasted_iota(jnp.int32, (gb * SUBLANES, tf), 0) % SUBLANES
    prev1 = pltpu.roll(prev, gb * SUBLANES - 1, axis=0)
    gm1 = pltpu.roll(g, 1, axis=0)
    gm2 = pltpu.roll(g, 2, axis=0)
    if rows == SUBLANES:
        gm1 = jnp.where(r8 < 1, prev1, gm1)
        gm2 = jnp.where(r8 < 2, prev, gm2)
    else:
        gm1 = jnp.concatenate([jnp.where(r8 < 1, prev1, gm1[:SUBLANES]), gm1[SUBLANES:]], axis=0)
        gm2 = jnp.concatenate([jnp.where(r8 < 2, prev, gm2[:SUBLANES]), gm2[SUBLANES:]], axis=0)
    gc = cw_ref[0:1, :] * gm2 + cw_ref[1:2, :] * gm1 + cw_ref[2:3, :] * g + cb_ref[...]
    act = (gc * _sigmoid(gc)) * u_ref[...].reshape(n, tf)
    part = jnp.dot(act.astype(BF16), wd_ref[...], preferred_element_type=F32)

    @pl.when(f == 0)
    def _():
        acc_s[...] = part

    @pl.when(f > 0)
    def _():
        acc_s[...] += part

    @pl.when(f == pl.num_programs(1) - 1)
    def _():
        x = x_ref[...]
        o_ref[...] = x + gate_ref[...] * acc_s[...].reshape(x.shape)


def _ffn_down(gu3, buf8, conv_w, conv_b, wd, x3, gate, *, tm=512, tf=512, name="ffn_down"):
    bn, t, d = x3.shape
    dff = wd.shape[0]
    gb, rows, n_i, tpb = _row_tiling(bn, t, tm)
    tf = _col_tile(dff, tf)
    n_f = dff // tf
    assert dff % tf == 0 and conv_w.shape[0] == 3
    hpb = rows // SUBLANES
    row = lambda i, f: (i // tpb, i % tpb, f)
    return pl.pallas_call(
        functools.partial(_ffn_down_kernel, tpb=tpb, rows=rows),
        grid=(n_i, n_f),
        in_specs=[pl.BlockSpec((gb, rows, tf), row),
                  pl.BlockSpec((gb, rows, tf), lambda i, f: (i // tpb, i % tpb, f + n_f)),
                  pl.BlockSpec((gb, SUBLANES, tf), lambda i, f: (i // tpb, jnp.maximum((i % tpb) * hpb - 1, 0), f)),
                  pl.BlockSpec((gb, SUBLANES, tf), lambda i, f: (i // tpb, 0, f)),
                  pl.BlockSpec((3, tf), lambda i, f: (0, f)),
                  pl.BlockSpec((1, tf), lambda i, f: (0, f)),
                  pl.BlockSpec((tf, d), lambda i, f: (f, 0)),
                  pl.BlockSpec((gb, rows, d), lambda i, f: (i // tpb, i % tpb, 0)),
                  pl.BlockSpec((gb, 1, d), lambda i, f: (i // tpb, 0, 0))],
        out_specs=pl.BlockSpec((gb, rows, d), lambda i, f: (i // tpb, i % tpb, 0)),
        out_shape=jax.ShapeDtypeStruct((bn, t, d), F32),
        scratch_shapes=[pltpu.VMEM((gb * rows, d), F32)],
        compiler_params=_cparams(2), name=name,
    )(gu3, gu3, gu3, buf8, conv_w, conv_b.reshape(1, dff), wd, x3, gate)


def _t5_bucket(dist):
    n = jnp.maximum(dist, 0)
    exact = REL_BUCKETS // 2
    large = exact + (jnp.log(jnp.maximum(n, 1).astype(F32) / exact)
                     / math.log(REL_MAX_DIST / exact) * (REL_BUCKETS - exact)).astype(jnp.int32)
    return jnp.where(n < exact, n, jnp.minimum(large, REL_BUCKETS - 1))


def _bias_lookup(dist, table_col):
    bucket = _t5_bucket(dist)
    bias = jnp.zeros(dist.shape, F32)
    for kk in range(REL_BUCKETS):
        bias = jnp.where(bucket == kk, table_col(kk), bias)
    return bias


def _bias_prompt_kernel(rb_ref, o_ref, *, blk):
    rb = rb_ref[0]
    r = lax.broadcasted_iota(jnp.int32, (blk, blk), 0)
    c = lax.broadcasted_iota(jnp.int32, (blk, blk), 1)
    col = lambda kk: rb[:, kk:kk + 1]
    own = _bias_lookup(r - c, col)
    o_ref[0, 0] = jnp.where(r >= c, own, NEG)
    o_ref[0, 1] = _bias_lookup(blk + r - c, col)


def _bias_prompt(rel_bias, blk):
    n_b, n_h = rel_bias.shape
    rbt = jnp.pad(rel_bias.T, ((0, 0), (0, LANES - n_b))).reshape(n_h, 1, LANES)
    return pl.pallas_call(
        functools.partial(_bias_prompt_kernel, blk=blk),
        grid=(n_h,),
        in_specs=[pl.BlockSpec((1, 1, LANES), lambda h: (h, 0, 0))],
        out_specs=pl.BlockSpec((1, 2, blk, blk), lambda h: (h, 0, 0, 0)),
        out_shape=jax.ShapeDtypeStruct((n_h, 2, blk, blk), F32),
        compiler_params=_cparams(1), name="bias_prompt",
    )(rbt), rbt


def _bias_sample_kernel(rb_ref, far_ref, last_ref, new_ref, *, n_heads, page):
    rb = rb_ref[...]
    col = lambda kk: rb[:, kk:kk + 1]
    rows = rb.shape[0]
    for ref, base in ((far_ref, 2 * page), (last_ref, page), (new_ref, 0)):
        shape = ref.shape
        r = lax.broadcasted_iota(jnp.int32, shape, 0)
        c = lax.broadcasted_iota(jnp.int32, shape, 1)
        dist = base + r // n_heads - c // n_heads
        ok = (r % n_heads == c % n_heads) & (dist >= 0)
        ref[...] = jnp.where(ok, _bias_lookup(dist, col), NEG)


def _bias_sample(rel_bias, t, page):
    n_b, n_h = rel_bias.shape
    rows = t * n_h
    rb = jnp.pad(jnp.tile(rel_bias.T, (t, 1)), ((0, 0), (0, LANES - n_b)))
    return pl.pallas_call(
        functools.partial(_bias_sample_kernel, n_heads=n_h, page=page),
        out_shape=[jax.ShapeDtypeStruct((rows, page * n_h), F32),
                   jax.ShapeDtypeStruct((rows, page * n_h), F32),
                   jax.ShapeDtypeStruct((rows, rows), F32)],
        compiler_params=pltpu.CompilerParams(vmem_limit_bytes=VMEM_LIMIT_BYTES), name="bias_sample",
    )(rb)


def _attn_prompt_kernel(q_ref, k_ref, v_ref, bias_ref, rb_ref, o_ref, km_s, m_s, l_s, acc_s, *, blk, n_blk, scale):
    i = pl.program_id(2)

    @pl.when(i == 0)
    def _():
        km_s[...] = jnp.zeros_like(km_s)
        for j in range(n_blk):
            km_s[j:j + 1, :] = jnp.mean(k_ref[0, j * blk:(j + 1) * blk, :], axis=0, keepdims=True)

    qf = q_ref[0]
    qb = qf.astype(BF16)
    sc = lax.dot_general(qf, km_s[...], _NT, precision=HIGHEST, preferred_element_type=F32)
    lane = lax.broadcasted_iota(jnp.int32, sc.shape, 1)
    lanef = lane.astype(F32)
    sc = jnp.where(lane < i, sc, -jnp.inf)
    picks = []
    for _ in range(MOBA_TOPK):
        top = jnp.max(sc, axis=-1, keepdims=True)
        first = jnp.min(jnp.where(sc == top, lanef, float(LANES)), axis=-1, keepdims=True)
        picks.append(jnp.where(top > -jnp.inf, first, -1.0))
        sc = jnp.where(lanef == first, -jnp.inf, sc)

    def logits(start):
        kb = k_ref[0, pl.ds(start, blk), :].astype(BF16)
        return lax.dot_general(qb, kb, _NT, preferred_element_type=F32) * scale

    def pv(p, start):
        return jnp.dot(p.astype(BF16), v_ref[0, pl.ds(start, blk), :].astype(BF16), preferred_element_type=F32)

    own = pl.multiple_of(i * blk, blk)
    s = logits(own) + bias_ref[0, 0]
    m0 = jnp.max(s, axis=-1, keepdims=True)
    p0 = jnp.exp(s - m0)
    m_s[...] = m0
    l_s[...] = jnp.sum(p0, axis=-1, keepdims=True)
    acc_s[...] = pv(p0, own)

    far = rb_ref[0][:, REL_BUCKETS - 1:REL_BUCKETS]

    def visit(j, tile):
        jf = j.astype(F32)
        chosen = (picks[0] == jf) | (picks[1] == jf) | (picks[2] == jf)
        start = pl.multiple_of(j * blk, blk)
        if tile is None:
            sj = logits(start) + jnp.where(chosen, far, NEG)
        else:
            sj = logits(start) + tile + jnp.where(chosen, 0.0, NEG)
        m_old = m_s[...]
        m_new = jnp.maximum(m_old, jnp.max(sj, axis=-1, keepdims=True))
        alpha = jnp.exp(m_old - m_new)
        pj = jnp.exp(sj - m_new)
        l_s[...] = alpha * l_s[...] + jnp.sum(pj, axis=-1, keepdims=True)
        acc_s[...] = alpha * acc_s[...] + pv(pj, start)
        m_s[...] = m_new

    @pl.when(i >= 1)
    def _():
        visit(i - 1, bias_ref[0, 1])

    def body(j, carry):
        visit(j, None)
        return carry

    lax.fori_loop(0, i - 1, body, 0)
    o_ref[0] = (acc_s[...] / l_s[...]).astype(o_ref.dtype)


def _attn_prompt(q3, k3, v3, bias_tiles, rbt, *, out_dtype):
    assert MOBA_TOPK == 3
    bn, s, _ = q3.shape
    n_h = bias_tiles.shape[0]
    dh = q3.shape[2] // n_h
    blk = MOBA_BLOCK
    assert s % blk == 0 and dh == LANES and blk >= REL_MAX_DIST
    n_blk = s // blk
    return pl.pallas_call(
        functools.partial(_attn_prompt_kernel, blk=blk, n_blk=n_blk, scale=dh ** -0.5),
        grid=(bn, n_h, n_blk),
        in_specs=[pl.BlockSpec((1, blk, dh), lambda b, h, i: (b, i, h)),
                  pl.BlockSpec((1, s, dh), lambda b, h, i: (b, 0, h)),
                  pl.BlockSpec((1, s, dh), lambda b, h, i: (b, 0, h)),
                  pl.BlockSpec((1, 2, blk, blk), lambda b, h, i: (h, 0, 0, 0)),
                  pl.BlockSpec((1, 1, LANES), lambda b, h, i: (h, 0, 0))],
        out_specs=pl.BlockSpec((1, blk, dh), lambda b, h, i: (b, i, h)),
        out_shape=jax.ShapeDtypeStruct(q3.shape, out_dtype),
        scratch_shapes=[pltpu.VMEM((LANES, dh), F32), pltpu.VMEM((blk, 1), F32), pltpu.VMEM((blk, 1), F32),
                        pltpu.VMEM((blk, dh), F32)],
        compiler_params=_cparams(3), name="moba_prompt",
    )(q3, k3, v3, bias_tiles, rbt)


def _attn_sample_kernel(pt_ref, q_ref, kn_ref, vn_ref, kp_ref, vp_ref, far_ref, last_ref, new_ref, o_ref,
                        ks_s, m_s, l_s, o_s, *, n_heads, ppb, scale):
    del pt_ref
    p = pl.program_id(1)
    n_pages = pl.num_programs(1)
    qf = q_ref[0]
    qb = qf.astype(BF16)
    rows, dh = qf.shape

    def partial_softmax(kk, vv, bias):
        s = lax.dot_general(qb, kk.astype(BF16), _NT, preferred_element_type=F32) * scale + bias
        m = jnp.max(s, axis=-1, keepdims=True)
        e = jnp.exp(s - m)
        return m, jnp.sum(e, axis=-1, keepdims=True), jnp.dot(e.astype(BF16), vv.astype(BF16),
                                                              preferred_element_type=F32)

    def page_step(bias_ref):
        kp = kp_ref[0]
        ks_s[p] = jnp.sum(kp.reshape(kp.shape[0] // n_heads, n_heads, dh), axis=0)
        m, l, o = partial_softmax(kp, vp_ref[0], bias_ref[...])
        m_s[p] = m
        l_s[p] = l
        o_s[p] = o

    @pl.when(p < n_pages - 1)
    def _():
        page_step(far_ref)

    @pl.when(p == n_pages - 1)
    def _():
        page_step(last_ref)
        n_pg = m_s.shape[0]
        n_blk = n_pg // ppb
        inv = 1.0 / (ppb * (kp_ref.shape[1] // n_heads))
        scs = []
        for j in range(n_blk):
            km = ks_s[j * ppb]
            for t in range(1, ppb):
                km = km + ks_s[j * ppb + t]
            km = jnp.concatenate([km * inv] * (rows // n_heads), axis=0)
            scs.append(jnp.sum(qf * km, axis=-1, keepdims=True))
        sel = []
        for j in range(n_blk):
            rank = jnp.zeros((rows, 1), F32)
            for jj in range(n_blk):
                if jj != j:
                    ahead = (scs[jj] >= scs[j]) if jj < j else (scs[jj] > scs[j])
                    rank = rank + jnp.where(ahead, 1.0, 0.0)
            sel.append(rank < MOBA_TOPK)
        m_n, l_n, o_n = partial_softmax(kn_ref[0], vn_ref[0], new_ref[...])
        m_all = m_n
        for pg in range(n_pg):
            m_all = jnp.maximum(m_all, jnp.where(sel[pg // ppb], m_s[pg], NEG))
        w_n = jnp.exp(m_n - m_all)
        l_all = w_n * l_n
        o_all = w_n * o_n
        for pg in range(n_pg):
            w = jnp.where(sel[pg // ppb], jnp.exp(m_s[pg] - m_all), 0.0)
            l_all = l_all + w * l_s[pg]
            o_all = o_all + w * o_s[pg]
        o_ref[0] = (o_all / l_all).astype(o_ref.dtype)


def _attn_sample(q, k, v, cache_k, cache_v, page_table, far, last, new, *, out_dtype):
    db, t, hd = q.shape
    n_pool, page, n_h, dh = cache_k.shape
    n_pages = page_table.shape[1]
    rows = t * n_h
    assert MOBA_BLOCK % page == 0 and (n_pages * page) % MOBA_BLOCK == 0 and t <= MOBA_BLOCK
    assert page >= REL_MAX_DIST and n_pages * page // MOBA_BLOCK > MOBA_TOPK and dh == LANES
    ppb = MOBA_BLOCK // page
    as_rows = lambda a: a.reshape(db, rows, dh)
    seq = lambda b, p, pt: (b, 0, 0)
    const = lambda b, p, pt: (0, 0)
    paged = lambda b, p, pt: (pt[b, p], 0, 0)
    grid_spec = pltpu.PrefetchScalarGridSpec(
        num_scalar_prefetch=1,
        grid=(db, n_pages),
        in_specs=[pl.BlockSpec((1, rows, dh), seq), pl.BlockSpec((1, rows, dh), seq), pl.BlockSpec((1, rows, dh), seq),
                  pl.BlockSpec((1, page * n_h, dh), paged), pl.BlockSpec((1, page * n_h, dh), paged),
                  pl.BlockSpec(far.shape, const), pl.BlockSpec(last.shape, const), pl.BlockSpec(new.shape, const)],
        out_specs=pl.BlockSpec((1, rows, dh), seq),
        scratch_shapes=[pltpu.VMEM((n_pages, n_h, dh), F32), pltpu.VMEM((n_pages, rows, 1), F32),
                        pltpu.VMEM((n_pages, rows, 1), F32), pltpu.VMEM((n_pages, rows, dh), F32)],
    )
    out = pl.pallas_call(
        functools.partial(_attn_sample_kernel, n_heads=n_h, ppb=ppb, scale=dh ** -0.5),
        grid_spec=grid_spec,
        out_shape=jax.ShapeDtypeStruct((db, rows, dh), out_dtype),
        compiler_params=_cparams(2), name="moba_sample",
    )(page_table, as_rows(q), as_rows(k), as_rows(v),
      cache_k.reshape(n_pool, page * n_h, dh), cache_v.reshape(n_pool, page * n_h, dh), far, last, new)
    return out.reshape(db, t, hd)


def _trunk(x, mods, mc, mn, mm, cb, fb, wts, attend):
    bn, t, d = x.shape
    depth = wts["ffn_up"].shape[0]
    n_ha = mc.shape[2]
    qkw = n_ha * mc.shape[3]
    aw = n_ha * mc.shape[4]
    bw = wts["conv_w"].shape[2]
    dff = wts["ffn_down"].shape[1]
    big = t >= 512
    act_dtype = BF16 if big else F32
    out_c, out_n, out_m, out_conv, out_k, out_v, out_ffn = [], [], [], [], [], [], []
    for l in range(depth):
        shift, scale, gate = mods[l][0]
        if l % 2 == 0:
            e = l // 2
            z = _nm_matmul(x, shift, scale, wts["in_even_main"][e], n_out=2 * qkw + 2 * aw + 2 * bw,
                           name="in_even").reshape(bn, t, -1)
            g = _nm_matmul(x, shift, scale, wts["in_even_gate"][e], n_out=LANES, name="in_even_gates")
            h_a, c_new, n_new, m_new = _mlstm(z, g.reshape(bn, t, LANES), wts["b_gates"][e], wts["mlstm_norm_g"][e],
                                              mc[e], mn[e], mm[e], out_dtype=act_dtype, name="mlstm")
            assert qkw == aw and (2 * qkw + 2 * aw) % bw == 0
            cblk = (2 * qkw + 2 * aw) // bw
            y_b, cb_new = _conv_module(z, cblk, cblk + 1, cb[e], wts["conv_w"][e], wts["conv_b"][e],
                                       wts["conv_ln_g"][e], wts["conv_ln_b"][e], out_dtype=act_dtype, name="conv_module")
            x = _mm_res([h_a.reshape(bn * t, aw), y_b.reshape(bn * t, bw)],
                        [wts["out_even_a"][e], wts["out_even_b"][e]], x, gate, name="out_even")
            out_c.append(c_new); out_n.append(n_new); out_m.append(m_new); out_conv.append(cb_new)
        else:
            o = l // 2
            w = wts["in_odd"][o]
            q = _nm_matmul(x, shift, scale, w, n_out=d, col_off=0, headnorm_g=wts["q_norm_g"][o], name="in_odd_q")
            k = _nm_matmul(x, shift, scale, w, n_out=d, col_off=d, headnorm_g=wts["k_norm_g"][o], name="in_odd_k")
            v = _nm_matmul(x, shift, scale, w, n_out=d, col_off=2 * d, name="in_odd_v")
            q, k, v = (a.reshape(bn, t, d) for a in (q, k, v))
            a = attend(o, q, k, v, act_dtype)
            x = _mm_res([a.reshape(bn * t, d)], [wts["out_odd"][o]], x, gate, name="out_odd")
            out_k.append(k); out_v.append(v)
        shift, scale, gate = mods[l][1]
        gu = _nm_matmul(x, shift, scale, wts["ffn_up"][l], n_out=2 * dff, name="ffn_up").reshape(bn, t, 2 * dff)
        buf8 = jnp.pad(fb[l], ((0, 0), (0, SUBLANES - fb[l].shape[1]), (0, 0)))
        x = _ffn_down(gu, buf8, wts["ffn_conv_w"][l], wts["ffn_conv_b"][l], wts["ffn_down"][l], x, gate)
        hist = fb[l].shape[1]
        out_ffn.append(gu[:, t - hist:, :dff] if t >= hist
                       else jnp.concatenate([fb[l], gu[:, :, :dff]], axis=1)[:, -hist:])
    return (x, jnp.stack(out_c), jnp.stack(out_n), jnp.stack(out_m), jnp.stack(out_conv),
            jnp.stack(out_k), jnp.stack(out_v), jnp.stack(out_ffn))


def kernel(x_prompt, x_sample, state_mlstm_c, state_mlstm_n, state_mlstm_m, state_conv, cache_k, cache_v, page_table, state_ffn, c_prompt, c_sample, ada_w, ada_b, w_in_even, b_gates, mlstm_norm_g, conv_w, conv_b, conv_ln_g, conv_ln_b, w_out_even, w_in_odd, q_norm_g, k_norm_g, rel_bias, w_out_odd, ffn_w_up, ffn_conv_w, ffn_conv_b, ffn_w_down):
    bsz, seq, d = x_prompt.shape
    db, dt, _ = x_sample.shape
    depth = ada_w.shape[0]
    n_even, _, n_ha, dk, dv = state_mlstm_c.shape
    n_hc, dh = cache_k.shape[3], cache_k.shape[4]
    qkw, aw, bw = n_ha * dk, n_ha * dv, conv_w.shape[2]
    n_gate = 2 * n_ha
    o_gate = 2 * qkw + 2 * aw
    dt_x = x_prompt.dtype

    wts = {
        "in_even_main": jnp.concatenate([w_in_even[:, :, :o_gate], w_in_even[:, :, o_gate + n_gate:]], axis=2).astype(BF16),
        "in_even_gate": jnp.pad(w_in_even[:, :, o_gate:o_gate + n_gate], ((0, 0), (0, 0), (0, LANES - n_gate))).astype(BF16),
        "b_gates": jnp.pad(b_gates, ((0, 0), (0, LANES - n_gate))).reshape(n_even, 1, LANES),
        "mlstm_norm_g": mlstm_norm_g.reshape(n_even, 1, aw),
        "conv_w": conv_w, "conv_b": conv_b, "conv_ln_g": conv_ln_g, "conv_ln_b": conv_ln_b,
        "out_even_a": w_out_even[:, :aw].astype(BF16), "out_even_b": w_out_even[:, aw:].astype(BF16),
        "in_odd": w_in_odd.astype(BF16), "q_norm_g": q_norm_g, "k_norm_g": k_norm_g,
        "out_odd": w_out_odd.astype(BF16),
        "ffn_up": ffn_w_up.astype(BF16), "ffn_conv_w": ffn_conv_w, "ffn_conv_b": ffn_conv_b,
        "ffn_down": ffn_w_down.astype(BF16),
    }

    mod = _adaln_mods(jnp.concatenate([c_prompt, c_sample], axis=0), ada_w, ada_b)

    def split_mods(rows):
        m = mod[:, rows].reshape(depth, 2, -1, 3, 1, d)
        return [[tuple(m[l, j, :, s] for s in range(3)) for j in range(2)] for l in range(depth)]

    mods_p = split_mods(slice(0, bsz))
    mods_s = split_mods(slice(bsz, bsz + db))

    bias_tiles, rbt = _bias_prompt(rel_bias, MOBA_BLOCK)
    far, last, new = _bias_sample(rel_bias, dt, cache_k.shape[2])

    def prompt_attend(o, q, k, v, out_dtype):
        return _attn_prompt(q, k, v, bias_tiles, rbt, out_dtype=out_dtype)

    def sample_attend(o, q, k, v, out_dtype):
        return _attn_sample(q, k, v, cache_k[o], cache_v[o], page_table, far, last, new, out_dtype=out_dtype)

    zc = jnp.zeros((n_even, bsz, n_ha, dk, dv), dt_x)
    zn = jnp.zeros((n_even, bsz, n_ha, dk), dt_x)
    zm = jnp.zeros((n_even, bsz, n_ha), dt_x)
    zconv = jnp.zeros((n_even, bsz) + state_conv.shape[2:], dt_x)
    zffn = jnp.zeros((depth, bsz) + state_ffn.shape[2:], dt_x)

    y_p, p_c, p_n, p_m, p_conv, p_k, p_v, p_ffn = _trunk(
        x_prompt, mods_p, zc, zn, zm, zconv, zffn, wts, prompt_attend)
    y_s, s_c, s_n, s_m, s_conv, s_k, s_v, s_ffn = _trunk(
        x_sample, mods_s, state_mlstm_c, state_mlstm_n, state_mlstm_m, state_conv, state_ffn, wts, sample_attend)

    def heads(a):
        return a.reshape(a.shape[:3] + (n_hc, dh))

    return (y_p, y_s, p_c, p_n, p_m, p_conv, heads(p_k), heads(p_v), p_ffn,
            s_c, s_n, s_m, s_conv, heads(s_k), heads(s_v), s_ffn)
```

```python
import functools
import math

import jax
import jax.numpy as jnp
from jax import lax
from jax.experimental import pallas as pl
from jax.experimental.pallas import tpu as pltpu

F32 = jnp.float32
BF16 = jnp.bfloat16
HIGHEST = lax.Precision.HIGHEST

LANES = 128
SUBLANES = 8
VMEM_LIMIT_BYTES = 56 * 1024 * 1024

EPS = 1e-6
LN_EPS = 1e-5
MLSTM_CHUNK = 256
MLSTM_MIN_CHUNK = 128
MOBA_BLOCK = 256
MOBA_TOPK = 3
REL_BUCKETS = 32
REL_MAX_DIST = 128
NEG = -1e30
CONV_HALO = 32

_NT = (((1,), (1,)), ((), ()))
_TN = (((0,), (0,)), ((), ()))


def _cparams(n_axes):
    return pltpu.CompilerParams(dimension_semantics=("arbitrary",) * n_axes, vmem_limit_bytes=VMEM_LIMIT_BYTES)


def _sigmoid(x):
    return jax.nn.sigmoid(x)


def _col_tile(n, pref):
    tile = min(pref, n)
    while n % tile:
        tile -= LANES
    return tile


def _row_tiling(bn, t, tm):
    if t >= tm:
        assert t % tm == 0
        tpb = t // tm
        return 1, tm, bn * tpb, tpb
    gb = min(bn, tm // t)
    assert bn % gb == 0 and t % SUBLANES == 0
    return gb, t, bn // gb, 1


def _mod_kernel(c_ref, w_ref, b_ref, o_ref):
    c = c_ref[...]
    s = (c * _sigmoid(c)).astype(BF16)
    o_ref[...] = jnp.dot(s, w_ref[...].astype(BF16), preferred_element_type=F32) + b_ref[...]


def _adaln_mods(c_all, ada_w, ada_b, tn=512):
    depth, two, d, n3 = ada_w.shape
    nlj = depth * two
    rows = c_all.shape[0]
    tn = _col_tile(n3, tn)
    return pl.pallas_call(
        _mod_kernel,
        grid=(nlj, n3 // tn),
        in_specs=[pl.BlockSpec((rows, d), lambda l, j: (0, 0)),
                  pl.BlockSpec((None, d, tn), lambda l, j: (l, 0, j)),
                  pl.BlockSpec((None, 1, tn), lambda l, j: (l, 0, j))],
        out_specs=pl.BlockSpec((None, rows, tn), lambda l, j: (l, 0, j)),
        out_shape=jax.ShapeDtypeStruct((nlj, rows, n3), F32),
        compiler_params=_cparams(2), name="adaln_mods",
    )(c_all, ada_w.reshape(nlj, d, n3), ada_b.reshape(nlj, 1, n3))


def _nm_matmul_kernel(x_ref, sh_ref, sc_ref, w_ref, *rest, headnorm):
    if headnorm:
        g_ref, o_ref, h_ref = rest
    else:
        o_ref, h_ref = rest

    @pl.when(pl.program_id(1) == 0)
    def _():
        x = x_ref[...]
        r = lax.rsqrt(jnp.mean(x * x, axis=-1, keepdims=True) + EPS)
        h = (x * r) * (1.0 + sc_ref[...]) + sh_ref[...]
        h_ref[...] = h.reshape(h_ref.shape).astype(BF16)

    acc = jnp.dot(h_ref[...], w_ref[...], preferred_element_type=F32)
    if headnorm:
        g = g_ref[...]
        for s in range(acc.shape[1] // LANES):
            y = acc[:, s * LANES:(s + 1) * LANES]
            y = y * lax.rsqrt(jnp.mean(y * y, axis=-1, keepdims=True) + EPS) * g
            o_ref[:, s * LANES:(s + 1) * LANES] = y.astype(o_ref.dtype)
    else:
        o_ref[...] = acc.astype(o_ref.dtype)


def _nm_matmul(x3, shift, scale, w, *, n_out, col_off=0, tm=1024, tn=1024, headnorm_g=None, name="nm_matmul"):
    bn, t, d = x3.shape
    gb, rows, n_i, tpb = _row_tiling(bn, t, tm)
    tmm = gb * rows
    tn = _col_tile(math.gcd(n_out, col_off) if col_off else n_out, tn)
    joff = col_off // tn
    in_specs = [pl.BlockSpec((gb, rows, d), lambda i, j: (i // tpb, i % tpb, 0)),
                pl.BlockSpec((gb, 1, d), lambda i, j: (i // tpb, 0, 0)),
                pl.BlockSpec((gb, 1, d), lambda i, j: (i // tpb, 0, 0)),
                pl.BlockSpec((d, tn), lambda i, j: (0, j + joff))]
    args = [x3, shift, scale, w]
    if headnorm_g is not None:
        in_specs.append(pl.BlockSpec((1, LANES), lambda i, j: (0, 0)))
        args.append(headnorm_g.reshape(1, LANES))
    return pl.pallas_call(
        functools.partial(_nm_matmul_kernel, headnorm=headnorm_g is not None),
        grid=(n_i, n_out // tn),
        in_specs=in_specs,
        out_specs=pl.BlockSpec((tmm, tn), lambda i, j: (i, j)),
        out_shape=jax.ShapeDtypeStruct((bn * t, n_out), F32),
        scratch_shapes=[pltpu.VMEM((tmm, d), BF16)],
        compiler_params=_cparams(2), name=name,
    )(*args)


def _mm_res_kernel(*refs, n_in):
    a_refs, w_refs = refs[:n_in], refs[n_in:2 * n_in]
    x_ref, gate_ref, o_ref = refs[2 * n_in:]
    acc = None
    for a_ref, w_ref in zip(a_refs, w_refs):
        part = jnp.dot(a_ref[...].astype(BF16), w_ref[...], preferred_element_type=F32)
        acc = part if acc is None else acc + part
    x = x_ref[...]
    o_ref[...] = x + gate_ref[...] * acc.reshape(x.shape)


def _mm_res(a_list, w_list, x3, gate, *, tm=1024, tn=1024, name="mm_res"):
    bn, t, d = x3.shape
    gb, rows, n_i, tpb = _row_tiling(bn, t, tm)
    tmm = gb * rows
    tn = _col_tile(d, tn)
    n_in = len(a_list)
    in_specs = ([pl.BlockSpec((tmm, a.shape[1]), lambda i, j: (i, 0)) for a in a_list]
                + [pl.BlockSpec((w.shape[0], tn), lambda i, j: (0, j)) for w in w_list]
                + [pl.BlockSpec((gb, rows, tn), lambda i, j: (i // tpb, i % tpb, j)),
                   pl.BlockSpec((gb, 1, tn), lambda i, j: (i // tpb, 0, j))])
    return pl.pallas_call(
        functools.partial(_mm_res_kernel, n_in=n_in),
        grid=(n_i, d // tn),
        in_specs=in_specs,
        out_specs=pl.BlockSpec((gb, rows, tn), lambda i, j: (i // tpb, i % tpb, j)),
        out_shape=jax.ShapeDtypeStruct((bn, t, d), F32),
        compiler_params=_cparams(2), name=name,
    )(*a_list, *w_list, x3, gate)


def _mlstm_kernel(q_ref, k_ref, v_ref, zo_ref, g_ref, bg_ref, gn_ref, c0_ref, n0_ref, m0_ref,
                  h_ref, co_ref, no_ref, mo_ref, c_s, n_s, m_s, *, tb, chunk):
    ck = pl.program_id(1)
    L = chunk
    n_heads, dk, dv = c_s.shape

    @pl.when(ck == 0)
    def _():
        c_s[...] = c0_ref[0]
        n_s[...] = n0_ref[0]
        m_s[...] = m0_ref[0]

    def pad_rows(x):
        if tb == L:
            return x
        return jnp.concatenate([x, jnp.zeros((L - tb, x.shape[1]), x.dtype)], axis=0)

    gates = pad_rows(g_ref[0] + bg_ref[...])
    lane = lax.broadcasted_iota(jnp.int32, (L, LANES), 1)
    li_all = jnp.where(lane < n_heads, gates, 0.0)
    lf_all = jnp.where((lane >= n_heads) & (lane < 2 * n_heads),
                       -(jnp.maximum(-gates, 0.0) + jnp.log1p(jnp.exp(-jnp.abs(gates)))), 0.0)
    if tb != L:
        real = lax.broadcasted_iota(jnp.int32, (L, LANES), 0) < tb
        li_all = jnp.where(real | (lane >= n_heads), li_all, NEG)
        lf_all = jnp.where(real, lf_all, 0.0)
    ri = lax.broadcasted_iota(jnp.int32, (L, L), 0)
    ci = lax.broadcasted_iota(jnp.int32, (L, L), 1)
    tri = ri >= ci
    y = li_all + jnp.dot(tri.astype(F32), lf_all, precision=HIGHEST, preferred_element_type=F32)
    er = lax.broadcasted_iota(jnp.int32, (SUBLANES, LANES), 0)
    el = lax.broadcasted_iota(jnp.int32, (SUBLANES, LANES), 1)
    e = jnp.where(er < n_heads, jnp.where(el == er + n_heads, 1.0, 0.0) - jnp.where(el == er, 1.0, 0.0), 0.0)
    rt = lax.dot_general(e, y, _NT, precision=HIGHEST, preferred_element_type=F32)

    for hd in range(n_heads):
        li = y[:, hd:hd + 1]
        b = y[:, n_heads + hd:n_heads + hd + 1]
        r_row = rt[hd:hd + 1, :]
        q = pad_rows(q_ref[0, :, hd * dk:(hd + 1) * dk]) * (dk ** -0.5)
        k = pad_rows(k_ref[0, :, hd * dk:(hd + 1) * dk])
        v = pad_rows(v_ref[0, :, hd * dv:(hd + 1) * dv])
        a = b + m_s[hd][:, :1]
        dmat = jnp.where(tri, b - r_row, NEG)
        mt = jnp.maximum(a, jnp.max(dmat, axis=-1, keepdims=True))
        w_inter = jnp.exp(a - mt)
        qb, kb, vb = q.astype(BF16), k.astype(BF16), v.astype(BF16)
        s = lax.dot_general(qb, kb, _NT, preferred_element_type=F32) * jnp.exp(dmat - mt)
        c_old = c_s[hd]
        n_old = n_s[hd]
        num = (w_inter * jnp.dot(qb, c_old.astype(BF16), preferred_element_type=F32)
               + jnp.dot(s.astype(BF16), vb, preferred_element_type=F32))
        den = w_inter * jnp.sum(q * n_old, axis=-1, keepdims=True) + jnp.sum(s, axis=-1, keepdims=True)
        hh = (num / jnp.maximum(jnp.abs(den), jnp.exp(-mt)))[:tb]
        hn = hh * lax.rsqrt(jnp.mean(hh * hh, axis=-1, keepdims=True) + EPS)
        h_ref[0, :, hd * dv:(hd + 1) * dv] = (hn * gn_ref[:, hd * dv:(hd + 1) * dv]
                                              * _sigmoid(zo_ref[0, :, hd * dv:(hd + 1) * dv])).astype(h_ref.dtype)

        m_l = mt[L - 1:L]
        g_inter = jnp.exp(a[L - 1:L] - m_l)
        g_s = jnp.exp(b[L - 1:L] - b + li - m_l)
        c_s[hd] = g_inter * c_old + lax.dot_general(kb, (g_s * v).astype(BF16), _TN, preferred_element_type=F32)
        n_s[hd] = g_inter * n_old + jnp.sum(g_s * k, axis=0, keepdims=True)
        m_s[hd] = jnp.broadcast_to(m_l, (1, LANES))

    @pl.when(ck == pl.num_programs(1) - 1)
    def _():
        co_ref[0] = c_s[...]
        no_ref[0] = n_s[...]
        mo_ref[0] = m_s[...]


def _mlstm(z3, g3, bg, gn, c0, n0, m0, *, out_dtype, name):
    bn, t, _ = z3.shape
    _, n_heads, dk, dv = c0.shape
    assert dk == dv and dk % LANES == 0 and 2 * n_heads <= LANES and n_heads <= SUBLANES
    if t % MLSTM_CHUNK == 0:
        tb = chunk = MLSTM_CHUNK
    else:
        assert t <= MLSTM_MIN_CHUNK
        tb, chunk = t, MLSTM_MIN_CHUNK
    n_c = t // tb
    hw = n_heads * dk
    col = lambda sec: (lambda b, c: (b, c, sec))
    state = lambda b, c: (b, 0, 0, 0)
    h, c, n, m = pl.pallas_call(
        functools.partial(_mlstm_kernel, tb=tb, chunk=chunk),
        grid=(bn, n_c),
        in_specs=[pl.BlockSpec((1, tb, hw), col(0)), pl.BlockSpec((1, tb, hw), col(1)),
                  pl.BlockSpec((1, tb, hw), col(2)), pl.BlockSpec((1, tb, hw), col(3)),
                  pl.BlockSpec((1, tb, LANES), lambda b, c: (b, c, 0)),
                  pl.BlockSpec((1, LANES), lambda b, c: (0, 0)),
                  pl.BlockSpec((1, hw), lambda b, c: (0, 0)),
                  pl.BlockSpec((1, n_heads, dk, dv), state), pl.BlockSpec((1, n_heads, 1, dk), state),
                  pl.BlockSpec((1, n_heads, 1, LANES), state)],
        out_specs=[pl.BlockSpec((1, tb, hw), lambda b, c: (b, c, 0)),
                   pl.BlockSpec((1, n_heads, dk, dv), state), pl.BlockSpec((1, n_heads, 1, dk), state),
                   pl.BlockSpec((1, n_heads, 1, LANES), state)],
        out_shape=[jax.ShapeDtypeStruct((bn, t, hw), out_dtype),
                   jax.ShapeDtypeStruct((bn, n_heads, dk, dv), F32),
                   jax.ShapeDtypeStruct((bn, n_heads, 1, dk), F32),
                   jax.ShapeDtypeStruct((bn, n_heads, 1, LANES), F32)],
        scratch_shapes=[pltpu.VMEM((n_heads, dk, dv), F32), pltpu.VMEM((n_heads, 1, dk), F32),
                        pltpu.VMEM((n_heads, 1, LANES), F32)],
        compiler_params=_cparams(2), name=name,
    )(z3, z3, z3, z3, g3, bg, gn, c0, n0.reshape(bn, n_heads, 1, dk),
      jnp.broadcast_to(m0[:, :, None, None], (bn, n_heads, 1, LANES)))
    return h, c, n.reshape(bn, n_heads, dk), m[:, :, 0, 0]


def _convmod_kernel(a_ref, g_ref, buf_ref, w_ref, b_ref, lg_ref, lb_ref, y_ref, nb_ref, u_s, *, tt, width, rc):
    t = pl.program_id(1)
    hist = width - 1
    lo = CONV_HALO - hist

    @pl.when(t == 0)
    def _():
        u_s[lo:CONV_HALO, :] = buf_ref[0]

    @pl.when(t > 0)
    def _():
        u_s[0:CONV_HALO, :] = u_s[tt:tt + CONV_HALO, :]

    u_s[CONV_HALO:CONV_HALO + tt, :] = a_ref[0] * _sigmoid(g_ref[0])
    bias = b_ref[...]
    lg = lg_ref[...]
    lb = lb_ref[...]
    for r0 in range(0, tt, rc):
        acc = jnp.broadcast_to(bias, (rc, bias.shape[1]))
        for j in range(width):
            acc = acc + w_ref[j:j + 1, :] * u_s[lo + r0 + j:lo + r0 + j + rc, :]
        mu = jnp.mean(acc, axis=-1, keepdims=True)
        xc = acc - mu
        var = jnp.mean(xc * xc, axis=-1, keepdims=True)
        y = xc * lax.rsqrt(var + LN_EPS) * lg + lb
        y_ref[0, r0:r0 + rc, :] = (y * _sigmoid(y)).astype(y_ref.dtype)

    @pl.when(t == pl.num_programs(1) - 1)
    def _():
        nb_ref[0] = u_s[tt + lo:tt + CONV_HALO, :]


def _conv_module(z3, col_a, col_g, cbuf, conv_w, conv_b, ln_g, ln_b, *, out_dtype, name):
    bn, t, _ = z3.shape
    width, ch = conv_w.shape
    assert width - 1 <= CONV_HALO
    tt = 128 if t % 128 == 0 else t
    n_t = t // tt
    assert n_t == 1 or tt >= CONV_HALO
    y, nb = pl.pallas_call(
        functools.partial(_convmod_kernel, tt=tt, width=width,
                          rc=2 * SUBLANES if tt % (2 * SUBLANES) == 0 else SUBLANES),
        grid=(bn, n_t),
        in_specs=[pl.BlockSpec((1, tt, ch), lambda b, i: (b, i, col_a)),
                  pl.BlockSpec((1, tt, ch), lambda b, i: (b, i, col_g)),
                  pl.BlockSpec((1, width - 1, ch), lambda b, i: (b, 0, 0)),
                  pl.BlockSpec((width, ch), lambda b, i: (0, 0)),
                  pl.BlockSpec((1, ch), lambda b, i: (0, 0)),
                  pl.BlockSpec((1, ch), lambda b, i: (0, 0)),
                  pl.BlockSpec((1, ch), lambda b, i: (0, 0))],
        out_specs=[pl.BlockSpec((1, tt, ch), lambda b, i: (b, i, 0)),
                   pl.BlockSpec((1, width - 1, ch), lambda b, i: (b, 0, 0))],
        out_shape=[jax.ShapeDtypeStruct((bn, t, ch), out_dtype),
                   jax.ShapeDtypeStruct((bn, width - 1, ch), F32)],
        scratch_shapes=[pltpu.VMEM((CONV_HALO + tt, ch), F32)],
        compiler_params=_cparams(2), name=name,
    )(z3, z3, cbuf, conv_w, conv_b.reshape(1, ch), ln_g.reshape(1, ch), ln_b.reshape(1, ch))
    return y, nb


def _ffn_down_kernel(g_ref, u_ref, halo_ref, buf_ref, cw_ref, cb_ref, wd_ref, x_ref, gate_ref, o_ref, acc_s,
                     *, tpb, rows):
    ti = pl.program_id(0) % tpb
    f = pl.program_id(1)
    g3 = g_ref[...]
    gb, _, tf = g3.shape
    g = g3.reshape(gb * rows, tf)
    n = gb * rows
    prev = buf_ref[...].reshape(gb * SUBLANES, tf)
    if tpb > 1:
        prev = jnp.where(ti == 0, prev, pltpu.roll(halo_ref[...].reshape(gb * SUBLANES, tf), 2, axis=0))
    r8 = lax.broadcasted_iota(jnp.int32, (gb * SUBLANES, tf), 0) % SUBLANES
    prev1 = pltpu.roll(prev, gb * SUBLANES - 1, axis=0)
    gm1 = pltpu.roll(g, 1, axis=0)
    gm2 = pltpu.roll(g, 2, axis=0)
    if rows == SUBLANES:
        gm1 = jnp.where(r8 < 1, prev1, gm1)
        gm2 = jnp.where(r8 < 2, prev, gm2)
    else:
        gm1 = jnp.concatenate([jnp.where(r8 < 1, prev1, gm1[:SUBLANES]), gm1[SUBLANES:]], axis=0)
        gm2 = jnp.concatenate([jnp.where(r8 < 2, prev, gm2[:SUBLANES]), gm2[SUBLANES:]], axis=0)
    gc = cw_ref[0:1, :] * gm2 + cw_ref[1:2, :] * gm1 + cw_ref[2:3, :] * g + cb_ref[...]
    act = (gc * _sigmoid(gc)) * u_ref[...].reshape(n, tf)
    part = jnp.dot(act.astype(BF16), wd_ref[...], preferred_element_type=F32)

    @pl.when(f == 0)
    def _():
        acc_s[...] = part

    @pl.when(f > 0)
    def _():
        acc_s[...] += part

    @pl.when(f == pl.num_programs(1) - 1)
    def _():
        x = x_ref[...]
        o_ref[...] = x + gate_ref[...] * acc_s[...].reshape(x.shape)


def _ffn_down(gu3, buf8, conv_w, conv_b, wd, x3, gate, *, tm=512, tf=1408, name="ffn_down"):
    bn, t, d = x3.shape
    dff = wd.shape[0]
    gb, rows, n_i, tpb = _row_tiling(bn, t, tm)
    tf = _col_tile(dff, tf if gb == 1 else 512)
    n_f = dff // tf
    assert dff % tf == 0 and conv_w.shape[0] == 3
    hpb = rows // SUBLANES
    row = lambda i, f: (i // tpb, i % tpb, f)
    return pl.pallas_call(
        functools.partial(_ffn_down_kernel, tpb=tpb, rows=rows),
        grid=(n_i, n_f),
        in_specs=[pl.BlockSpec((gb, rows, tf), row),
                  pl.BlockSpec((gb, rows, tf), lambda i, f: (i // tpb, i % tpb, f + n_f)),
                  pl.BlockSpec((gb, SUBLANES, tf), lambda i, f: (i // tpb, jnp.maximum((i % tpb) * hpb - 1, 0), f)),
                  pl.BlockSpec((gb, SUBLANES, tf), lambda i, f: (i // tpb, 0, f)),
                  pl.BlockSpec((3, tf), lambda i, f: (0, f)),
                  pl.BlockSpec((1, tf), lambda i, f: (0, f)),
                  pl.BlockSpec((tf, d), lambda i, f: (f, 0)),
                  pl.BlockSpec((gb, rows, d), lambda i, f: (i // tpb, i % tpb, 0)),
                  pl.BlockSpec((gb, 1, d), lambda i, f: (i // tpb, 0, 0))],
        out_specs=pl.BlockSpec((gb, rows, d), lambda i, f: (i // tpb, i % tpb, 0)),
        out_shape=jax.ShapeDtypeStruct((bn, t, d), F32),
        scratch_shapes=[pltpu.VMEM((gb * rows, d), F32)],
        compiler_params=_cparams(2), name=name,
    )(gu3, gu3, gu3, buf8, conv_w, conv_b.reshape(1, dff), wd, x3, gate)


def _t5_bucket(dist):
    n = jnp.maximum(dist, 0)
    exact = REL_BUCKETS // 2
    large = exact + (jnp.log(jnp.maximum(n, 1).astype(F32) / exact)
                     / math.log(REL_MAX_DIST / exact) * (REL_BUCKETS - exact)).astype(jnp.int32)
    return jnp.where(n < exact, n, jnp.minimum(large, REL_BUCKETS - 1))


def _bias_lookup(dist, table_col):
    bucket = _t5_bucket(dist)
    bias = jnp.zeros(dist.shape, F32)
    for kk in range(REL_BUCKETS):
        bias = jnp.where(bucket == kk, table_col(kk), bias)
    return bias


def _bias_prompt_kernel(rb_ref, o_ref, *, blk):
    rb = rb_ref[0]
    key = lax.broadcasted_iota(jnp.int32, (blk, blk), 0)
    qry = lax.broadcasted_iota(jnp.int32, (blk, blk), 1)
    col = lambda kk: rb[:, kk:kk + 1]
    own = _bias_lookup(qry - key, col)
    o_ref[0, 0] = jnp.where(qry >= key, own, NEG)
    o_ref[0, 1] = _bias_lookup(blk + qry - key, col)


def _bias_prompt(rel_bias, blk):
    n_b, n_h = rel_bias.shape
    rbt = jnp.pad(rel_bias.T, ((0, 0), (0, LANES - n_b))).reshape(n_h, 1, LANES)
    return pl.pallas_call(
        functools.partial(_bias_prompt_kernel, blk=blk),
        grid=(n_h,),
        in_specs=[pl.BlockSpec((1, 1, LANES), lambda h: (h, 0, 0))],
        out_specs=pl.BlockSpec((1, 2, blk, blk), lambda h: (h, 0, 0, 0)),
        out_shape=jax.ShapeDtypeStruct((n_h, 2, blk, blk), F32),
        compiler_params=_cparams(1), name="bias_prompt",
    )(rbt), rbt


def _bias_sample_kernel(rb_ref, fl_ref, new_ref, *, n_heads, page):
    rb = rb_ref[...]
    col = lambda kk: rb[:, kk:kk + 1]
    for ref, base in ((fl_ref.at[0], 2 * page), (fl_ref.at[1], page), (new_ref, 0)):
        shape = ref.shape
        r = lax.broadcasted_iota(jnp.int32, shape, 0)
        c = lax.broadcasted_iota(jnp.int32, shape, 1)
        dist = base + r // n_heads - c // n_heads
        ok = (r % n_heads == c % n_heads) & (dist >= 0)
        ref[...] = jnp.where(ok, _bias_lookup(dist, col), NEG)


def _bias_sample(rel_bias, t, page):
    n_b, n_h = rel_bias.shape
    rows = t * n_h
    rb = jnp.pad(jnp.tile(rel_bias.T, (t, 1)), ((0, 0), (0, LANES - n_b)))
    return pl.pallas_call(
        functools.partial(_bias_sample_kernel, n_heads=n_h, page=page),
        out_shape=[jax.ShapeDtypeStruct((2, rows, page * n_h), F32),
                   jax.ShapeDtypeStruct((rows, rows), F32)],
        compiler_params=pltpu.CompilerParams(vmem_limit_bytes=VMEM_LIMIT_BYTES), name="bias_sample",
    )(rb)


def _attn_prompt_kernel(q_ref, k_ref, v_ref, bias_ref, rb_ref, o_ref, km_s, kb_s, vt_s, s_s, *, blk, n_blk, scale):
    km_s[...] = jnp.zeros_like(km_s)
    for j in range(n_blk):
        kj = k_ref[0, j * blk:(j + 1) * blk, :]
        km_s[j:j + 1, :] = jnp.mean(kj, axis=0, keepdims=True)
        kb_s[j * blk:(j + 1) * blk, :] = kj.astype(BF16)
        vt_s[:, j * blk:(j + 1) * blk] = v_ref[0, j * blk:(j + 1) * blk, :].T.astype(BF16)

    far = rb_ref[0][:, REL_BUCKETS - 1:REL_BUCKETS]
    km = km_s[...]
    row = lax.broadcasted_iota(jnp.int32, (km.shape[0], blk), 0)
    tile = 0
    for c in range(n_blk):
        qf = q_ref[0, c * blk:(c + 1) * blk, :]
        qb = qf.astype(BF16)
        add = [0.0] * c
        if c > MOBA_TOPK:
            sc = lax.dot_general(km, qf, _NT, precision=HIGHEST, preferred_element_type=F32)
            sc = jnp.where(row < c, sc, -jnp.inf)
            for j in range(c):
                sj = sc[j:j + 1, :]
                ahead = (sc > sj) | ((sc == sj) & (row < j))
                rank = jnp.sum(jnp.where(ahead, 1.0, 0.0), axis=0, keepdims=True)
                add[j] = jnp.where(rank < MOBA_TOPK, 0.0, NEG)
        m = None
        for j in range(c + 1):
            st = lax.dot_general(kb_s[j * blk:(j + 1) * blk, :], qb, _NT, preferred_element_type=F32) * scale
            if j == c:
                st = st + bias_ref[0, 0]
            elif j == c - 1:
                st = st + (bias_ref[0, 1] + add[j])
            else:
                st = st + (add[j] + far)
            s_s[tile + j] = st
            mj = jnp.max(st, axis=0, keepdims=True)
            m = mj if m is None else jnp.maximum(m, mj)
        l = jnp.zeros_like(m)
        acc = jnp.zeros((vt_s.shape[0], blk), F32)
        for j in range(c + 1):
            p = jnp.exp(s_s[tile + j] - m)
            l = l + jnp.sum(p, axis=0, keepdims=True)
            acc = acc + jnp.dot(vt_s[:, j * blk:(j + 1) * blk], p.astype(BF16), preferred_element_type=F32)
        o_ref[0, c * blk:(c + 1) * blk, :] = (acc / l).T.astype(o_ref.dtype)
        tile += c + 1


def _attn_prompt(q3, k3, v3, bias_tiles, rbt, *, out_dtype):
    bn, s, _ = q3.shape
    n_h = bias_tiles.shape[0]
    dh = q3.shape[2] // n_h
    blk = MOBA_BLOCK
    assert s % blk == 0 and dh == LANES and blk >= REL_MAX_DIST
    n_blk = s // blk
    head = lambda b, h: (b, 0, h)
    return pl.pallas_call(
        functools.partial(_attn_prompt_kernel, blk=blk, n_blk=n_blk, scale=dh ** -0.5),
        grid=(bn, n_h),
        in_specs=[pl.BlockSpec((1, s, dh), head), pl.BlockSpec((1, s, dh), head), pl.BlockSpec((1, s, dh), head),
                  pl.BlockSpec((1, 2, blk, blk), lambda b, h: (h, 0, 0, 0)),
                  pl.BlockSpec((1, 1, LANES), lambda b, h: (h, 0, 0))],
        out_specs=pl.BlockSpec((1, s, dh), head),
        out_shape=jax.ShapeDtypeStruct(q3.shape, out_dtype),
        scratch_shapes=[pltpu.VMEM((-(-n_blk // SUBLANES) * SUBLANES, dh), F32), pltpu.VMEM((s, dh), BF16),
                        pltpu.VMEM((dh, s), BF16), pltpu.VMEM((n_blk * (n_blk + 1) // 2, blk, blk), F32)],
        compiler_params=_cparams(2), name="moba_prompt",
    )(q3, k3, v3, bias_tiles, rbt)


def _attn_sample_kernel(pt_ref, q_ref, kn_ref, vn_ref, *refs, n_heads, ppb, pps, scale):
    del pt_ref
    kp_refs, vp_refs = refs[:pps], refs[pps:2 * pps]
    bias_ref, new_ref, o_ref, ks_s, m_s, l_s, o_s = refs[2 * pps:]
    g = pl.program_id(1)
    last = pl.num_programs(1) - 1
    qf = q_ref[0]
    qb = qf.astype(BF16)
    rows, dh = qf.shape

    def partial_softmax(kk, vv, bias):
        s = lax.dot_general(qb, kk.astype(BF16), _NT, preferred_element_type=F32) * scale + bias
        m = jnp.max(s, axis=-1, keepdims=True)
        e = jnp.exp(s - m)
        return m, jnp.sum(e, axis=-1, keepdims=True), jnp.dot(e.astype(BF16), vv.astype(BF16),
                                                              preferred_element_type=F32)

    for r in range(pps):
        pg = g * pps + r
        kp = kp_refs[r][0]
        ks_s[pg] = jnp.sum(kp.reshape(kp.shape[0] // n_heads, n_heads, dh), axis=0)
        bias = bias_ref[jnp.where(g == last, 1, 0)] if r == pps - 1 else bias_ref[0]
        m, l, o = partial_softmax(kp, vp_refs[r][0], bias)
        m_s[pg] = m
        l_s[pg] = l
        o_s[pg] = o

    @pl.when(g == last)
    def _():
        n_pg = m_s.shape[0]
        n_blk = n_pg // ppb
        inv = 1.0 / (ppb * (kp_refs[0].shape[1] // n_heads))
        scs = []
        for j in range(n_blk):
            km = ks_s[j * ppb]
            for t in range(1, ppb):
                km = km + ks_s[j * ppb + t]
            km = jnp.concatenate([km * inv] * (rows // n_heads), axis=0)
            scs.append(jnp.sum(qf * km, axis=-1, keepdims=True))
        sel = []
        for j in range(n_blk):
            rank = jnp.zeros((rows, 1), F32)
            for jj in range(n_blk):
                if jj != j:
                    ahead = (scs[jj] >= scs[j]) if jj < j else (scs[jj] > scs[j])
                    rank = rank + jnp.where(ahead, 1.0, 0.0)
            sel.append(rank < MOBA_TOPK)
        m_n, l_n, o_n = partial_softmax(kn_ref[0], vn_ref[0], new_ref[...])
        m_all = m_n
        for pg in range(n_pg):
            m_all = jnp.maximum(m_all, jnp.where(sel[pg // ppb], m_s[pg], NEG))
        w_n = jnp.exp(m_n - m_all)
        l_all = w_n * l_n
        o_all = w_n * o_n
        for pg in range(n_pg):
            w = jnp.where(sel[pg // ppb], jnp.exp(m_s[pg] - m_all), 0.0)
            l_all = l_all + w * l_s[pg]
            o_all = o_all + w * o_s[pg]
        o_ref[0] = (o_all / l_all).astype(o_ref.dtype)


def _attn_sample(q, k, v, cache_k, cache_v, page_table, far_last, new, *, out_dtype, pps=4):
    db, t, hd = q.shape
    n_pool, page, n_h, dh = cache_k.shape
    n_pages = page_table.shape[1]
    rows = t * n_h
    assert MOBA_BLOCK % page == 0 and (n_pages * page) % MOBA_BLOCK == 0 and t <= MOBA_BLOCK
    assert page >= REL_MAX_DIST and n_pages * page // MOBA_BLOCK > MOBA_TOPK and dh == LANES
    assert n_pages % pps == 0
    ppb = MOBA_BLOCK // page
    as_rows = lambda a: a.reshape(db, rows, dh)
    seq = lambda b, g, pt: (b, 0, 0)
    paged = [(lambda b, g, pt, r=r: (pt[b, g * pps + r], 0, 0)) for r in range(pps)]
    page_spec = lambda r: pl.BlockSpec((1, page * n_h, dh), paged[r])
    grid_spec = pltpu.PrefetchScalarGridSpec(
        num_scalar_prefetch=1,
        grid=(db, n_pages // pps),
        in_specs=([pl.BlockSpec((1, rows, dh), seq)] * 3 + [page_spec(r) for r in range(pps)] * 2
                  + [pl.BlockSpec(far_last.shape, lambda b, g, pt: (0, 0, 0)),
                     pl.BlockSpec(new.shape, lambda b, g, pt: (0, 0))]),
        out_specs=pl.BlockSpec((1, rows, dh), seq),
        scratch_shapes=[pltpu.VMEM((n_pages, n_h, dh), F32), pltpu.VMEM((n_pages, rows, 1), F32),
                        pltpu.VMEM((n_pages, rows, 1), F32), pltpu.VMEM((n_pages, rows, dh), F32)],
    )
    ck = cache_k.reshape(n_pool, page * n_h, dh)
    cv = cache_v.reshape(n_pool, page * n_h, dh)
    out = pl.pallas_call(
        functools.partial(_attn_sample_kernel, n_heads=n_h, ppb=ppb, pps=pps, scale=dh ** -0.5),
        grid_spec=grid_spec,
        out_shape=jax.ShapeDtypeStruct((db, rows, dh), out_dtype),
        compiler_params=_cparams(2), name="moba_sample",
    )(page_table, as_rows(q), as_rows(k), as_rows(v), *([ck] * pps), *([cv] * pps), far_last, new)
    return out.reshape(db, t, hd)


def _trunk(x, mods, mc, mn, mm, cb, fb, wts, attend):
    bn, t, d = x.shape
    depth = wts["ffn_up"].shape[0]
    n_ha = mc.shape[2]
    qkw = n_ha * mc.shape[3]
    aw = n_ha * mc.shape[4]
    bw = wts["conv_w"].shape[2]
    dff = wts["ffn_down"].shape[1]
    big = t >= 512
    act_dtype = BF16 if big else F32
    out_c, out_n, out_m, out_conv, out_k, out_v, out_ffn = [], [], [], [], [], [], []
    for l in range(depth):
        shift, scale, gate = mods[l][0]
        if l % 2 == 0:
            e = l // 2
            z = _nm_matmul(x, shift, scale, wts["in_even_main"][e], n_out=2 * qkw + 2 * aw + 2 * bw,
                           name="in_even").reshape(bn, t, -1)
            g = _nm_matmul(x, shift, scale, wts["in_even_gate"][e], n_out=LANES, name="in_even_gates")
            h_a, c_new, n_new, m_new = _mlstm(z, g.reshape(bn, t, LANES), wts["b_gates"][e], wts["mlstm_norm_g"][e],
                                              mc[e], mn[e], mm[e], out_dtype=act_dtype, name="mlstm")
            assert qkw == aw and (2 * qkw + 2 * aw) % bw == 0
            cblk = (2 * qkw + 2 * aw) // bw
            y_b, cb_new = _conv_module(z, cblk, cblk + 1, cb[e], wts["conv_w"][e], wts["conv_b"][e],
                                       wts["conv_ln_g"][e], wts["conv_ln_b"][e], out_dtype=act_dtype, name="conv_module")
            x = _mm_res([h_a.reshape(bn * t, aw), y_b.reshape(bn * t, bw)],
                        [wts["out_even_a"][e], wts["out_even_b"][e]], x, gate, name="out_even")
            out_c.append(c_new); out_n.append(n_new); out_m.append(m_new); out_conv.append(cb_new)
        else:
            o = l // 2
            w = wts["in_odd"][o]
            q = _nm_matmul(x, shift, scale, w, n_out=d, col_off=0, headnorm_g=wts["q_norm_g"][o], name="in_odd_q")
            k = _nm_matmul(x, shift, scale, w, n_out=d, col_off=d, headnorm_g=wts["k_norm_g"][o], name="in_odd_k")
            v = _nm_matmul(x, shift, scale, w, n_out=d, col_off=2 * d, name="in_odd_v")
            q, k, v = (a.reshape(bn, t, d) for a in (q, k, v))
            a = attend(o, q, k, v, act_dtype)
            x = _mm_res([a.reshape(bn * t, d)], [wts["out_odd"][o]], x, gate, name="out_odd")
            out_k.append(k); out_v.append(v)
        shift, scale, gate = mods[l][1]
        gu = _nm_matmul(x, shift, scale, wts["ffn_up"][l], n_out=2 * dff, name="ffn_up").reshape(bn, t, 2 * dff)
        buf8 = jnp.pad(fb[l], ((0, 0), (0, SUBLANES - fb[l].shape[1]), (0, 0)))
        x = _ffn_down(gu, buf8, wts["ffn_conv_w"][l], wts["ffn_conv_b"][l], wts["ffn_down"][l], x, gate)
        hist = fb[l].shape[1]
        out_ffn.append(gu[:, t - hist:, :dff] if t >= hist
                       else jnp.concatenate([fb[l], gu[:, :, :dff]], axis=1)[:, -hist:])
    return (x, jnp.stack(out_c), jnp.stack(out_n), jnp.stack(out_m), jnp.stack(out_conv),
            jnp.stack(out_k), jnp.stack(out_v), jnp.stack(out_ffn))


def kernel(x_prompt, x_sample, state_mlstm_c, state_mlstm_n, state_mlstm_m, state_conv, cache_k, cache_v, page_table, state_ffn, c_prompt, c_sample, ada_w, ada_b, w_in_even, b_gates, mlstm_norm_g, conv_w, conv_b, conv_ln_g, conv_ln_b, w_out_even, w_in_odd, q_norm_g, k_norm_g, rel_bias, w_out_odd, ffn_w_up, ffn_conv_w, ffn_conv_b, ffn_w_down):
    bsz, seq, d = x_prompt.shape
    db, dt, _ = x_sample.shape
    depth = ada_w.shape[0]
    n_even, _, n_ha, dk, dv = state_mlstm_c.shape
    n_hc, dh = cache_k.shape[3], cache_k.shape[4]
    qkw, aw, bw = n_ha * dk, n_ha * dv, conv_w.shape[2]
    n_gate = 2 * n_ha
    o_gate = 2 * qkw + 2 * aw
    dt_x = x_prompt.dtype

    wts = {
        "in_even_main": jnp.concatenate([w_in_even[:, :, :o_gate], w_in_even[:, :, o_gate + n_gate:]], axis=2).astype(BF16),
        "in_even_gate": jnp.pad(w_in_even[:, :, o_gate:o_gate + n_gate], ((0, 0), (0, 0), (0, LANES - n_gate))).astype(BF16),
        "b_gates": jnp.pad(b_gates, ((0, 0), (0, LANES - n_gate))).reshape(n_even, 1, LANES),
        "mlstm_norm_g": mlstm_norm_g.reshape(n_even, 1, aw),
        "conv_w": conv_w, "conv_b": conv_b, "conv_ln_g": conv_ln_g, "conv_ln_b": conv_ln_b,
        "out_even_a": w_out_even[:, :aw].astype(BF16), "out_even_b": w_out_even[:, aw:].astype(BF16),
        "in_odd": w_in_odd.astype(BF16), "q_norm_g": q_norm_g, "k_norm_g": k_norm_g,
        "out_odd": w_out_odd.astype(BF16),
        "ffn_up": ffn_w_up.astype(BF16), "ffn_conv_w": ffn_conv_w, "ffn_conv_b": ffn_conv_b,
        "ffn_down": ffn_w_down.astype(BF16),
    }

    mod = _adaln_mods(jnp.concatenate([c_prompt, c_sample], axis=0), ada_w, ada_b)

    def split_mods(rows):
        m = mod[:, rows].reshape(depth, 2, -1, 3, 1, d)
        return [[tuple(m[l, j, :, s] for s in range(3)) for j in range(2)] for l in range(depth)]

    mods_p = split_mods(slice(0, bsz))
    mods_s = split_mods(slice(bsz, bsz + db))

    bias_tiles, rbt = _bias_prompt(rel_bias, MOBA_BLOCK)
    far_last, new = _bias_sample(rel_bias, dt, cache_k.shape[2])

    def prompt_attend(o, q, k, v, out_dtype):
        return _attn_prompt(q, k, v, bias_tiles, rbt, out_dtype=out_dtype)

    def sample_attend(o, q, k, v, out_dtype):
        return _attn_sample(q, k, v, cache_k[o], cache_v[o], page_table, far_last, new, out_dtype=out_dtype)

    zc = jnp.zeros((n_even, bsz, n_ha, dk, dv), dt_x)
    zn = jnp.zeros((n_even, bsz, n_ha, dk), dt_x)
    zm = jnp.zeros((n_even, bsz, n_ha), dt_x)
    zconv = jnp.zeros((n_even, bsz) + state_conv.shape[2:], dt_x)
    zffn = jnp.zeros((depth, bsz) + state_ffn.shape[2:], dt_x)

    y_p, p_c, p_n, p_m, p_conv, p_k, p_v, p_ffn = _trunk(
        x_prompt, mods_p, zc, zn, zm, zconv, zffn, wts, prompt_attend)
    y_s, s_c, s_n, s_m, s_conv, s_k, s_v, s_ffn = _trunk(
        x_sample, mods_s, state_mlstm_c, state_mlstm_n, state_mlstm_m, state_conv, state_ffn, wts, sample_attend)

    def heads(a):
        return a.reshape(a.shape[:3] + (n_hc, dh))

    return (y_p, y_s, p_c, p_n, p_m, p_conv, heads(p_k), heads(p_v), p_ffn,
            s_c, s_n, s_m, s_conv, heads(s_k), heads(s_v), s_ffn)
```

```python
import functools
import math

import jax
import jax.numpy as jnp
from jax import lax
from jax.experimental import pallas as pl
from jax.experimental.pallas import tpu as pltpu

F32 = jnp.float32
BF16 = jnp.bfloat16
HIGHEST = lax.Precision.HIGHEST

LANES = 128
SUBLANES = 8
VMEM_LIMIT_BYTES = 56 * 1024 * 1024

EPS = 1e-6
LN_EPS = 1e-5
MLSTM_CHUNK = 256
MLSTM_MIN_CHUNK = 32
MOBA_BLOCK = 256
MOBA_TOPK = 3
REL_BUCKETS = 32
REL_MAX_DIST = 128
NEG = -1e30
CONV_HALO = 32

_NT = (((1,), (1,)), ((), ()))
_TN = (((0,), (0,)), ((), ()))


def _cparams(n_axes):
    return pltpu.CompilerParams(dimension_semantics=("arbitrary",) * n_axes, vmem_limit_bytes=VMEM_LIMIT_BYTES)


def _sigmoid(x):
    return jax.nn.sigmoid(x)


def _col_tile(n, pref):
    tile = min(pref, n)
    while n % tile:
        tile -= LANES
    return tile


def _row_tiling(bn, t, tm):
    if t >= tm:
        assert t % tm == 0
        tpb = t // tm
        return 1, tm, bn * tpb, tpb
    gb = min(bn, tm // t)
    assert bn % gb == 0 and t % SUBLANES == 0
    return gb, t, bn // gb, 1


def _mod_kernel(c_ref, w_ref, b_ref, o_ref):
    c = c_ref[...]
    s = (c * _sigmoid(c)).astype(BF16)
    o_ref[...] = jnp.dot(s, w_ref[...].astype(BF16), preferred_element_type=F32) + b_ref[...]


def _adaln_mods(c_all, ada_w, ada_b, tn=512):
    depth, two, d, n3 = ada_w.shape
    nlj = depth * two
    rows = c_all.shape[0]
    tn = _col_tile(n3, tn)
    return pl.pallas_call(
        _mod_kernel,
        grid=(nlj, n3 // tn),
        in_specs=[pl.BlockSpec((rows, d), lambda l, j: (0, 0)),
                  pl.BlockSpec((None, d, tn), lambda l, j: (l, 0, j)),
                  pl.BlockSpec((None, 1, tn), lambda l, j: (l, 0, j))],
        out_specs=pl.BlockSpec((None, rows, tn), lambda l, j: (l, 0, j)),
        out_shape=jax.ShapeDtypeStruct((nlj, rows, n3), F32),
        compiler_params=_cparams(2), name="adaln_mods",
    )(c_all, ada_w.reshape(nlj, d, n3), ada_b.reshape(nlj, 1, n3))


def _nm_matmul_kernel(x_ref, sh_ref, sc_ref, w_ref, *rest, tiles, normed):
    n_g = sum(normed)
    g_refs, o_refs, h_ref = rest[:n_g], rest[n_g:-1], rest[-1]
    j = pl.program_id(1)

    @pl.when(j == 0)
    def _():
        x = x_ref[...]
        r = lax.rsqrt(jnp.mean(x * x, axis=-1, keepdims=True) + EPS)
        h = (x * r) * (1.0 + sc_ref[...]) + sh_ref[...]
        h_ref[...] = h.reshape(h_ref.shape).astype(BF16)

    acc = jnp.dot(h_ref[...], w_ref[...], preferred_element_type=F32)

    def emit(o_ref, g_ref):
        if g_ref is None:
            o_ref[...] = acc.astype(o_ref.dtype)
            return
        g = g_ref[...]
        for s in range(acc.shape[1] // LANES):
            y = acc[:, s * LANES:(s + 1) * LANES]
            y = y * lax.rsqrt(jnp.mean(y * y, axis=-1, keepdims=True) + EPS) * g
            o_ref[:, s * LANES:(s + 1) * LANES] = y.astype(o_ref.dtype)

    g_iter = iter(g_refs)
    start = 0
    for o_ref, n_tiles, has_g in zip(o_refs, tiles, normed):
        g_ref = next(g_iter) if has_g else None
        if len(tiles) == 1:
            emit(o_ref, g_ref)
        else:
            pl.when((j >= start) & (j < start + n_tiles))(functools.partial(emit, o_ref, g_ref))
        start += n_tiles


def _nm_matmul(x3, shift, scale, w, sections, *, tm=1024, tn=1024, name="nm_matmul"):
    bn, t, d = x3.shape
    gb, rows, n_i, tpb = _row_tiling(bn, t, tm)
    tmm = gb * rows
    tn = _col_tile(math.gcd(*[n for n, _ in sections]), tn)
    tiles = [n // tn for n, _ in sections]
    starts = [sum(tiles[:s]) for s in range(len(tiles))]
    gains = [g.reshape(1, LANES) for _, g in sections if g is not None]
    in_specs = ([pl.BlockSpec((gb, rows, d), lambda i, j: (i // tpb, i % tpb, 0)),
                 pl.BlockSpec((gb, 1, d), lambda i, j: (i // tpb, 0, 0)),
                 pl.BlockSpec((gb, 1, d), lambda i, j: (i // tpb, 0, 0)),
                 pl.BlockSpec((d, tn), lambda i, j: (0, j))]
                + [pl.BlockSpec((1, LANES), lambda i, j: (0, 0))] * len(gains))
    clamp = lambda s: (lambda i, j: (i, jnp.clip(j - starts[s], 0, tiles[s] - 1)))
    return pl.pallas_call(
        functools.partial(_nm_matmul_kernel, tiles=tiles, normed=[g is not None for _, g in sections]),
        grid=(n_i, sum(tiles)),
        in_specs=in_specs,
        out_specs=[pl.BlockSpec((tmm, tn), clamp(s)) for s in range(len(sections))],
        out_shape=[jax.ShapeDtypeStruct((bn * t, n), F32) for n, _ in sections],
        scratch_shapes=[pltpu.VMEM((tmm, d), BF16)],
        compiler_params=_cparams(2), name=name,
    )(x3, shift, scale, w, *gains)


def _mm_res_kernel(*refs, n_in):
    a_refs, w_refs = refs[:n_in], refs[n_in:2 * n_in]
    x_ref, gate_ref, o_ref = refs[2 * n_in:]
    acc = None
    for a_ref, w_ref in zip(a_refs, w_refs):
        part = jnp.dot(a_ref[...].astype(BF16), w_ref[...], preferred_element_type=F32)
        acc = part if acc is None else acc + part
    x = x_ref[...]
    o_ref[...] = x + gate_ref[...] * acc.reshape(x.shape)


def _mm_res(a_list, w_list, x3, gate, *, tm=1024, tn=1024, name="mm_res"):
    bn, t, d = x3.shape
    gb, rows, n_i, tpb = _row_tiling(bn, t, tm)
    tmm = gb * rows
    tn = _col_tile(d, tn)
    n_in = len(a_list)
    in_specs = ([pl.BlockSpec((tmm, a.shape[1]), lambda i, j: (i, 0)) for a in a_list]
                + [pl.BlockSpec((w.shape[0], tn), lambda i, j: (0, j)) for w in w_list]
                + [pl.BlockSpec((gb, rows, tn), lambda i, j: (i // tpb, i % tpb, j)),
                   pl.BlockSpec((gb, 1, tn), lambda i, j: (i // tpb, 0, j))])
    return pl.pallas_call(
        functools.partial(_mm_res_kernel, n_in=n_in),
        grid=(n_i, d // tn),
        in_specs=in_specs,
        out_specs=pl.BlockSpec((gb, rows, tn), lambda i, j: (i // tpb, i % tpb, j)),
        out_shape=jax.ShapeDtypeStruct((bn, t, d), F32),
        compiler_params=_cparams(2), name=name,
    )(*a_list, *w_list, x3, gate)


def _mlstm_kernel(q_ref, k_ref, v_ref, zo_ref, g_ref, bg_ref, gn_ref, c0_ref, n0_ref, m0_ref,
                  h_ref, co_ref, no_ref, mo_ref, c_s, n_s, m_s, *, tb, chunk):
    ck = pl.program_id(1)
    L = chunk
    n_heads, dk, dv = c_s.shape

    @pl.when(ck == 0)
    def _():
        c_s[...] = c0_ref[0]
        n_s[...] = n0_ref[0]
        m_s[...] = m0_ref[0]

    def pad_rows(x):
        if tb == L:
            return x
        return jnp.concatenate([x, jnp.zeros((L - tb, x.shape[1]), x.dtype)], axis=0)

    gates = pad_rows(g_ref[0] + bg_ref[...])
    lane = lax.broadcasted_iota(jnp.int32, (L, LANES), 1)
    li_all = jnp.where(lane < n_heads, gates, 0.0)
    lf_all = jnp.where((lane >= n_heads) & (lane < 2 * n_heads),
                       -(jnp.maximum(-gates, 0.0) + jnp.log1p(jnp.exp(-jnp.abs(gates)))), 0.0)
    if tb != L:
        real = lax.broadcasted_iota(jnp.int32, (L, LANES), 0) < tb
        li_all = jnp.where(real | (lane >= n_heads), li_all, NEG)
        lf_all = jnp.where(real, lf_all, 0.0)
    ri = lax.broadcasted_iota(jnp.int32, (L, L), 0)
    ci = lax.broadcasted_iota(jnp.int32, (L, L), 1)
    tri = ri >= ci
    y = li_all + jnp.dot(tri.astype(F32), lf_all, precision=HIGHEST, preferred_element_type=F32)
    er = lax.broadcasted_iota(jnp.int32, (SUBLANES, LANES), 0)
    el = lax.broadcasted_iota(jnp.int32, (SUBLANES, LANES), 1)
    e = jnp.where(er < n_heads, jnp.where(el == er + n_heads, 1.0, 0.0) - jnp.where(el == er, 1.0, 0.0), 0.0)
    rt = lax.dot_general(e, y, _NT, precision=HIGHEST, preferred_element_type=F32)

    for hd in range(n_heads):
        li = y[:, hd:hd + 1]
        b = y[:, n_heads + hd:n_heads + hd + 1]
        r_row = rt[hd:hd + 1, :]
        q = pad_rows(q_ref[0, :, hd * dk:(hd + 1) * dk]) * (dk ** -0.5)
        k = pad_rows(k_ref[0, :, hd * dk:(hd + 1) * dk])
        v = pad_rows(v_ref[0, :, hd * dv:(hd + 1) * dv])
        a = b + m_s[hd][:, :1]
        dmat = jnp.where(tri, b - r_row, NEG)
        mt = jnp.maximum(a, jnp.max(dmat, axis=-1, keepdims=True))
        w_inter = jnp.exp(a - mt)
        qb, kb, vb = q.astype(BF16), k.astype(BF16), v.astype(BF16)
        s = lax.dot_general(qb, kb, _NT, preferred_element_type=F32) * jnp.exp(dmat - mt)
        c_old = c_s[hd]
        n_old = n_s[hd]
        num = (w_inter * jnp.dot(qb, c_old.astype(BF16), preferred_element_type=F32)
               + jnp.dot(s.astype(BF16), vb, preferred_element_type=F32))
        den = w_inter * jnp.sum(q * n_old, axis=-1, keepdims=True) + jnp.sum(s, axis=-1, keepdims=True)
        hh = (num / jnp.maximum(jnp.abs(den), jnp.exp(-mt)))[:tb]
        hn = hh * lax.rsqrt(jnp.mean(hh * hh, axis=-1, keepdims=True) + EPS)
        h_ref[0, :, hd * dv:(hd + 1) * dv] = (hn * gn_ref[:, hd * dv:(hd + 1) * dv]
                                              * _sigmoid(zo_ref[0, :, hd * dv:(hd + 1) * dv])).astype(h_ref.dtype)

        m_l = mt[L - 1:L]
        g_inter = jnp.exp(a[L - 1:L] - m_l)
        g_s = jnp.exp(b[L - 1:L] - b + li - m_l)
        c_s[hd] = g_inter * c_old + lax.dot_general(kb, (g_s * v).astype(BF16), _TN, preferred_element_type=F32)
        n_s[hd] = g_inter * n_old + jnp.sum(g_s * k, axis=0, keepdims=True)
        m_s[hd] = jnp.broadcast_to(m_l, (1, LANES))

    @pl.when(ck == pl.num_programs(1) - 1)
    def _():
        co_ref[0] = c_s[...]
        no_ref[0] = n_s[...]
        mo_ref[0] = m_s[...]


def _mlstm(z3, g3, bg, gn, c0, n0, m0, *, out_dtype, name):
    bn, t, _ = z3.shape
    _, n_heads, dk, dv = c0.shape
    assert dk == dv and dk % LANES == 0 and 2 * n_heads <= LANES and n_heads <= SUBLANES
    if t % MLSTM_CHUNK == 0:
        tb = chunk = MLSTM_CHUNK
    else:
        assert t <= MLSTM_MIN_CHUNK
        tb, chunk = t, MLSTM_MIN_CHUNK
    n_c = t // tb
    hw = n_heads * dk
    col = lambda sec: (lambda b, c: (b, c, sec))
    state = lambda b, c: (b, 0, 0, 0)
    h, c, n, m = pl.pallas_call(
        functools.partial(_mlstm_kernel, tb=tb, chunk=chunk),
        grid=(bn, n_c),
        in_specs=[pl.BlockSpec((1, tb, hw), col(0)), pl.BlockSpec((1, tb, hw), col(1)),
                  pl.BlockSpec((1, tb, hw), col(2)), pl.BlockSpec((1, tb, hw), col(3)),
                  pl.BlockSpec((1, tb, LANES), lambda b, c: (b, c, 0)),
                  pl.BlockSpec((1, LANES), lambda b, c: (0, 0)),
                  pl.BlockSpec((1, hw), lambda b, c: (0, 0)),
                  pl.BlockSpec((1, n_heads, dk, dv), state), pl.BlockSpec((1, n_heads, 1, dk), state),
                  pl.BlockSpec((1, n_heads, 1, LANES), state)],
        out_specs=[pl.BlockSpec((1, tb, hw), lambda b, c: (b, c, 0)),
                   pl.BlockSpec((1, n_heads, dk, dv), state), pl.BlockSpec((1, n_heads, 1, dk), state),
                   pl.BlockSpec((1, n_heads, 1, LANES), state)],
        out_shape=[jax.ShapeDtypeStruct((bn, t, hw), out_dtype),
                   jax.ShapeDtypeStruct((bn, n_heads, dk, dv), F32),
                   jax.ShapeDtypeStruct((bn, n_heads, 1, dk), F32),
                   jax.ShapeDtypeStruct((bn, n_heads, 1, LANES), F32)],
        scratch_shapes=[pltpu.VMEM((n_heads, dk, dv), F32), pltpu.VMEM((n_heads, 1, dk), F32),
                        pltpu.VMEM((n_heads, 1, LANES), F32)],
        compiler_params=_cparams(2), name=name,
    )(z3, z3, z3, z3, g3, bg, gn, c0, n0.reshape(bn, n_heads, 1, dk),
      jnp.broadcast_to(m0[:, :, None, None], (bn, n_heads, 1, LANES)))
    return h, c, n.reshape(bn, n_heads, dk), m[:, :, 0, 0]


def _convmod_kernel(a_ref, g_ref, buf_ref, w_ref, b_ref, lg_ref, lb_ref, y_ref, nb_ref, u_s, *, tt, width, rc):
    t = pl.program_id(1)
    hist = width - 1

    @pl.when(t == 0)
    def _():
        buf = buf_ref[0]
        for r in range(SUBLANES):
            u_s[r, 0:hist - r, :] = buf[r:, :]

    @pl.when(t > 0)
    def _():
        for r in range(SUBLANES):
            u_s[r, 0:CONV_HALO, :] = u_s[r, tt:tt + CONV_HALO, :]

    u = a_ref[0] * _sigmoid(g_ref[0])
    for r in range(SUBLANES):
        u_s[r, hist - r:hist - r + tt, :] = u
    bias = b_ref[...]
    lg = lg_ref[...]
    lb = lb_ref[...]
    n_p = rc // SUBLANES
    for r0 in range(0, tt, rc):
        accs = [jnp.broadcast_to(bias, (SUBLANES, bias.shape[1]))] * n_p
        for j in range(width):
            r = j % SUBLANES
            wj = w_ref[j]
            accs = [acc + wj * u_s[r, r0 + p * SUBLANES + j - r:r0 + (p + 1) * SUBLANES + j - r, :]
                    for p, acc in enumerate(accs)]
        acc = accs[0] if n_p == 1 else jnp.concatenate(accs, axis=0)
        mu = jnp.mean(acc, axis=-1, keepdims=True)
        xc = acc - mu
        var = jnp.mean(xc * xc, axis=-1, keepdims=True)
        y = xc * lax.rsqrt(var + LN_EPS) * lg + lb
        y_ref[0, r0:r0 + rc, :] = (y * _sigmoid(y)).astype(y_ref.dtype)

    @pl.when(t == pl.num_programs(1) - 1)
    def _():
        nb_ref[0] = u_s[0, tt:tt + hist, :]


def _conv_module(z3, col_a, col_g, cbuf, conv_w, conv_b, ln_g, ln_b, *, out_dtype, name):
    bn, t, _ = z3.shape
    width, ch = conv_w.shape
    assert width - 1 <= CONV_HALO
    tt = 128 if t % 128 == 0 else t
    n_t = t // tt
    assert n_t == 1 or tt >= CONV_HALO
    y, nb = pl.pallas_call(
        functools.partial(_convmod_kernel, tt=tt, width=width,
                          rc=4 * SUBLANES if tt % (4 * SUBLANES) == 0 else SUBLANES),
        grid=(bn, n_t),
        in_specs=[pl.BlockSpec((1, tt, ch), lambda b, i: (b, i, col_a)),
                  pl.BlockSpec((1, tt, ch), lambda b, i: (b, i, col_g)),
                  pl.BlockSpec((1, width - 1, ch), lambda b, i: (b, 0, 0)),
                  pl.BlockSpec((width, SUBLANES, ch), lambda b, i: (0, 0, 0)),
                  pl.BlockSpec((1, ch), lambda b, i: (0, 0)),
                  pl.BlockSpec((1, ch), lambda b, i: (0, 0)),
                  pl.BlockSpec((1, ch), lambda b, i: (0, 0))],
        out_specs=[pl.BlockSpec((1, tt, ch), lambda b, i: (b, i, 0)),
                   pl.BlockSpec((1, width - 1, ch), lambda b, i: (b, 0, 0))],
        out_shape=[jax.ShapeDtypeStruct((bn, t, ch), out_dtype),
                   jax.ShapeDtypeStruct((bn, width - 1, ch), F32)],
        scratch_shapes=[pltpu.VMEM((SUBLANES, CONV_HALO + tt, ch), F32)],
        compiler_params=_cparams(2), name=name,
    )(z3, z3, cbuf, jnp.broadcast_to(conv_w[:, None, :], (width, SUBLANES, ch)), conv_b.reshape(1, ch),
      ln_g.reshape(1, ch), ln_b.reshape(1, ch))
    return y, nb


def _ffn_up_kernel(x_ref, sh_ref, sc_ref, wg_ref, wu_ref, buf_ref, cw_ref, cb_ref, act_ref, gl_ref, h_s, carry_s,
                   *, tpb, rows):
    ti = pl.program_id(0) % tpb
    j = pl.program_id(1)

    @pl.when(j == 0)
    def _():
        x = x_ref[...]
        r = lax.rsqrt(jnp.mean(x * x, axis=-1, keepdims=True) + EPS)
        h = (x * r) * (1.0 + sc_ref[...]) + sh_ref[...]
        h_s[...] = h.reshape(h_s.shape).astype(BF16)

    h = h_s[...]
    g = jnp.dot(h, wg_ref[...], preferred_element_type=F32)
    u = jnp.dot(h, wu_ref[...], preferred_element_type=F32)
    n, tn = g.shape
    gb = n // rows
    if tpb > 1:
        @pl.when(ti == 0)
        def _():
            carry_s[j] = buf_ref[0]

        prev = carry_s[j]
        carry_s[j] = pltpu.roll(g[n - SUBLANES:], 2, axis=0)
    else:
        prev = buf_ref[...].reshape(gb * SUBLANES, tn)
    r8 = lax.broadcasted_iota(jnp.int32, (gb * SUBLANES, tn), 0) % SUBLANES
    prev1 = pltpu.roll(prev, gb * SUBLANES - 1, axis=0)
    gm1 = pltpu.roll(g, 1, axis=0)
    gm2 = pltpu.roll(g, 2, axis=0)
    if rows == SUBLANES:
        gm1 = jnp.where(r8 < 1, prev1, gm1)
        gm2 = jnp.where(r8 < 2, prev, gm2)
        gl_ref[...] = g.reshape(gl_ref.shape)
    else:
        gm1 = jnp.concatenate([jnp.where(r8 < 1, prev1, gm1[:SUBLANES]), gm1[SUBLANES:]], axis=0)
        gm2 = jnp.concatenate([jnp.where(r8 < 2, prev, gm2[:SUBLANES]), gm2[SUBLANES:]], axis=0)
        gl_ref[...] = g[n - SUBLANES:].reshape(gl_ref.shape)
    gc = cw_ref[0:1, :] * gm2 + cw_ref[1:2, :] * gm1 + cw_ref[2:3, :] * g + cb_ref[...]
    act_ref[...] = ((gc * _sigmoid(gc)) * u).astype(act_ref.dtype)


def _ffn_up(x3, shift, scale, w, buf8, conv_w, conv_b, *, tm=1024, tn=512, name="ffn_up"):
    bn, t, d = x3.shape
    dff = w.shape[1] // 2
    gb, rows, n_i, tpb = _row_tiling(bn, t, tm)
    tmm = gb * rows
    tn = _col_tile(dff, tn)
    n_f = dff // tn
    assert conv_w.shape[0] == 3 and (tpb == 1 or gb == 1)
    act, g_tail = pl.pallas_call(
        functools.partial(_ffn_up_kernel, tpb=tpb, rows=rows),
        grid=(n_i, n_f),
        in_specs=[pl.BlockSpec((gb, rows, d), lambda i, j: (i // tpb, i % tpb, 0)),
                  pl.BlockSpec((gb, 1, d), lambda i, j: (i // tpb, 0, 0)),
                  pl.BlockSpec((gb, 1, d), lambda i, j: (i // tpb, 0, 0)),
                  pl.BlockSpec((d, tn), lambda i, j: (0, j)),
                  pl.BlockSpec((d, tn), lambda i, j: (0, j + n_f)),
                  pl.BlockSpec((gb, SUBLANES, tn), lambda i, j: (i // tpb, 0, j)),
                  pl.BlockSpec((3, tn), lambda i, j: (0, j)),
                  pl.BlockSpec((1, tn), lambda i, j: (0, j))],
        out_specs=[pl.BlockSpec((tmm, tn), lambda i, j: (i, j)),
                   pl.BlockSpec((gb, SUBLANES, tn), lambda i, j: (i, 0, j))],
        out_shape=[jax.ShapeDtypeStruct((bn * t, dff), BF16),
                   jax.ShapeDtypeStruct((n_i * gb, SUBLANES, dff), F32)],
        scratch_shapes=[pltpu.VMEM((tmm, d), BF16), pltpu.VMEM((n_f, SUBLANES, tn), F32)],
        compiler_params=_cparams(2), name=name,
    )(x3, shift, scale, w, w, buf8, conv_w, conv_b.reshape(1, dff))
    return act, g_tail.reshape(bn, tpb, SUBLANES, dff)[:, tpb - 1]


def _t5_bucket(dist):
    n = jnp.maximum(dist, 0)
    exact = REL_BUCKETS // 2
    large = exact + (jnp.log(jnp.maximum(n, 1).astype(F32) / exact)
                     / math.log(REL_MAX_DIST / exact) * (REL_BUCKETS - exact)).astype(jnp.int32)
    return jnp.where(n < exact, n, jnp.minimum(large, REL_BUCKETS - 1))


def _bias_lookup(dist, table_col):
    bucket = _t5_bucket(dist)
    bias = jnp.zeros(dist.shape, F32)
    for kk in range(REL_BUCKETS):
        bias = jnp.where(bucket == kk, table_col(kk), bias)
    return bias


def _bias_prompt_kernel(rb_ref, o_ref, *, blk):
    rb = rb_ref[0]
    key = lax.broadcasted_iota(jnp.int32, (blk, blk), 0)
    qry = lax.broadcasted_iota(jnp.int32, (blk, blk), 1)
    col = lambda kk: rb[:, kk:kk + 1]
    own = _bias_lookup(qry - key, col)
    o_ref[0, 0] = jnp.where(qry >= key, own, NEG)
    o_ref[0, 1] = _bias_lookup(blk + qry - key, col)


def _bias_prompt(rel_bias, blk):
    n_b, n_h = rel_bias.shape
    rbt = jnp.pad(rel_bias.T, ((0, 0), (0, LANES - n_b))).reshape(n_h, 1, LANES)
    return pl.pallas_call(
        functools.partial(_bias_prompt_kernel, blk=blk),
        grid=(n_h,),
        in_specs=[pl.BlockSpec((1, 1, LANES), lambda h: (h, 0, 0))],
        out_specs=pl.BlockSpec((1, 2, blk, blk), lambda h: (h, 0, 0, 0)),
        out_shape=jax.ShapeDtypeStruct((n_h, 2, blk, blk), F32),
        compiler_params=_cparams(1), name="bias_prompt",
    )(rbt), rbt


def _bias_sample_kernel(rb_ref, fl_ref, new_ref, *, n_heads, page):
    rb = rb_ref[...]
    col = lambda kk: rb[:, kk:kk + 1]
    for ref, base in ((fl_ref.at[0], 2 * page), (fl_ref.at[1], page), (new_ref, 0)):
        shape = ref.shape
        r = lax.broadcasted_iota(jnp.int32, shape, 0)
        c = lax.broadcasted_iota(jnp.int32, shape, 1)
        dist = base + r // n_heads - c // n_heads
        ok = (r % n_heads == c % n_heads) & (dist >= 0)
        ref[...] = jnp.where(ok, _bias_lookup(dist, col), NEG)


def _bias_sample(rel_bias, t, page):
    n_b, n_h = rel_bias.shape
    rows = t * n_h
    rb = jnp.pad(jnp.tile(rel_bias.T, (t, 1)), ((0, 0), (0, LANES - n_b)))
    return pl.pallas_call(
        functools.partial(_bias_sample_kernel, n_heads=n_h, page=page),
        out_shape=[jax.ShapeDtypeStruct((2, rows, page * n_h), F32),
                   jax.ShapeDtypeStruct((rows, rows), F32)],
        compiler_params=pltpu.CompilerParams(vmem_limit_bytes=VMEM_LIMIT_BYTES), name="bias_sample",
    )(rb)


def _attn_prompt_kernel(q_ref, k_ref, v_ref, bias_ref, rb_ref, o_ref, km_s, kb_s, vt_s, s_s, *, blk, n_blk, scale):
    km_s[...] = jnp.zeros_like(km_s)
    for j in range(n_blk):
        kj = k_ref[0, j * blk:(j + 1) * blk, :]
        km_s[j:j + 1, :] = jnp.mean(kj, axis=0, keepdims=True)
        kb_s[j * blk:(j + 1) * blk, :] = kj.astype(BF16)
        vt_s[:, j * blk:(j + 1) * blk] = v_ref[0, j * blk:(j + 1) * blk, :].T.astype(BF16)

    far = rb_ref[0][:, REL_BUCKETS - 1:REL_BUCKETS]
    km = km_s[...]
    row = lax.broadcasted_iota(jnp.int32, (km.shape[0], blk), 0)
    tile = 0
    for c in range(n_blk):
        qf = q_ref[0, c * blk:(c + 1) * blk, :]
        qb = qf.astype(BF16)
        add = [0.0] * c
        if c > MOBA_TOPK:
            sc = lax.dot_general(km, qf, _NT, precision=HIGHEST, preferred_element_type=F32)
            sc = jnp.where(row < c, sc, -jnp.inf)
            for j in range(c):
                sj = sc[j:j + 1, :]
                ahead = (sc > sj) | ((sc == sj) & (row < j))
                rank = jnp.sum(jnp.where(ahead, 1.0, 0.0), axis=0, keepdims=True)
                add[j] = jnp.where(rank < MOBA_TOPK, 0.0, NEG)
        m = None
        for j in range(c + 1):
            st = lax.dot_general(kb_s[j * blk:(j + 1) * blk, :], qb, _NT, preferred_element_type=F32) * scale
            if j == c:
                st = st + bias_ref[0, 0]
            elif j == c - 1:
                st = st + (bias_ref[0, 1] + add[j])
            else:
                st = st + (add[j] + far)
            s_s[tile + j] = st
            mj = jnp.max(st, axis=0, keepdims=True)
            m = mj if m is None else jnp.maximum(m, mj)
        l = jnp.zeros_like(m)
        acc = jnp.zeros((vt_s.shape[0], blk), F32)
        for j in range(c + 1):
            p = jnp.exp(s_s[tile + j] - m)
            l = l + jnp.sum(p, axis=0, keepdims=True)
            acc = acc + jnp.dot(vt_s[:, j * blk:(j + 1) * blk], p.astype(BF16), preferred_element_type=F32)
        o_ref[0, c * blk:(c + 1) * blk, :] = (acc / l).T.astype(o_ref.dtype)
        tile += c + 1


def _attn_prompt(q3, k3, v3, bias_tiles, rbt, *, out_dtype):
    bn, s, _ = q3.shape
    n_h = bias_tiles.shape[0]
    dh = q3.shape[2] // n_h
    blk = MOBA_BLOCK
    assert s % blk == 0 and dh == LANES and blk >= REL_MAX_DIST
    n_blk = s // blk
    head = lambda b, h: (b, 0, h)
    return pl.pallas_call(
        functools.partial(_attn_prompt_kernel, blk=blk, n_blk=n_blk, scale=dh ** -0.5),
        grid=(bn, n_h),
        in_specs=[pl.BlockSpec((1, s, dh), head), pl.BlockSpec((1, s, dh), head), pl.BlockSpec((1, s, dh), head),
                  pl.BlockSpec((1, 2, blk, blk), lambda b, h: (h, 0, 0, 0)),
                  pl.BlockSpec((1, 1, LANES), lambda b, h: (h, 0, 0))],
        out_specs=pl.BlockSpec((1, s, dh), head),
        out_shape=jax.ShapeDtypeStruct(q3.shape, out_dtype),
        scratch_shapes=[pltpu.VMEM((-(-n_blk // SUBLANES) * SUBLANES, dh), F32), pltpu.VMEM((s, dh), BF16),
                        pltpu.VMEM((dh, s), BF16), pltpu.VMEM((n_blk * (n_blk + 1) // 2, blk, blk), F32)],
        compiler_params=_cparams(2), name="moba_prompt",
    )(q3, k3, v3, bias_tiles, rbt)


def _attn_sample_kernel(pt_ref, q_ref, kn_ref, vn_ref, *refs, n_heads, ppb, pps, scale):
    del pt_ref
    kp_refs, vp_refs = refs[:pps], refs[pps:2 * pps]
    bias_ref, new_ref, o_ref, ks_s, m_s, l_s, o_s = refs[2 * pps:]
    g = pl.program_id(1)
    last = pl.num_programs(1) - 1
    qf = q_ref[0]
    qb = qf.astype(BF16)
    rows, dh = qf.shape

    def partial_softmax(kk, vv, bias):
        s = lax.dot_general(qb, kk.astype(BF16), _NT, preferred_element_type=F32) * scale + bias
        m = jnp.max(s, axis=-1, keepdims=True)
        e = jnp.exp(s - m)
        return m, jnp.sum(e, axis=-1, keepdims=True), jnp.dot(e.astype(BF16), vv.astype(BF16),
                                                              preferred_element_type=F32)

    for r in range(pps):
        pg = g * pps + r
        kp = kp_refs[r][0]
        ks_s[pg] = jnp.sum(kp.reshape(kp.shape[0] // n_heads, n_heads, dh), axis=0)
        bias = bias_ref[jnp.where(g == last, 1, 0)] if r == pps - 1 else bias_ref[0]
        m, l, o = partial_softmax(kp, vp_refs[r][0], bias)
        m_s[pg] = m
        l_s[pg] = l
        o_s[pg] = o

    @pl.when(g == last)
    def _():
        n_pg = m_s.shape[0]
        n_blk = n_pg // ppb
        inv = 1.0 / (ppb * (kp_refs[0].shape[1] // n_heads))
        scs = []
        for j in range(n_blk):
            km = ks_s[j * ppb]
            for t in range(1, ppb):
                km = km + ks_s[j * ppb + t]
            km = jnp.concatenate([km * inv] * (rows // n_heads), axis=0)
            scs.append(jnp.sum(qf * km, axis=-1, keepdims=True))
        sel = []
        for j in range(n_blk):
            rank = jnp.zeros((rows, 1), F32)
            for jj in range(n_blk):
                if jj != j:
                    ahead = (scs[jj] >= scs[j]) if jj < j else (scs[jj] > scs[j])
                    rank = rank + jnp.where(ahead, 1.0, 0.0)
            sel.append(rank < MOBA_TOPK)
        m_n, l_n, o_n = partial_softmax(kn_ref[0], vn_ref[0], new_ref[...])
        m_all = m_n
        for pg in range(n_pg):
            m_all = jnp.maximum(m_all, jnp.where(sel[pg // ppb], m_s[pg], NEG))
        w_n = jnp.exp(m_n - m_all)
        l_all = w_n * l_n
        o_all = w_n * o_n
        for pg in range(n_pg):
            w = jnp.where(sel[pg // ppb], jnp.exp(m_s[pg] - m_all), 0.0)
            l_all = l_all + w * l_s[pg]
            o_all = o_all + w * o_s[pg]
        o_ref[0] = (o_all / l_all).astype(o_ref.dtype)


def _attn_sample(q, k, v, cache_k, cache_v, page_table, far_last, new, *, out_dtype, pps=4):
    db, t, hd = q.shape
    n_pool, page, n_h, dh = cache_k.shape
    n_pages = page_table.shape[1]
    rows = t * n_h
    assert MOBA_BLOCK % page == 0 and (n_pages * page) % MOBA_BLOCK == 0 and t <= MOBA_BLOCK
    assert page >= REL_MAX_DIST and n_pages * page // MOBA_BLOCK > MOBA_TOPK and dh == LANES
    assert n_pages % pps == 0
    ppb = MOBA_BLOCK // page
    as_rows = lambda a: a.reshape(db, rows, dh)
    seq = lambda b, g, pt: (b, 0, 0)
    paged = [(lambda b, g, pt, r=r: (pt[b, g * pps + r], 0, 0)) for r in range(pps)]
    page_spec = lambda r: pl.BlockSpec((1, page * n_h, dh), paged[r])
    grid_spec = pltpu.PrefetchScalarGridSpec(
        num_scalar_prefetch=1,
        grid=(db, n_pages // pps),
        in_specs=([pl.BlockSpec((1, rows, dh), seq)] * 3 + [page_spec(r) for r in range(pps)] * 2
                  + [pl.BlockSpec(far_last.shape, lambda b, g, pt: (0, 0, 0)),
                     pl.BlockSpec(new.shape, lambda b, g, pt: (0, 0))]),
        out_specs=pl.BlockSpec((1, rows, dh), seq),
        scratch_shapes=[pltpu.VMEM((n_pages, n_h, dh), F32), pltpu.VMEM((n_pages, rows, 1), F32),
                        pltpu.VMEM((n_pages, rows, 1), F32), pltpu.VMEM((n_pages, rows, dh), F32)],
    )
    ck = cache_k.reshape(n_pool, page * n_h, dh)
    cv = cache_v.reshape(n_pool, page * n_h, dh)
    out = pl.pallas_call(
        functools.partial(_attn_sample_kernel, n_heads=n_h, ppb=ppb, pps=pps, scale=dh ** -0.5),
        grid_spec=grid_spec,
        out_shape=jax.ShapeDtypeStruct((db, rows, dh), out_dtype),
        compiler_params=_cparams(2), name="moba_sample",
    )(page_table, as_rows(q), as_rows(k), as_rows(v), *([ck] * pps), *([cv] * pps), far_last, new)
    return out.reshape(db, t, hd)


def _trunk(x, mods, mc, mn, mm, cb, fb, wts, attend):
    bn, t, d = x.shape
    depth = wts["ffn_up"].shape[0]
    n_ha = mc.shape[2]
    qkw = n_ha * mc.shape[3]
    aw = n_ha * mc.shape[4]
    bw = wts["conv_w"].shape[2]
    dff = wts["ffn_down"].shape[1]
    big = t >= 512
    act_dtype = BF16 if big else F32
    out_c, out_n, out_m, out_conv, out_k, out_v, out_ffn = [], [], [], [], [], [], []
    for l in range(depth):
        shift, scale, gate = mods[l][0]
        if l % 2 == 0:
            e = l // 2
            z, = _nm_matmul(x, shift, scale, wts["in_even_main"][e], [(2 * qkw + 2 * aw + 2 * bw, None)], name="in_even")
            z = z.reshape(bn, t, -1)
            g, = _nm_matmul(x, shift, scale, wts["in_even_gate"][e], [(LANES, None)], name="in_even_gates")
            h_a, c_new, n_new, m_new = _mlstm(z, g.reshape(bn, t, LANES), wts["b_gates"][e], wts["mlstm_norm_g"][e],
                                              mc[e], mn[e], mm[e], out_dtype=act_dtype, name="mlstm")
            assert qkw == aw and (2 * qkw + 2 * aw) % bw == 0
            cblk = (2 * qkw + 2 * aw) // bw
            y_b, cb_new = _conv_module(z, cblk, cblk + 1, cb[e], wts["conv_w"][e], wts["conv_b"][e],
                                       wts["conv_ln_g"][e], wts["conv_ln_b"][e], out_dtype=act_dtype, name="conv_module")
            x = _mm_res([h_a.reshape(bn * t, aw), y_b.reshape(bn * t, bw)],
                        [wts["out_even_a"][e], wts["out_even_b"][e]], x, gate, name="out_even")
            out_c.append(c_new); out_n.append(n_new); out_m.append(m_new); out_conv.append(cb_new)
        else:
            o = l // 2
            qkv = _nm_matmul(x, shift, scale, wts["in_odd"][o],
                             [(d, wts["q_norm_g"][o]), (d, wts["k_norm_g"][o]), (d, None)], tn=512, name="in_odd")
            q, k, v = (a.reshape(bn, t, d) for a in qkv)
            a = attend(o, q, k, v, act_dtype)
            x = _mm_res([a.reshape(bn * t, d)], [wts["out_odd"][o]], x, gate, name="out_odd")
            out_k.append(k); out_v.append(v)
        shift, scale, gate = mods[l][1]
        hist = fb[l].shape[1]
        assert hist <= min(t, SUBLANES)
        buf8 = jnp.pad(fb[l], ((0, 0), (0, SUBLANES - hist), (0, 0)))
        act, g_last = _ffn_up(x, shift, scale, wts["ffn_up"][l], buf8, wts["ffn_conv_w"][l], wts["ffn_conv_b"][l])
        x = _mm_res([act], [wts["ffn_down"][l]], x, gate, tn=512, name="ffn_down")
        out_ffn.append(g_last[:, SUBLANES - hist:])
    return (x, jnp.stack(out_c), jnp.stack(out_n), jnp.stack(out_m), jnp.stack(out_conv),
            jnp.stack(out_k), jnp.stack(out_v), jnp.stack(out_ffn))


def kernel(x_prompt, x_sample, state_mlstm_c, state_mlstm_n, state_mlstm_m, state_conv, cache_k, cache_v, page_table, state_ffn, c_prompt, c_sample, ada_w, ada_b, w_in_even, b_gates, mlstm_norm_g, conv_w, conv_b, conv_ln_g, conv_ln_b, w_out_even, w_in_odd, q_norm_g, k_norm_g, rel_bias, w_out_odd, ffn_w_up, ffn_conv_w, ffn_conv_b, ffn_w_down):
    bsz, seq, d = x_prompt.shape
    db, dt, _ = x_sample.shape
    depth = ada_w.shape[0]
    n_even, _, n_ha, dk, dv = state_mlstm_c.shape
    n_hc, dh = cache_k.shape[3], cache_k.shape[4]
    qkw, aw, bw = n_ha * dk, n_ha * dv, conv_w.shape[2]
    n_gate = 2 * n_ha
    o_gate = 2 * qkw + 2 * aw
    dt_x = x_prompt.dtype

    wts = {
        "in_even_main": jnp.concatenate([w_in_even[:, :, :o_gate], w_in_even[:, :, o_gate + n_gate:]], axis=2).astype(BF16),
        "in_even_gate": jnp.pad(w_in_even[:, :, o_gate:o_gate + n_gate], ((0, 0), (0, 0), (0, LANES - n_gate))).astype(BF16),
        "b_gates": jnp.pad(b_gates, ((0, 0), (0, LANES - n_gate))).reshape(n_even, 1, LANES),
        "mlstm_norm_g": mlstm_norm_g.reshape(n_even, 1, aw),
        "conv_w": conv_w, "conv_b": conv_b, "conv_ln_g": conv_ln_g, "conv_ln_b": conv_ln_b,
        "out_even_a": w_out_even[:, :aw].astype(BF16), "out_even_b": w_out_even[:, aw:].astype(BF16),
        "in_odd": w_in_odd.astype(BF16), "q_norm_g": q_norm_g, "k_norm_g": k_norm_g,
        "out_odd": w_out_odd.astype(BF16),
        "ffn_up": ffn_w_up.astype(BF16), "ffn_conv_w": ffn_conv_w, "ffn_conv_b": ffn_conv_b,
        "ffn_down": ffn_w_down.astype(BF16),
    }

    mod = _adaln_mods(jnp.concatenate([c_prompt, c_sample], axis=0), ada_w, ada_b)

    def split_mods(rows):
        m = mod[:, rows].reshape(depth, 2, -1, 3, 1, d)
        return [[tuple(m[l, j, :, s] for s in range(3)) for j in range(2)] for l in range(depth)]

    mods_p = split_mods(slice(0, bsz))
    mods_s = split_mods(slice(bsz, bsz + db))

    bias_tiles, rbt = _bias_prompt(rel_bias, MOBA_BLOCK)
    far_last, new = _bias_sample(rel_bias, dt, cache_k.shape[2])

    def prompt_attend(o, q, k, v, out_dtype):
        return _attn_prompt(q, k, v, bias_tiles, rbt, out_dtype=out_dtype)

    def sample_attend(o, q, k, v, out_dtype):
        return _attn_sample(q, k, v, cache_k[o], cache_v[o], page_table, far_last, new, out_dtype=out_dtype)

    zc = jnp.zeros((n_even, bsz, n_ha, dk, dv), dt_x)
    zn = jnp.zeros((n_even, bsz, n_ha, dk), dt_x)
    zm = jnp.zeros((n_even, bsz, n_ha), dt_x)
    zconv = jnp.zeros((n_even, bsz) + state_conv.shape[2:], dt_x)
    zffn = jnp.zeros((depth, bsz) + state_ffn.shape[2:], dt_x)

    y_p, p_c, p_n, p_m, p_conv, p_k, p_v, p_ffn = _trunk(
        x_prompt, mods_p, zc, zn, zm, zconv, zffn, wts, prompt_attend)
    y_s, s_c, s_n, s_m, s_conv, s_k, s_v, s_ffn = _trunk(
        x_sample, mods_s, state_mlstm_c, state_mlstm_n, state_mlstm_m, state_conv, state_ffn, wts, sample_attend)

    def heads(a):
        return a.reshape(a.shape[:3] + (n_hc, dh))

    return (y_p, y_s, p_c, p_n, p_m, p_conv, heads(p_k), heads(p_v), p_ffn,
            s_c, s_n, s_m, s_conv, heads(s_k), heads(s_v), s_ffn)
```

```python
import functools
import math

import jax
import jax.numpy as jnp
from jax import lax
from jax.experimental import pallas as pl
from jax.experimental.pallas import tpu as pltpu

F32 = jnp.float32
BF16 = jnp.bfloat16
HIGHEST = lax.Precision.HIGHEST

LANES = 128
SUBLANES = 8
VMEM_LIMIT_BYTES = 56 * 1024 * 1024

EPS = 1e-6
LN_EPS = 1e-5
MLSTM_CHUNK = 256
MLSTM_MIN_CHUNK = 32
MOBA_BLOCK = 256
MOBA_TOPK = 3
REL_BUCKETS = 32
REL_MAX_DIST = 128
NEG = -1e30
LOG2E = math.log2(math.e)
CONV_HALO = 32

_NT = (((1,), (1,)), ((), ()))
_TN = (((0,), (0,)), ((), ()))


def _cparams(n_axes):
    return pltpu.CompilerParams(dimension_semantics=("arbitrary",) * n_axes, vmem_limit_bytes=VMEM_LIMIT_BYTES)


def _sigmoid(x):
    return jax.nn.sigmoid(x)


def _col_tile(n, pref):
    tile = min(pref, n)
    while n % tile:
        tile -= LANES
    return tile


def _row_tiling(bn, t, tm):
    if t >= tm:
        assert t % tm == 0
        tpb = t // tm
        return 1, tm, bn * tpb, tpb
    gb = min(bn, tm // t)
    assert bn % gb == 0 and t % SUBLANES == 0
    return gb, t, bn // gb, 1


def _mod_kernel(c_ref, w_ref, b_ref, o_ref):
    c = c_ref[...]
    s = (c * _sigmoid(c)).astype(BF16)
    o_ref[...] = jnp.dot(s, w_ref[...].astype(BF16), preferred_element_type=F32) + b_ref[...]


def _adaln_mods(c_all, ada_w, ada_b, tn=512):
    depth, two, d, n3 = ada_w.shape
    nlj = depth * two
    rows = c_all.shape[0]
    tn = _col_tile(n3, tn)
    return pl.pallas_call(
        _mod_kernel,
        grid=(nlj, n3 // tn),
        in_specs=[pl.BlockSpec((rows, d), lambda l, j: (0, 0)),
                  pl.BlockSpec((None, d, tn), lambda l, j: (l, 0, j)),
                  pl.BlockSpec((None, 1, tn), lambda l, j: (l, 0, j))],
        out_specs=pl.BlockSpec((None, rows, tn), lambda l, j: (l, 0, j)),
        out_shape=jax.ShapeDtypeStruct((nlj, rows, n3), F32),
        compiler_params=_cparams(2), name="adaln_mods",
    )(c_all, ada_w.reshape(nlj, d, n3), ada_b.reshape(nlj, 1, n3))


def _nm_matmul_kernel(x_ref, sh_ref, sc_ref, *rest, normed):
    n_s, n_g = len(normed), sum(normed)
    w_refs, g_refs, o_refs, h_ref = rest[:n_s], rest[n_s:n_s + n_g], rest[n_s + n_g:-1], rest[-1]

    @pl.when(pl.program_id(1) == 0)
    def _():
        x = x_ref[...]
        r = lax.rsqrt(jnp.mean(x * x, axis=-1, keepdims=True) + EPS)
        h = (x * r) * (1.0 + sc_ref[...]) + sh_ref[...]
        h_ref[...] = h.reshape(h_ref.shape).astype(BF16)

    h = h_ref[...]
    g_iter = iter(g_refs)
    for w_ref, o_ref, has_g in zip(w_refs, o_refs, normed):
        acc = jnp.dot(h, w_ref[...], preferred_element_type=F32)
        if not has_g:
            o_ref[...] = acc.astype(o_ref.dtype)
            continue
        g = next(g_iter)[...]
        for s in range(acc.shape[1] // LANES):
            y = acc[:, s * LANES:(s + 1) * LANES]
            y = y * lax.rsqrt(jnp.mean(y * y, axis=-1, keepdims=True) + EPS) * g
            o_ref[:, s * LANES:(s + 1) * LANES] = y.astype(o_ref.dtype)


def _nm_matmul(x3, shift, scale, sections, width, *, tm=1024, tn=1024, name="nm_matmul"):
    bn, t, d = x3.shape
    gb, rows, n_i, tpb = _row_tiling(bn, t, tm)
    tmm = gb * rows
    n_s = len(sections)
    tn = _col_tile(math.gcd(width, *[c0 for _, c0, _ in sections if c0]), tn)
    n_j = width // tn
    gains = [g.reshape(1, LANES) for _, _, g in sections if g is not None]
    in_specs = ([pl.BlockSpec((gb, rows, d), lambda i, j: (i // tpb, i % tpb, 0)),
                 pl.BlockSpec((gb, 1, d), lambda i, j: (i // tpb, 0, 0)),
                 pl.BlockSpec((gb, 1, d), lambda i, j: (i // tpb, 0, 0))]
                + [pl.BlockSpec((d, tn), lambda i, j, j0=c0 // tn: (0, j + j0)) for _, c0, _ in sections]
                + [pl.BlockSpec((1, LANES), lambda i, j: (0, 0))] * len(gains))
    return pl.pallas_call(
        functools.partial(_nm_matmul_kernel, normed=[g is not None for _, _, g in sections]),
        grid=(n_i, n_j),
        in_specs=in_specs,
        out_specs=[pl.BlockSpec((tmm, tn), lambda i, j: (i, j))] * n_s,
        out_shape=[jax.ShapeDtypeStruct((bn * t, width), F32)] * n_s,
        scratch_shapes=[pltpu.VMEM((tmm, d), BF16)],
        compiler_params=_cparams(2), name=name,
    )(x3, shift, scale, *[w for w, _, _ in sections], *gains)


def _mm_res_kernel(*refs, n_in):
    a_refs, w_refs = refs[:n_in], refs[n_in:2 * n_in]
    x_ref, gate_ref, o_ref = refs[2 * n_in:]
    acc = None
    for a_ref, w_ref in zip(a_refs, w_refs):
        part = jnp.dot(a_ref[...].astype(BF16), w_ref[...], preferred_element_type=F32)
        acc = part if acc is None else acc + part
    x = x_ref[...]
    o_ref[...] = x + gate_ref[...] * acc.reshape(x.shape)


def _mm_res(a_list, w_list, x3, gate, *, tm=1024, tn=1024, name="mm_res"):
    bn, t, d = x3.shape
    gb, rows, n_i, tpb = _row_tiling(bn, t, tm)
    tmm = gb * rows
    tn = _col_tile(d, tn)
    n_in = len(a_list)
    in_specs = ([pl.BlockSpec((tmm, a.shape[1]), lambda i, j: (i, 0)) for a in a_list]
                + [pl.BlockSpec((w.shape[0], tn), lambda i, j: (0, j)) for w in w_list]
                + [pl.BlockSpec((gb, rows, tn), lambda i, j: (i // tpb, i % tpb, j)),
                   pl.BlockSpec((gb, 1, tn), lambda i, j: (i // tpb, 0, j))])
    return pl.pallas_call(
        functools.partial(_mm_res_kernel, n_in=n_in),
        grid=(n_i, d // tn),
        in_specs=in_specs,
        out_specs=pl.BlockSpec((gb, rows, tn), lambda i, j: (i // tpb, i % tpb, j)),
        out_shape=jax.ShapeDtypeStruct((bn, t, d), F32),
        compiler_params=_cparams(2), name=name,
    )(*a_list, *w_list, x3, gate)


def _mlstm_kernel(q_ref, k_ref, v_ref, zo_ref, g_ref, bg_ref, gn_ref, c0_ref, n0_ref, m0_ref,
                  h_ref, co_ref, no_ref, mo_ref, c_s, n_s, m_s, *, tb, chunk):
    ck = pl.program_id(1)
    L = chunk
    n_heads, dk, dv = c_s.shape

    @pl.when(ck == 0)
    def _():
        c_s[...] = c0_ref[0]
        n_s[...] = n0_ref[0]
        m_s[...] = m0_ref[0]

    def pad_rows(x):
        if tb == L:
            return x
        return jnp.concatenate([x, jnp.zeros((L - tb, x.shape[1]), x.dtype)], axis=0)

    gates = pad_rows(g_ref[0] + bg_ref[...])
    lane = lax.broadcasted_iota(jnp.int32, (L, LANES), 1)
    li_all = jnp.where(lane < n_heads, gates, 0.0)
    lf_all = jnp.where((lane >= n_heads) & (lane < 2 * n_heads),
                       -(jnp.maximum(-gates, 0.0) + jnp.log1p(jnp.exp(-jnp.abs(gates)))), 0.0)
    if tb != L:
        real = lax.broadcasted_iota(jnp.int32, (L, LANES), 0) < tb
        li_all = jnp.where(real | (lane >= n_heads), li_all, NEG)
        lf_all = jnp.where(real, lf_all, 0.0)
    ri = lax.broadcasted_iota(jnp.int32, (L, L), 0)
    ci = lax.broadcasted_iota(jnp.int32, (L, L), 1)
    tri = ri >= ci
    y = li_all + jnp.dot(tri.astype(F32), lf_all, precision=HIGHEST, preferred_element_type=F32)
    er = lax.broadcasted_iota(jnp.int32, (SUBLANES, LANES), 0)
    el = lax.broadcasted_iota(jnp.int32, (SUBLANES, LANES), 1)
    e = jnp.where(er < n_heads, jnp.where(el == er + n_heads, 1.0, 0.0) - jnp.where(el == er, 1.0, 0.0), 0.0)
    rt = lax.dot_general(e, y, _NT, precision=HIGHEST, preferred_element_type=F32)

    for hd in range(n_heads):
        li = y[:, hd:hd + 1]
        b = y[:, n_heads + hd:n_heads + hd + 1]
        r_row = rt[hd:hd + 1, :]
        q = pad_rows(q_ref[0, :, hd * dk:(hd + 1) * dk]) * (dk ** -0.5)
        k = pad_rows(k_ref[0, :, hd * dk:(hd + 1) * dk])
        v = pad_rows(v_ref[0, :, hd * dv:(hd + 1) * dv])
        a = b + m_s[hd][:, :1]
        dmat = jnp.where(tri, b - r_row, NEG)
        mt = jnp.maximum(a, jnp.max(dmat, axis=-1, keepdims=True))
        w_inter = jnp.exp(a - mt)
        qb, kb, vb = q.astype(BF16), k.astype(BF16), v.astype(BF16)
        s = lax.dot_general(qb, kb, _NT, preferred_element_type=F32) * jnp.exp(dmat - mt)
        c_old = c_s[hd]
        n_old = n_s[hd]
        num = (w_inter * jnp.dot(qb, c_old.astype(BF16), preferred_element_type=F32)
               + jnp.dot(s.astype(BF16), vb, preferred_element_type=F32))
        den = w_inter * jnp.sum(q * n_old, axis=-1, keepdims=True) + jnp.sum(s, axis=-1, keepdims=True)
        hh = (num / jnp.maximum(jnp.abs(den), jnp.exp(-mt)))[:tb]
        hn = hh * lax.rsqrt(jnp.mean(hh * hh, axis=-1, keepdims=True) + EPS)
        h_ref[0, :, hd * dv:(hd + 1) * dv] = (hn * gn_ref[:, hd * dv:(hd + 1) * dv]
                                              * _sigmoid(zo_ref[0, :, hd * dv:(hd + 1) * dv])).astype(h_ref.dtype)

        m_l = mt[L - 1:L]
        g_inter = jnp.exp(a[L - 1:L] - m_l)
        g_s = jnp.exp(b[L - 1:L] - b + li - m_l)
        c_s[hd] = g_inter * c_old + lax.dot_general(kb, (g_s * v).astype(BF16), _TN, preferred_element_type=F32)
        n_s[hd] = g_inter * n_old + jnp.sum(g_s * k, axis=0, keepdims=True)
        m_s[hd] = jnp.broadcast_to(m_l, (1, LANES))

    @pl.when(ck == pl.num_programs(1) - 1)
    def _():
        co_ref[0] = c_s[...]
        no_ref[0] = n_s[...]
        mo_ref[0] = m_s[...]


def _mlstm(qk3, vo3, g3, bg, gn, c0, n0, m0, *, out_dtype, name):
    bn, t, _ = qk3.shape
    _, n_heads, dk, dv = c0.shape
    assert dk == dv and dk % LANES == 0 and 2 * n_heads <= LANES and n_heads <= SUBLANES
    if t % MLSTM_CHUNK == 0:
        tb = chunk = MLSTM_CHUNK
    else:
        assert t <= MLSTM_MIN_CHUNK
        tb, chunk = t, MLSTM_MIN_CHUNK
    n_c = t // tb
    hw = n_heads * dk
    col = lambda sec: (lambda b, c: (b, c, sec))
    state = lambda b, c: (b, 0, 0, 0)
    h, c, n, m = pl.pallas_call(
        functools.partial(_mlstm_kernel, tb=tb, chunk=chunk),
        grid=(bn, n_c),
        in_specs=[pl.BlockSpec((1, tb, hw), col(0)), pl.BlockSpec((1, tb, hw), col(1)),
                  pl.BlockSpec((1, tb, hw), col(0)), pl.BlockSpec((1, tb, hw), col(1)),
                  pl.BlockSpec((1, tb, LANES), lambda b, c: (b, c, 0)),
                  pl.BlockSpec((1, LANES), lambda b, c: (0, 0)),
                  pl.BlockSpec((1, hw), lambda b, c: (0, 0)),
                  pl.BlockSpec((1, n_heads, dk, dv), state), pl.BlockSpec((1, n_heads, 1, dk), state),
                  pl.BlockSpec((1, n_heads, 1, LANES), state)],
        out_specs=[pl.BlockSpec((1, tb, hw), lambda b, c: (b, c, 0)),
                   pl.BlockSpec((1, n_heads, dk, dv), state), pl.BlockSpec((1, n_heads, 1, dk), state),
                   pl.BlockSpec((1, n_heads, 1, LANES), state)],
        out_shape=[jax.ShapeDtypeStruct((bn, t, hw), out_dtype),
                   jax.ShapeDtypeStruct((bn, n_heads, dk, dv), F32),
                   jax.ShapeDtypeStruct((bn, n_heads, 1, dk), F32),
                   jax.ShapeDtypeStruct((bn, n_heads, 1, LANES), F32)],
        scratch_shapes=[pltpu.VMEM((n_heads, dk, dv), F32), pltpu.VMEM((n_heads, 1, dk), F32),
                        pltpu.VMEM((n_heads, 1, LANES), F32)],
        compiler_params=_cparams(2), name=name,
    )(qk3, qk3, vo3, vo3, g3, bg, gn, c0, n0.reshape(bn, n_heads, 1, dk),
      jnp.broadcast_to(m0[:, :, None, None], (bn, n_heads, 1, LANES)))
    return h, c, n.reshape(bn, n_heads, dk), m[:, :, 0, 0]


def _convmod_kernel(a_ref, g_ref, buf_ref, w_ref, b_ref, lg_ref, lb_ref, y_ref, nb_ref, u_s, *, tt, width, rc):
    t = pl.program_id(1)
    hist = width - 1

    @pl.when(t == 0)
    def _():
        buf = buf_ref[0]
        for r in range(SUBLANES):
            u_s[r, 0:hist - r, :] = buf[r:, :]

    @pl.when(t > 0)
    def _():
        for r in range(SUBLANES):
            u_s[r, 0:CONV_HALO, :] = u_s[r, tt:tt + CONV_HALO, :]

    u = a_ref[0] * _sigmoid(g_ref[0])
    for r in range(SUBLANES):
        u_s[r, hist - r:hist - r + tt, :] = u
    bias = b_ref[...]
    lg = lg_ref[...]
    lb = lb_ref[...]
    n_p = rc // SUBLANES
    for r0 in range(0, tt, rc):
        accs = [jnp.broadcast_to(bias, (SUBLANES, bias.shape[1]))] * n_p
        for j in range(width):
            r = j % SUBLANES
            wj = w_ref[j]
            accs = [acc + wj * u_s[r, r0 + p * SUBLANES + j - r:r0 + (p + 1) * SUBLANES + j - r, :]
                    for p, acc in enumerate(accs)]
        acc = accs[0] if n_p == 1 else jnp.concatenate(accs, axis=0)
        mu = jnp.mean(acc, axis=-1, keepdims=True)
        xc = acc - mu
        var = jnp.mean(xc * xc, axis=-1, keepdims=True)
        y = xc * lax.rsqrt(var + LN_EPS) * lg + lb
        y_ref[0, r0:r0 + rc, :] = (y * _sigmoid(y)).astype(y_ref.dtype)

    @pl.when(t == pl.num_programs(1) - 1)
    def _():
        nb_ref[0] = u_s[0, tt:tt + hist, :]


def _conv_module(z3, col_a, col_g, cbuf, conv_w, conv_b, ln_g, ln_b, *, out_dtype, name):
    bn, t, _ = z3.shape
    width, ch = conv_w.shape
    assert width - 1 <= CONV_HALO
    tt = 128 if t % 128 == 0 else t
    n_t = t // tt
    assert n_t == 1 or tt >= CONV_HALO
    y, nb = pl.pallas_call(
        functools.partial(_convmod_kernel, tt=tt, width=width,
                          rc=4 * SUBLANES if tt % (4 * SUBLANES) == 0 else SUBLANES),
        grid=(bn, n_t),
        in_specs=[pl.BlockSpec((1, tt, ch), lambda b, i: (b, i, col_a)),
                  pl.BlockSpec((1, tt, ch), lambda b, i: (b, i, col_g)),
                  pl.BlockSpec((1, width - 1, ch), lambda b, i: (b, 0, 0)),
                  pl.BlockSpec((width, SUBLANES, ch), lambda b, i: (0, 0, 0)),
                  pl.BlockSpec((1, ch), lambda b, i: (0, 0)),
                  pl.BlockSpec((1, ch), lambda b, i: (0, 0)),
                  pl.BlockSpec((1, ch), lambda b, i: (0, 0))],
        out_specs=[pl.BlockSpec((1, tt, ch), lambda b, i: (b, i, 0)),
                   pl.BlockSpec((1, width - 1, ch), lambda b, i: (b, 0, 0))],
        out_shape=[jax.ShapeDtypeStruct((bn, t, ch), out_dtype),
                   jax.ShapeDtypeStruct((bn, width - 1, ch), F32)],
        scratch_shapes=[pltpu.VMEM((SUBLANES, CONV_HALO + tt, ch), F32)],
        compiler_params=_cparams(2), name=name,
    )(z3, z3, cbuf, jnp.broadcast_to(conv_w[:, None, :], (width, SUBLANES, ch)), conv_b.reshape(1, ch),
      ln_g.reshape(1, ch), ln_b.reshape(1, ch))
    return y, nb


def _ffn_up_kernel(x_ref, sh_ref, sc_ref, wg_ref, wu_ref, buf_ref, cw_ref, cb_ref, act_ref, gl_ref, h_s, carry_s,
                   *, tpb, rows, strip):
    ti = pl.program_id(0) % tpb
    j = pl.program_id(1)

    @pl.when(j == 0)
    def _():
        x = x_ref[...]
        r = lax.rsqrt(jnp.mean(x * x, axis=-1, keepdims=True) + EPS)
        h = (x * r) * (1.0 + sc_ref[...]) + sh_ref[...]
        h_s[...] = h.reshape(h_s.shape).astype(BF16)

    if tpb > 1:
        @pl.when(ti == 0)
        def _():
            carry_s[j] = buf_ref[0]

    h = h_s[...]
    n = h.shape[0]
    gb = n // rows
    tn = wg_ref.shape[1]
    for c0 in range(0, tn, strip):
        cs = slice(c0, c0 + strip)
        g = jnp.dot(h, wg_ref[:, cs], preferred_element_type=F32)
        u = jnp.dot(h, wu_ref[:, cs], preferred_element_type=F32)
        if tpb > 1:
            prev = carry_s[j, :, cs]
            carry_s[j, :, cs] = pltpu.roll(g[n - SUBLANES:], 2, axis=0)
        else:
            prev = buf_ref[:, :, cs].reshape(gb * SUBLANES, strip)
        r8 = lax.broadcasted_iota(jnp.int32, (gb * SUBLANES, strip), 0) % SUBLANES
        prev1 = pltpu.roll(prev, gb * SUBLANES - 1, axis=0)
        gm1 = pltpu.roll(g, 1, axis=0)
        gm2 = pltpu.roll(g, 2, axis=0)
        if rows == SUBLANES:
            gm1 = jnp.where(r8 < 1, prev1, gm1)
            gm2 = jnp.where(r8 < 2, prev, gm2)
            gl_ref[:, :, cs] = g.reshape(gb, SUBLANES, strip)
        else:
            gm1 = jnp.concatenate([jnp.where(r8 < 1, prev1, gm1[:SUBLANES]), gm1[SUBLANES:]], axis=0)
            gm2 = jnp.concatenate([jnp.where(r8 < 2, prev, gm2[:SUBLANES]), gm2[SUBLANES:]], axis=0)
            gl_ref[:, :, cs] = g[n - SUBLANES:].reshape(gb, SUBLANES, strip)
        gc = cw_ref[0:1, cs] * gm2 + cw_ref[1:2, cs] * gm1 + cw_ref[2:3, cs] * g + cb_ref[:, cs]
        act_ref[:, cs] = ((gc * _sigmoid(gc)) * u).astype(act_ref.dtype)


def _ffn_up(x3, shift, scale, w, buf8, conv_w, conv_b, *, tm=1024, tn=512, name="ffn_up"):
    bn, t, d = x3.shape
    dff = w.shape[1] // 2
    gb, rows, n_i, tpb = _row_tiling(bn, t, tm)
    tmm = gb * rows
    tn = _col_tile(dff, tn)
    n_f = dff // tn
    assert conv_w.shape[0] == 3 and (tpb == 1 or gb == 1)
    act, g_tail = pl.pallas_call(
        functools.partial(_ffn_up_kernel, tpb=tpb, rows=rows, strip=min(tn, 256)),
        grid=(n_i, n_f),
        in_specs=[pl.BlockSpec((gb, rows, d), lambda i, j: (i // tpb, i % tpb, 0)),
                  pl.BlockSpec((gb, 1, d), lambda i, j: (i // tpb, 0, 0)),
                  pl.BlockSpec((gb, 1, d), lambda i, j: (i // tpb, 0, 0)),
                  pl.BlockSpec((d, tn), lambda i, j: (0, j)),
                  pl.BlockSpec((d, tn), lambda i, j: (0, j + n_f)),
                  pl.BlockSpec((gb, SUBLANES, tn), lambda i, j: (i // tpb, 0, j)),
                  pl.BlockSpec((3, tn), lambda i, j: (0, j)),
                  pl.BlockSpec((1, tn), lambda i, j: (0, j))],
        out_specs=[pl.BlockSpec((tmm, tn), lambda i, j: (i, j)),
                   pl.BlockSpec((gb, SUBLANES, tn), lambda i, j: (i, 0, j))],
        out_shape=[jax.ShapeDtypeStruct((bn * t, dff), BF16),
                   jax.ShapeDtypeStruct((n_i * gb, SUBLANES, dff), F32)],
        scratch_shapes=[pltpu.VMEM((tmm, d), BF16), pltpu.VMEM((n_f, SUBLANES, tn), F32)],
        compiler_params=_cparams(2), name=name,
    )(x3, shift, scale, w, w, buf8, conv_w, conv_b.reshape(1, dff))
    return act, g_tail.reshape(bn, tpb, SUBLANES, dff)[:, tpb - 1]


def _t5_bucket(dist):
    n = jnp.maximum(dist, 0)
    exact = REL_BUCKETS // 2
    large = exact + (jnp.log(jnp.maximum(n, 1).astype(F32) / exact)
                     / math.log(REL_MAX_DIST / exact) * (REL_BUCKETS - exact)).astype(jnp.int32)
    return jnp.where(n < exact, n, jnp.minimum(large, REL_BUCKETS - 1))


def _bias_lookup(dist, table_col):
    bucket = _t5_bucket(dist)
    bias = jnp.zeros(dist.shape, F32)
    for kk in range(REL_BUCKETS):
        bias = jnp.where(bucket == kk, table_col(kk), bias)
    return bias * LOG2E


def _bias_prompt_kernel(rb_ref, o_ref, *, blk):
    rb = rb_ref[0]
    key = lax.broadcasted_iota(jnp.int32, (blk, blk), 0)
    qry = lax.broadcasted_iota(jnp.int32, (blk, blk), 1)
    col = lambda kk: rb[:, kk:kk + 1]
    own = _bias_lookup(qry - key, col)
    o_ref[0, 0] = jnp.where(qry >= key, own, NEG)
    o_ref[0, 1] = _bias_lookup(blk + qry - key, col)


def _bias_prompt(rel_bias, blk):
    n_b, n_h = rel_bias.shape
    rbt = jnp.pad(rel_bias.T, ((0, 0), (0, LANES - n_b))).reshape(n_h, 1, LANES)
    return pl.pallas_call(
        functools.partial(_bias_prompt_kernel, blk=blk),
        grid=(n_h,),
        in_specs=[pl.BlockSpec((1, 1, LANES), lambda h: (h, 0, 0))],
        out_specs=pl.BlockSpec((1, 2, blk, blk), lambda h: (h, 0, 0, 0)),
        out_shape=jax.ShapeDtypeStruct((n_h, 2, blk, blk), F32),
        compiler_params=_cparams(1), name="bias_prompt",
    )(rbt), rbt


def _bias_sample_kernel(rb_ref, fl_ref, new_ref, *, n_heads, t, page):
    rb = rb_ref[...]
    col = lambda kk: rb[:, kk:kk + 1]
    for ref, base in ((fl_ref.at[0], 2 * page), (fl_ref.at[1], page), (new_ref, 0)):
        shape = ref.shape
        r = lax.broadcasted_iota(jnp.int32, shape, 0)
        c = lax.broadcasted_iota(jnp.int32, shape, 1)
        if ref is new_ref:
            key_head, key_off = c // t, c % t
        else:
            key_head, key_off = c % n_heads, c // n_heads
        dist = base + r % t - key_off
        ok = (r // t == key_head) & (dist >= 0)
        ref[...] = jnp.where(ok, _bias_lookup(dist, col), NEG)


def _bias_sample(rel_bias, t, page):
    n_b, n_h = rel_bias.shape
    rows = t * n_h
    rb = jnp.pad(jnp.repeat(rel_bias.T, t, axis=0), ((0, 0), (0, LANES - n_b)))
    return pl.pallas_call(
        functools.partial(_bias_sample_kernel, n_heads=n_h, t=t, page=page),
        out_shape=[jax.ShapeDtypeStruct((2, rows, page * n_h), F32),
                   jax.ShapeDtypeStruct((rows, rows), F32)],
        compiler_params=pltpu.CompilerParams(vmem_limit_bytes=VMEM_LIMIT_BYTES), name="bias_sample",
    )(rb)


def _attn_prompt_kernel(q_ref, k_ref, v_ref, bias_ref, rb_ref, o_ref, km_s, kb_s, vt_s, s_s, *, blk, n_blk, scale):
    km_s[...] = jnp.zeros_like(km_s)
    for j in range(n_blk):
        kj = k_ref[0, j * blk:(j + 1) * blk, :]
        km_s[j:j + 1, :] = jnp.mean(kj, axis=0, keepdims=True)
        kb_s[j * blk:(j + 1) * blk, :] = kj.astype(BF16)
        vt_s[:, j * blk:(j + 1) * blk] = v_ref[0, j * blk:(j + 1) * blk, :].T.astype(BF16)

    far = rb_ref[0][:, REL_BUCKETS - 1:REL_BUCKETS] * LOG2E
    km = km_s[...]
    row = lax.broadcasted_iota(jnp.int32, (km.shape[0], blk), 0)
    tile = 0
    for c in range(n_blk):
        qf = q_ref[0, c * blk:(c + 1) * blk, :]
        qb = qf.astype(BF16)
        add = [0.0] * c
        if c > MOBA_TOPK:
            sc = lax.dot_general(km, qf, _NT, precision=HIGHEST, preferred_element_type=F32)
            sc = jnp.where(row < c, sc, -jnp.inf)
            for j in range(c):
                sj = sc[j:j + 1, :]
                ahead = (sc > sj) | ((sc == sj) & (row < j))
                rank = jnp.sum(jnp.where(ahead, 1.0, 0.0), axis=0, keepdims=True)
                add[j] = jnp.where(rank < MOBA_TOPK, 0.0, NEG)
        m = None
        for j in range(c + 1):
            st = lax.dot_general(kb_s[j * blk:(j + 1) * blk, :], qb, _NT, preferred_element_type=F32) * (scale * LOG2E)
            if j == c:
                st = st + bias_ref[0, 0]
            elif j == c - 1:
                st = st + (bias_ref[0, 1] + add[j])
            else:
                st = st + (add[j] + far)
            s_s[tile + j] = st
            mj = jnp.max(st, axis=0, keepdims=True)
            m = mj if m is None else jnp.maximum(m, mj)
        l = jnp.zeros_like(m)
        acc = jnp.zeros((vt_s.shape[0], blk), F32)
        for j in range(c + 1):
            p = jnp.exp2(s_s[tile + j] - m)
            l = l + jnp.sum(p, axis=0, keepdims=True)
            acc = acc + jnp.dot(vt_s[:, j * blk:(j + 1) * blk], p.astype(BF16), preferred_element_type=F32)
        o_ref[0, c * blk:(c + 1) * blk, :] = (acc / l).T.astype(o_ref.dtype)
        tile += c + 1


def _attn_prompt(q3, k3, v3, bias_tiles, rbt, *, out_dtype):
    bn, s, _ = q3.shape
    n_h = bias_tiles.shape[0]
    dh = q3.shape[2] // n_h
    blk = MOBA_BLOCK
    assert s % blk == 0 and dh == LANES and blk >= REL_MAX_DIST
    n_blk = s // blk
    head = lambda b, h: (b, 0, h)
    return pl.pallas_call(
        functools.partial(_attn_prompt_kernel, blk=blk, n_blk=n_blk, scale=dh ** -0.5),
        grid=(bn, n_h),
        in_specs=[pl.BlockSpec((1, s, dh), head), pl.BlockSpec((1, s, dh), head), pl.BlockSpec((1, s, dh), head),
                  pl.BlockSpec((1, 2, blk, blk), lambda b, h: (h, 0, 0, 0)),
                  pl.BlockSpec((1, 1, LANES), lambda b, h: (h, 0, 0))],
        out_specs=pl.BlockSpec((1, s, dh), head),
        out_shape=jax.ShapeDtypeStruct(q3.shape, out_dtype),
        scratch_shapes=[pltpu.VMEM((-(-n_blk // SUBLANES) * SUBLANES, dh), F32), pltpu.VMEM((s, dh), BF16),
                        pltpu.VMEM((dh, s), BF16), pltpu.VMEM((n_blk * (n_blk + 1) // 2, blk, blk), F32)],
        compiler_params=_cparams(2), name="moba_prompt",
    )(q3, k3, v3, bias_tiles, rbt)


def _attn_sample_kernel(pt_ref, q_ref, kn_ref, vn_ref, *refs, n_heads, ppb, pps, scale):
    del pt_ref
    kp_refs, vp_refs = refs[:pps], refs[pps:2 * pps]
    bias_ref, new_ref, o_ref, ks_s, m_s, l_s, o_s = refs[2 * pps:]
    g = pl.program_id(1)
    last = pl.num_programs(1) - 1
    n_tok = q_ref.shape[1]
    dh = q_ref.shape[2] // n_heads

    def head_major(ref):
        return jnp.concatenate([ref[0, :, h * dh:(h + 1) * dh] for h in range(n_heads)], axis=0)

    qf = head_major(q_ref)
    qb = qf.astype(BF16)
    rows = n_heads * n_tok

    def partial_softmax(kk, vv, bias):
        s = lax.dot_general(qb, kk.astype(BF16), _NT, preferred_element_type=F32) * (scale * LOG2E) + bias
        m = jnp.max(s, axis=-1, keepdims=True)
        e = jnp.exp2(s - m)
        return m, jnp.sum(e, axis=-1, keepdims=True), jnp.dot(e.astype(BF16), vv.astype(BF16),
                                                              preferred_element_type=F32)

    for r in range(pps):
        pg = g * pps + r
        kp = kp_refs[r][0]
        ks_s[pg] = jnp.sum(kp.reshape(kp.shape[0] // n_heads, n_heads, dh), axis=0)
        bias = bias_ref[jnp.where(g == last, 1, 0)] if r == pps - 1 else bias_ref[0]
        m, l, o = partial_softmax(kp, vp_refs[r][0], bias)
        m_s[pg] = m
        l_s[pg] = l
        o_s[pg] = o

    @pl.when(g == last)
    def _():
        n_pg = m_s.shape[0]
        n_blk = n_pg // ppb
        inv = 1.0 / (ppb * (kp_refs[0].shape[1] // n_heads))
        scs = []
        for j in range(n_blk):
            km = ks_s[j * ppb]
            for pp in range(1, ppb):
                km = km + ks_s[j * ppb + pp]
            km = jnp.broadcast_to((km * inv)[:, None, :], (n_heads, n_tok, dh)).reshape(rows, dh)
            scs.append(jnp.sum(qf * km, axis=-1, keepdims=True))
        sel = []
        for j in range(n_blk):
            rank = jnp.zeros((rows, 1), F32)
            for jj in range(n_blk):
                if jj != j:
                    ahead = (scs[jj] >= scs[j]) if jj < j else (scs[jj] > scs[j])
                    rank = rank + jnp.where(ahead, 1.0, 0.0)
            sel.append(rank < MOBA_TOPK)
        m_n, l_n, o_n = partial_softmax(head_major(kn_ref), head_major(vn_ref), new_ref[...])
        m_all = m_n
        for pg in range(n_pg):
            m_all = jnp.maximum(m_all, jnp.where(sel[pg // ppb], m_s[pg], NEG))
        w_n = jnp.exp2(m_n - m_all)
        l_all = w_n * l_n
        o_all = w_n * o_n
        for pg in range(n_pg):
            w = jnp.where(sel[pg // ppb], jnp.exp2(m_s[pg] - m_all), 0.0)
            l_all = l_all + w * l_s[pg]
            o_all = o_all + w * o_s[pg]
        o = (o_all / l_all).astype(o_ref.dtype)
        for h in range(n_heads):
            o_ref[0, :, h * dh:(h + 1) * dh] = o[h * n_tok:(h + 1) * n_tok]


def _attn_sample(q, k, v, cache_k, cache_v, page_table, far_last, new, *, out_dtype, pps=4):
    db, t, hd = q.shape
    n_pool, page, n_h, dh = cache_k.shape
    n_pages = page_table.shape[1]
    rows = t * n_h
    assert MOBA_BLOCK % page == 0 and (n_pages * page) % MOBA_BLOCK == 0 and t <= MOBA_BLOCK
    assert page >= REL_MAX_DIST and n_pages * page // MOBA_BLOCK > MOBA_TOPK and dh == LANES
    assert n_pages % pps == 0
    ppb = MOBA_BLOCK // page
    seq = lambda b, g, pt: (b, 0, 0)
    paged = [(lambda b, g, pt, r=r: (pt[b, g * pps + r], 0, 0)) for r in range(pps)]
    page_spec = lambda r: pl.BlockSpec((1, page * n_h, dh), paged[r])
    grid_spec = pltpu.PrefetchScalarGridSpec(
        num_scalar_prefetch=1,
        grid=(db, n_pages // pps),
        in_specs=([pl.BlockSpec((1, t, hd), seq)] * 3 + [page_spec(r) for r in range(pps)] * 2
                  + [pl.BlockSpec(far_last.shape, lambda b, g, pt: (0, 0, 0)),
                     pl.BlockSpec(new.shape, lambda b, g, pt: (0, 0))]),
        out_specs=pl.BlockSpec((1, t, hd), seq),
        scratch_shapes=[pltpu.VMEM((n_pages, n_h, dh), F32), pltpu.VMEM((n_pages, rows, 1), F32),
                        pltpu.VMEM((n_pages, rows, 1), F32), pltpu.VMEM((n_pages, rows, dh), F32)],
    )
    ck = cache_k.reshape(n_pool, page * n_h, dh)
    cv = cache_v.reshape(n_pool, page * n_h, dh)
    return pl.pallas_call(
        functools.partial(_attn_sample_kernel, n_heads=n_h, ppb=ppb, pps=pps, scale=dh ** -0.5),
        grid_spec=grid_spec,
        out_shape=jax.ShapeDtypeStruct((db, t, hd), out_dtype),
        compiler_params=_cparams(2), name="moba_sample",
    )(page_table, q, k, v, *([ck] * pps), *([cv] * pps), far_last, new)


def _trunk(x, mods, mc, mn, mm, cb, fb, wts, attend):
    bn, t, d = x.shape
    depth = len(wts["ffn_up"])
    n_ha = mc.shape[2]
    qkw = n_ha * mc.shape[3]
    aw = n_ha * mc.shape[4]
    bw = wts["conv_w"].shape[2]
    big = t >= 512
    act_dtype = BF16 if big else F32
    out_c, out_n, out_m, out_conv, out_k, out_v, out_ffn = [], [], [], [], [], [], []
    for l in range(depth):
        shift, scale, gate = mods[l][0]
        if l % 2 == 0:
            e = l // 2
            assert qkw == aw == bw
            w_a, w_b = wts["in_even_a"][e], wts["in_even_b"][e]
            z_qk, z_vo, z_ag = (z.reshape(bn, t, 2 * qkw) for z in _nm_matmul(
                x, shift, scale, [(w_a, 0, None), (w_a, 2 * qkw, None), (w_b, 0, None)], 2 * qkw, tn=256,
                name="in_even"))
            g, = _nm_matmul(x, shift, scale, [(wts["in_even_gate"][e], 0, None)], LANES, name="in_even_gates")
            h_a, c_new, n_new, m_new = _mlstm(z_qk, z_vo, g.reshape(bn, t, LANES), wts["b_gates"][e],
                                              wts["mlstm_norm_g"][e], mc[e], mn[e], mm[e], out_dtype=act_dtype,
                                              name="mlstm")
            y_b, cb_new = _conv_module(z_ag, 0, 1, cb[e], wts["conv_w"][e], wts["conv_b"][e],
                                       wts["conv_ln_g"][e], wts["conv_ln_b"][e], out_dtype=act_dtype, name="conv_module")
            x = _mm_res([h_a.reshape(bn * t, aw), y_b.reshape(bn * t, bw)],
                        [wts["out_even_a"][e], wts["out_even_b"][e]], x, gate, name="out_even")
            out_c.append(c_new); out_n.append(n_new); out_m.append(m_new); out_conv.append(cb_new)
        else:
            o = l // 2
            w = wts["in_odd"][o]
            qkv = _nm_matmul(x, shift, scale, [(w, 0, wts["q_norm_g"][o]), (w, d, wts["k_norm_g"][o]), (w, 2 * d, None)],
                             d, tn=256, name="in_odd")
            q, k, v = (a.reshape(bn, t, d) for a in qkv)
            a = attend(o, q, k, v, act_dtype)
            x = _mm_res([a.reshape(bn * t, d)], [wts["out_odd"][o]], x, gate, name="out_odd")
            out_k.append(k); out_v.append(v)
        shift, scale, gate = mods[l][1]
        hist = fb[l].shape[1]
        assert hist <= min(t, SUBLANES)
        buf8 = jnp.pad(fb[l], ((0, 0), (0, SUBLANES - hist), (0, 0)))
        act, g_last = _ffn_up(x, shift, scale, wts["ffn_up"][l], buf8, wts["ffn_conv_w"][l], wts["ffn_conv_b"][l])
        x = _mm_res([act], [wts["ffn_down"][l]], x, gate, tn=512, name="ffn_down")
        out_ffn.append(g_last[:, SUBLANES - hist:])
    return (x, jnp.stack(out_c), jnp.stack(out_n), jnp.stack(out_m), jnp.stack(out_conv),
            jnp.stack(out_k), jnp.stack(out_v), jnp.stack(out_ffn))


def kernel(x_prompt, x_sample, state_mlstm_c, state_mlstm_n, state_mlstm_m, state_conv, cache_k, cache_v, page_table, state_ffn, c_prompt, c_sample, ada_w, ada_b, w_in_even, b_gates, mlstm_norm_g, conv_w, conv_b, conv_ln_g, conv_ln_b, w_out_even, w_in_odd, q_norm_g, k_norm_g, rel_bias, w_out_odd, ffn_w_up, ffn_conv_w, ffn_conv_b, ffn_w_down):
    bsz, seq, d = x_prompt.shape
    db, dt, _ = x_sample.shape
    depth = ada_w.shape[0]
    n_even, _, n_ha, dk, dv = state_mlstm_c.shape
    n_hc, dh = cache_k.shape[3], cache_k.shape[4]
    qkw, aw, bw = n_ha * dk, n_ha * dv, conv_w.shape[2]
    n_gate = 2 * n_ha
    o_gate = 2 * qkw + 2 * aw
    dt_x = x_prompt.dtype

    def per_layer(w, cast=lambda a: a.astype(BF16)):
        return [cast(w[i]) for i in range(w.shape[0])]

    wts = {
        "in_even_a": per_layer(w_in_even, lambda a: a[:, :o_gate].astype(BF16)),
        "in_even_b": per_layer(w_in_even, lambda a: a[:, o_gate + n_gate:].astype(BF16)),
        "in_even_gate": per_layer(w_in_even, lambda a: jnp.pad(a[:, o_gate:o_gate + n_gate],
                                                               ((0, 0), (0, LANES - n_gate))).astype(BF16)),
        "b_gates": jnp.pad(b_gates, ((0, 0), (0, LANES - n_gate))).reshape(n_even, 1, LANES),
        "mlstm_norm_g": mlstm_norm_g.reshape(n_even, 1, aw),
        "conv_w": conv_w, "conv_b": conv_b, "conv_ln_g": conv_ln_g, "conv_ln_b": conv_ln_b,
        "out_even_a": per_layer(w_out_even, lambda a: a[:aw].astype(BF16)),
        "out_even_b": per_layer(w_out_even, lambda a: a[aw:].astype(BF16)),
        "in_odd": per_layer(w_in_odd), "q_norm_g": q_norm_g, "k_norm_g": k_norm_g,
        "out_odd": per_layer(w_out_odd),
        "ffn_up": per_layer(ffn_w_up), "ffn_conv_w": ffn_conv_w, "ffn_conv_b": ffn_conv_b,
        "ffn_down": per_layer(ffn_w_down),
    }

    mod = _adaln_mods(jnp.concatenate([c_prompt, c_sample], axis=0), ada_w, ada_b)

    def split_mods(rows):
        m = mod[:, rows].reshape(depth, 2, -1, 3, 1, d)
        return [[tuple(m[l, j, :, s] for s in range(3)) for j in range(2)] for l in range(depth)]

    mods_p = split_mods(slice(0, bsz))
    mods_s = split_mods(slice(bsz, bsz + db))

    bias_tiles, rbt = _bias_prompt(rel_bias, MOBA_BLOCK)
    far_last, new = _bias_sample(rel_bias, dt, cache_k.shape[2])

    def prompt_attend(o, q, k, v, out_dtype):
        return _attn_prompt(q, k, v, bias_tiles, rbt, out_dtype=out_dtype)

    def sample_attend(o, q, k, v, out_dtype):
        return _attn_sample(q, k, v, cache_k[o], cache_v[o], page_table, far_last, new, out_dtype=out_dtype)

    zc = jnp.zeros((n_even, bsz, n_ha, dk, dv), dt_x)
    zn = jnp.zeros((n_even, bsz, n_ha, dk), dt_x)
    zm = jnp.zeros((n_even, bsz, n_ha), dt_x)
    zconv = jnp.zeros((n_even, bsz) + state_conv.shape[2:], dt_x)
    zffn = jnp.zeros((depth, bsz) + state_ffn.shape[2:], dt_x)

    y_p, p_c, p_n, p_m, p_conv, p_k, p_v, p_ffn = _trunk(
        x_prompt, mods_p, zc, zn, zm, zconv, zffn, wts, prompt_attend)
    y_s, s_c, s_n, s_m, s_conv, s_k, s_v, s_ffn = _trunk(
        x_sample, mods_s, state_mlstm_c, state_mlstm_n, state_mlstm_m, state_conv, state_ffn, wts, sample_attend)

    def heads(a):
        return a.reshape(a.shape[:3] + (n_hc, dh))

    return (y_p, y_s, p_c, p_n, p_m, p_conv, heads(p_k), heads(p_v), p_ffn,
            s_c, s_n, s_m, s_conv, heads(s_k), heads(s_v), s_ffn)
```

```python
import functools
import math

import jax
import jax.numpy as jnp
from jax import lax
from jax.experimental import pallas as pl
from jax.experimental.pallas import tpu as pltpu

F32 = jnp.float32
BF16 = jnp.bfloat16
HIGHEST = lax.Precision.HIGHEST

LANES = 128
SUBLANES = 8
VMEM_LIMIT_BYTES = 56 * 1024 * 1024

EPS = 1e-6
LN_EPS = 1e-5
MLSTM_CHUNK = 256
MLSTM_MIN_CHUNK = 32
MOBA_BLOCK = 256
MOBA_TOPK = 3
REL_BUCKETS = 32
REL_MAX_DIST = 128
NEG = -1e30
LOG2E = math.log2(math.e)
CONV_HALO = 32

_NT = (((1,), (1,)), ((), ()))
_TN = (((0,), (0,)), ((), ()))


def _cparams(n_axes):
    return pltpu.CompilerParams(dimension_semantics=("arbitrary",) * n_axes, vmem_limit_bytes=VMEM_LIMIT_BYTES)


def _sigmoid(x):
    return jax.nn.sigmoid(x)


def _col_tile(n, pref):
    tile = min(pref, n)
    while n % tile:
        tile -= LANES
    return tile


def _row_tiling(bn, t, tm):
    if t >= tm:
        assert t % tm == 0
        tpb = t // tm
        return 1, tm, bn * tpb, tpb
    gb = min(bn, tm // t)
    assert bn % gb == 0 and t % SUBLANES == 0
    return gb, t, bn // gb, 1


def _mod_kernel(c_ref, w_ref, b_ref, o_ref):
    c = c_ref[...]
    s = (c * _sigmoid(c)).astype(BF16)
    o_ref[...] = jnp.dot(s, w_ref[...].astype(BF16), preferred_element_type=F32) + b_ref[...]


def _adaln_mods(c_all, ada_w, ada_b, tn=512):
    depth, two, d, n3 = ada_w.shape
    nlj = depth * two
    rows = c_all.shape[0]
    tn = _col_tile(n3, tn)
    return pl.pallas_call(
        _mod_kernel,
        grid=(nlj, n3 // tn),
        in_specs=[pl.BlockSpec((rows, d), lambda l, j: (0, 0)),
                  pl.BlockSpec((None, d, tn), lambda l, j: (l, 0, j)),
                  pl.BlockSpec((None, 1, tn), lambda l, j: (l, 0, j))],
        out_specs=pl.BlockSpec((None, rows, tn), lambda l, j: (l, 0, j)),
        out_shape=jax.ShapeDtypeStruct((nlj, rows, n3), F32),
        compiler_params=_cparams(2), name="adaln_mods",
    )(c_all, ada_w.reshape(nlj, d, n3), ada_b.reshape(nlj, 1, n3))


def _nm_matmul_kernel(x_ref, sh_ref, sc_ref, *rest, normed):
    n_s, n_g = len(normed), sum(normed)
    w_refs, g_refs, o_refs, h_ref = rest[:n_s], rest[n_s:n_s + n_g], rest[n_s + n_g:-1], rest[-1]

    @pl.when(pl.program_id(1) == 0)
    def _():
        x = x_ref[...]
        r = lax.rsqrt(jnp.mean(x * x, axis=-1, keepdims=True) + EPS)
        h = (x * r) * (1.0 + sc_ref[...]) + sh_ref[...]
        h_ref[...] = h.reshape(h_ref.shape).astype(BF16)

    h = h_ref[...]
    g_iter = iter(g_refs)
    for w_ref, o_ref, has_g in zip(w_refs, o_refs, normed):
        acc = jnp.dot(h, w_ref[...], preferred_element_type=F32)
        if not has_g:
            o_ref[...] = acc.astype(o_ref.dtype)
            continue
        g = next(g_iter)[...]
        for s in range(acc.shape[1] // LANES):
            y = acc[:, s * LANES:(s + 1) * LANES]
            y = y * lax.rsqrt(jnp.mean(y * y, axis=-1, keepdims=True) + EPS) * g
            o_ref[:, s * LANES:(s + 1) * LANES] = y.astype(o_ref.dtype)


def _nm_matmul(x3, shift, scale, sections, width, *, tm=1024, tn=1024, name="nm_matmul"):
    bn, t, d = x3.shape
    gb, rows, n_i, tpb = _row_tiling(bn, t, tm)
    tmm = gb * rows
    n_s = len(sections)
    tn = _col_tile(math.gcd(width, *[c0 for _, _, c0, _ in sections if c0]), tn)
    n_j = width // tn
    gains = [g.reshape(1, LANES) for _, _, _, g in sections if g is not None]
    in_specs = ([pl.BlockSpec((gb, rows, d), lambda i, j: (i // tpb, i % tpb, 0)),
                 pl.BlockSpec((gb, 1, d), lambda i, j: (i // tpb, 0, 0)),
                 pl.BlockSpec((gb, 1, d), lambda i, j: (i // tpb, 0, 0))]
                + [pl.BlockSpec((None, d, tn), lambda i, j, l=l, j0=c0 // tn: (l, 0, j + j0)) for _, l, c0, _ in sections]
                + [pl.BlockSpec((1, LANES), lambda i, j: (0, 0))] * len(gains))
    return pl.pallas_call(
        functools.partial(_nm_matmul_kernel, normed=[g is not None for _, _, _, g in sections]),
        grid=(n_i, n_j),
        in_specs=in_specs,
        out_specs=[pl.BlockSpec((tmm, tn), lambda i, j: (i, j))] * n_s,
        out_shape=[jax.ShapeDtypeStruct((bn * t, width), F32)] * n_s,
        scratch_shapes=[pltpu.VMEM((tmm, d), BF16)],
        compiler_params=_cparams(2), name=name,
    )(x3, shift, scale, *[w for w, _, _, _ in sections], *gains)


def _mm_res_kernel(*refs, n_in):
    a_refs, w_refs = refs[:n_in], refs[n_in:2 * n_in]
    x_ref, gate_ref, o_ref = refs[2 * n_in:]
    acc = None
    for a_ref, w_ref in zip(a_refs, w_refs):
        part = jnp.dot(a_ref[...].astype(BF16), w_ref[...], preferred_element_type=F32)
        acc = part if acc is None else acc + part
    x = x_ref[...]
    o_ref[...] = x + gate_ref[...] * acc.reshape(x.shape)


def _mm_res(a_list, w_list, x3, gate, *, tm=1024, tn=1024, name="mm_res"):
    bn, t, d = x3.shape
    gb, rows, n_i, tpb = _row_tiling(bn, t, tm)
    tmm = gb * rows
    tn = _col_tile(d, tn)
    n_in = len(a_list)
    in_specs = ([pl.BlockSpec((tmm, a.shape[1]), lambda i, j: (i, 0)) for a in a_list]
                + [pl.BlockSpec((None, a.shape[1], tn), lambda i, j, l=l, r=r0 // a.shape[1]: (l, r, j))
                   for a, (_, l, r0) in zip(a_list, w_list)]
                + [pl.BlockSpec((gb, rows, tn), lambda i, j: (i // tpb, i % tpb, j)),
                   pl.BlockSpec((gb, 1, tn), lambda i, j: (i // tpb, 0, j))])
    return pl.pallas_call(
        functools.partial(_mm_res_kernel, n_in=n_in),
        grid=(n_i, d // tn),
        in_specs=in_specs,
        out_specs=pl.BlockSpec((gb, rows, tn), lambda i, j: (i // tpb, i % tpb, j)),
        out_shape=jax.ShapeDtypeStruct((bn, t, d), F32),
        compiler_params=_cparams(2), name=name,
    )(*a_list, *[w for w, _, _ in w_list], x3, gate)


def _mlstm_kernel(q_ref, k_ref, v_ref, zo_ref, g_ref, bg_ref, gn_ref, c0_ref, n0_ref, m0_ref,
                  h_ref, co_ref, no_ref, mo_ref, c_s, n_s, m_s, *, tb, chunk):
    ck = pl.program_id(1)
    L = chunk
    n_heads, dk, dv = c_s.shape

    @pl.when(ck == 0)
    def _():
        c_s[...] = c0_ref[0]
        n_s[...] = n0_ref[0]
        m_s[...] = m0_ref[0]

    def pad_rows(x):
        if tb == L:
            return x
        return jnp.concatenate([x, jnp.zeros((L - tb, x.shape[1]), x.dtype)], axis=0)

    gates = pad_rows(g_ref[0] + bg_ref[...])
    lane = lax.broadcasted_iota(jnp.int32, (L, LANES), 1)
    li_all = jnp.where(lane < n_heads, gates, 0.0)
    lf_all = jnp.where((lane >= n_heads) & (lane < 2 * n_heads),
                       -(jnp.maximum(-gates, 0.0) + jnp.log1p(jnp.exp(-jnp.abs(gates)))), 0.0)
    if tb != L:
        real = lax.broadcasted_iota(jnp.int32, (L, LANES), 0) < tb
        li_all = jnp.where(real | (lane >= n_heads), li_all, NEG)
        lf_all = jnp.where(real, lf_all, 0.0)
    ri = lax.broadcasted_iota(jnp.int32, (L, L), 0)
    ci = lax.broadcasted_iota(jnp.int32, (L, L), 1)
    tri = ri >= ci
    y = li_all + jnp.dot(tri.astype(F32), lf_all, precision=HIGHEST, preferred_element_type=F32)
    er = lax.broadcasted_iota(jnp.int32, (SUBLANES, LANES), 0)
    el = lax.broadcasted_iota(jnp.int32, (SUBLANES, LANES), 1)
    e = jnp.where(er < n_heads, jnp.where(el == er + n_heads, 1.0, 0.0) - jnp.where(el == er, 1.0, 0.0), 0.0)
    rt = lax.dot_general(e, y, _NT, precision=HIGHEST, preferred_element_type=F32)

    for hd in range(n_heads):
        li = y[:, hd:hd + 1]
        b = y[:, n_heads + hd:n_heads + hd + 1]
        r_row = rt[hd:hd + 1, :]
        q = pad_rows(q_ref[0, :, hd * dk:(hd + 1) * dk]) * (dk ** -0.5)
        k = pad_rows(k_ref[0, :, hd * dk:(hd + 1) * dk])
        v = pad_rows(v_ref[0, :, hd * dv:(hd + 1) * dv])
        a = b + m_s[hd][:, :1]
        dmat = jnp.where(tri, b - r_row, NEG)
        mt = jnp.maximum(a, jnp.max(dmat, axis=-1, keepdims=True))
        w_inter = jnp.exp(a - mt)
        qb, kb, vb = q.astype(BF16), k.astype(BF16), v.astype(BF16)
        s = lax.dot_general(qb, kb, _NT, preferred_element_type=F32) * jnp.exp(dmat - mt)
        c_old = c_s[hd]
        n_old = n_s[hd]
        num = (w_inter * jnp.dot(qb, c_old.astype(BF16), preferred_element_type=F32)
               + jnp.dot(s.astype(BF16), vb, preferred_element_type=F32))
        den = w_inter * jnp.sum(q * n_old, axis=-1, keepdims=True) + jnp.sum(s, axis=-1, keepdims=True)
        hh = (num / jnp.maximum(jnp.abs(den), jnp.exp(-mt)))[:tb]
        hn = hh * lax.rsqrt(jnp.mean(hh * hh, axis=-1, keepdims=True) + EPS)
        h_ref[0, :, hd * dv:(hd + 1) * dv] = (hn * gn_ref[:, hd * dv:(hd + 1) * dv]
                                              * _sigmoid(zo_ref[0, :, hd * dv:(hd + 1) * dv])).astype(h_ref.dtype)

        m_l = mt[L - 1:L]
        g_inter = jnp.exp(a[L - 1:L] - m_l)
        g_s = jnp.exp(b[L - 1:L] - b + li - m_l)
        c_s[hd] = g_inter * c_old + lax.dot_general(kb, (g_s * v).astype(BF16), _TN, preferred_element_type=F32)
        n_s[hd] = g_inter * n_old + jnp.sum(g_s * k, axis=0, keepdims=True)
        m_s[hd] = jnp.broadcast_to(m_l, (1, LANES))

    @pl.when(ck == pl.num_programs(1) - 1)
    def _():
        co_ref[0] = c_s[...]
        no_ref[0] = n_s[...]
        mo_ref[0] = m_s[...]


def _mlstm(qk3, vo3, g3, bg, gn, c0, n0, m0, *, out_dtype, name):
    bn, t, _ = qk3.shape
    _, n_heads, dk, dv = c0.shape
    assert dk == dv and dk % LANES == 0 and 2 * n_heads <= LANES and n_heads <= SUBLANES
    if t % MLSTM_CHUNK == 0:
        tb = chunk = MLSTM_CHUNK
    else:
        assert t <= MLSTM_MIN_CHUNK
        tb, chunk = t, MLSTM_MIN_CHUNK
    n_c = t // tb
    hw = n_heads * dk
    col = lambda sec: (lambda b, c: (b, c, sec))
    state = lambda b, c: (b, 0, 0, 0)
    h, c, n, m = pl.pallas_call(
        functools.partial(_mlstm_kernel, tb=tb, chunk=chunk),
        grid=(bn, n_c),
        in_specs=[pl.BlockSpec((1, tb, hw), col(0)), pl.BlockSpec((1, tb, hw), col(1)),
                  pl.BlockSpec((1, tb, hw), col(0)), pl.BlockSpec((1, tb, hw), col(1)),
                  pl.BlockSpec((1, tb, LANES), lambda b, c: (b, c, 0)),
                  pl.BlockSpec((1, LANES), lambda b, c: (0, 0)),
                  pl.BlockSpec((1, hw), lambda b, c: (0, 0)),
                  pl.BlockSpec((1, n_heads, dk, dv), state), pl.BlockSpec((1, n_heads, 1, dk), state),
                  pl.BlockSpec((1, n_heads, 1, LANES), state)],
        out_specs=[pl.BlockSpec((1, tb, hw), lambda b, c: (b, c, 0)),
                   pl.BlockSpec((1, n_heads, dk, dv), state), pl.BlockSpec((1, n_heads, 1, dk), state),
                   pl.BlockSpec((1, n_heads, 1, LANES), state)],
        out_shape=[jax.ShapeDtypeStruct((bn, t, hw), out_dtype),
                   jax.ShapeDtypeStruct((bn, n_heads, dk, dv), F32),
                   jax.ShapeDtypeStruct((bn, n_heads, 1, dk), F32),
                   jax.ShapeDtypeStruct((bn, n_heads, 1, LANES), F32)],
        scratch_shapes=[pltpu.VMEM((n_heads, dk, dv), F32), pltpu.VMEM((n_heads, 1, dk), F32),
                        pltpu.VMEM((n_heads, 1, LANES), F32)],
        compiler_params=_cparams(2), name=name,
    )(qk3, qk3, vo3, vo3, g3, bg, gn, c0, n0.reshape(bn, n_heads, 1, dk),
      jnp.broadcast_to(m0[:, :, None, None], (bn, n_heads, 1, LANES)))
    return h, c, n.reshape(bn, n_heads, dk), m[:, :, 0, 0]


def _convmod_kernel(a_ref, g_ref, buf_ref, w_ref, b_ref, lg_ref, lb_ref, y_ref, nb_ref, u_s, *, tt, width, rc):
    t = pl.program_id(1)
    hist = width - 1

    @pl.when(t == 0)
    def _():
        buf = buf_ref[0]
        for r in range(SUBLANES):
            u_s[r, 0:hist - r, :] = buf[r:, :]

    @pl.when(t > 0)
    def _():
        for r in range(SUBLANES):
            u_s[r, 0:CONV_HALO, :] = u_s[r, tt:tt + CONV_HALO, :]

    u = a_ref[0] * _sigmoid(g_ref[0])
    for r in range(SUBLANES):
        u_s[r, hist - r:hist - r + tt, :] = u
    bias = b_ref[...]
    lg = lg_ref[...]
    lb = lb_ref[...]
    n_p = rc // SUBLANES
    for r0 in range(0, tt, rc):
        accs = [jnp.broadcast_to(bias, (SUBLANES, bias.shape[1]))] * n_p
        for j in range(width):
            r = j % SUBLANES
            wj = w_ref[j]
            accs = [acc + wj * u_s[r, r0 + p * SUBLANES + j - r:r0 + (p + 1) * SUBLANES + j - r, :]
                    for p, acc in enumerate(accs)]
        acc = accs[0] if n_p == 1 else jnp.concatenate(accs, axis=0)
        mu = jnp.mean(acc, axis=-1, keepdims=True)
        xc = acc - mu
        var = jnp.mean(xc * xc, axis=-1, keepdims=True)
        y = xc * lax.rsqrt(var + LN_EPS) * lg + lb
        y_ref[0, r0:r0 + rc, :] = (y * _sigmoid(y)).astype(y_ref.dtype)

    @pl.when(t == pl.num_programs(1) - 1)
    def _():
        nb_ref[0] = u_s[0, tt:tt + hist, :]


def _conv_module(z3, col_a, col_g, cbuf, conv_w, conv_b, ln_g, ln_b, *, out_dtype, name):
    bn, t, _ = z3.shape
    width, ch = conv_w.shape
    assert width - 1 <= CONV_HALO
    tt = 128 if t % 128 == 0 else t
    n_t = t // tt
    assert n_t == 1 or tt >= CONV_HALO
    y, nb = pl.pallas_call(
        functools.partial(_convmod_kernel, tt=tt, width=width,
                          rc=4 * SUBLANES if tt % (4 * SUBLANES) == 0 else SUBLANES),
        grid=(bn, n_t),
        in_specs=[pl.BlockSpec((1, tt, ch), lambda b, i: (b, i, col_a)),
                  pl.BlockSpec((1, tt, ch), lambda b, i: (b, i, col_g)),
                  pl.BlockSpec((1, width - 1, ch), lambda b, i: (b, 0, 0)),
                  pl.BlockSpec((width, SUBLANES, ch), lambda b, i: (0, 0, 0)),
                  pl.BlockSpec((1, ch), lambda b, i: (0, 0)),
                  pl.BlockSpec((1, ch), lambda b, i: (0, 0)),
                  pl.BlockSpec((1, ch), lambda b, i: (0, 0))],
        out_specs=[pl.BlockSpec((1, tt, ch), lambda b, i: (b, i, 0)),
                   pl.BlockSpec((1, width - 1, ch), lambda b, i: (b, 0, 0))],
        out_shape=[jax.ShapeDtypeStruct((bn, t, ch), out_dtype),
                   jax.ShapeDtypeStruct((bn, width - 1, ch), F32)],
        scratch_shapes=[pltpu.VMEM((SUBLANES, CONV_HALO + tt, ch), F32)],
        compiler_params=_cparams(2), name=name,
    )(z3, z3, cbuf, jnp.broadcast_to(conv_w[:, None, :], (width, SUBLANES, ch)), conv_b.reshape(1, ch),
      ln_g.reshape(1, ch), ln_b.reshape(1, ch))
    return y, nb


def _ffn_up_kernel(x_ref, sh_ref, sc_ref, wg_ref, wu_ref, buf_ref, cw_ref, cb_ref, act_ref, gl_ref, h_s, carry_s,
                   *, tpb, rows, strip):
    ti = pl.program_id(0) % tpb
    j = pl.program_id(1)

    @pl.when(j == 0)
    def _():
        x = x_ref[...]
        r = lax.rsqrt(jnp.mean(x * x, axis=-1, keepdims=True) + EPS)
        h = (x * r) * (1.0 + sc_ref[...]) + sh_ref[...]
        h_s[...] = h.reshape(h_s.shape).astype(BF16)

    if tpb > 1:
        @pl.when(ti == 0)
        def _():
            carry_s[j] = buf_ref[0]

    h = h_s[...]
    n = h.shape[0]
    gb = n // rows
    tn = wg_ref.shape[1]
    for c0 in range(0, tn, strip):
        cs = slice(c0, c0 + strip)
        g = jnp.dot(h, wg_ref[:, cs], preferred_element_type=F32)
        u = jnp.dot(h, wu_ref[:, cs], preferred_element_type=F32)
        if tpb > 1:
            prev = carry_s[j, :, cs]
            carry_s[j, :, cs] = pltpu.roll(g[n - SUBLANES:], 2, axis=0)
        else:
            prev = buf_ref[:, :, cs].reshape(gb * SUBLANES, strip)
        r8 = lax.broadcasted_iota(jnp.int32, (gb * SUBLANES, strip), 0) % SUBLANES
        prev1 = pltpu.roll(prev, gb * SUBLANES - 1, axis=0)
        gm1 = pltpu.roll(g, 1, axis=0)
        gm2 = pltpu.roll(g, 2, axis=0)
        if rows == SUBLANES:
            gm1 = jnp.where(r8 < 1, prev1, gm1)
            gm2 = jnp.where(r8 < 2, prev, gm2)
            gl_ref[:, :, cs] = g.reshape(gb, SUBLANES, strip)
        else:
            gm1 = jnp.concatenate([jnp.where(r8 < 1, prev1, gm1[:SUBLANES]), gm1[SUBLANES:]], axis=0)
            gm2 = jnp.concatenate([jnp.where(r8 < 2, prev, gm2[:SUBLANES]), gm2[SUBLANES:]], axis=0)
            gl_ref[:, :, cs] = g[n - SUBLANES:].reshape(gb, SUBLANES, strip)
        gc = cw_ref[0:1, cs] * gm2 + cw_ref[1:2, cs] * gm1 + cw_ref[2:3, cs] * g + cb_ref[:, cs]
        act_ref[:, cs] = ((gc * _sigmoid(gc)) * u).astype(act_ref.dtype)


def _ffn_up(x3, shift, scale, w, layer, buf8, conv_w, conv_b, *, tm=1024, tn=512, name="ffn_up"):
    bn, t, d = x3.shape
    dff = w.shape[2] // 2
    gb, rows, n_i, tpb = _row_tiling(bn, t, tm)
    tmm = gb * rows
    tn = _col_tile(dff, tn)
    n_f = dff // tn
    assert conv_w.shape[0] == 3 and (tpb == 1 or gb == 1)
    act, g_tail = pl.pallas_call(
        functools.partial(_ffn_up_kernel, tpb=tpb, rows=rows, strip=min(tn, 256)),
        grid=(n_i, n_f),
        in_specs=[pl.BlockSpec((gb, rows, d), lambda i, j: (i // tpb, i % tpb, 0)),
                  pl.BlockSpec((gb, 1, d), lambda i, j: (i // tpb, 0, 0)),
                  pl.BlockSpec((gb, 1, d), lambda i, j: (i // tpb, 0, 0)),
                  pl.BlockSpec((None, d, tn), lambda i, j: (layer, 0, j)),
                  pl.BlockSpec((None, d, tn), lambda i, j: (layer, 0, j + n_f)),
                  pl.BlockSpec((gb, SUBLANES, tn), lambda i, j: (i // tpb, 0, j)),
                  pl.BlockSpec((3, tn), lambda i, j: (0, j)),
                  pl.BlockSpec((1, tn), lambda i, j: (0, j))],
        out_specs=[pl.BlockSpec((tmm, tn), lambda i, j: (i, j)),
                   pl.BlockSpec((gb, SUBLANES, tn), lambda i, j: (i, 0, j))],
        out_shape=[jax.ShapeDtypeStruct((bn * t, dff), BF16),
                   jax.ShapeDtypeStruct((n_i * gb, SUBLANES, dff), F32)],
        scratch_shapes=[pltpu.VMEM((tmm, d), BF16), pltpu.VMEM((n_f, SUBLANES, tn), F32)],
        compiler_params=_cparams(2), name=name,
    )(x3, shift, scale, w, w, buf8, conv_w, conv_b.reshape(1, dff))
    return act, g_tail.reshape(bn, tpb, SUBLANES, dff)[:, tpb - 1]


def _t5_bucket(dist):
    n = jnp.maximum(dist, 0)
    exact = REL_BUCKETS // 2
    large = exact + (jnp.log(jnp.maximum(n, 1).astype(F32) / exact)
                     / math.log(REL_MAX_DIST / exact) * (REL_BUCKETS - exact)).astype(jnp.int32)
    return jnp.where(n < exact, n, jnp.minimum(large, REL_BUCKETS - 1))


def _bias_lookup(dist, table_col):
    bucket = _t5_bucket(dist)
    bias = jnp.zeros(dist.shape, F32)
    for kk in range(REL_BUCKETS):
        bias = jnp.where(bucket == kk, table_col(kk), bias)
    return bias * LOG2E


def _bias_prompt_kernel(rb_ref, o_ref, *, blk):
    rb = rb_ref[0]
    key = lax.broadcasted_iota(jnp.int32, (blk, blk), 0)
    qry = lax.broadcasted_iota(jnp.int32, (blk, blk), 1)
    col = lambda kk: rb[:, kk:kk + 1]
    own = _bias_lookup(qry - key, col)
    o_ref[0, 0] = jnp.where(qry >= key, own, NEG)
    o_ref[0, 1] = _bias_lookup(blk + qry - key, col)


def _bias_prompt(rel_bias, blk):
    n_b, n_h = rel_bias.shape
    rbt = jnp.pad(rel_bias.T, ((0, 0), (0, LANES - n_b))).reshape(n_h, 1, LANES)
    return pl.pallas_call(
        functools.partial(_bias_prompt_kernel, blk=blk),
        grid=(n_h,),
        in_specs=[pl.BlockSpec((1, 1, LANES), lambda h: (h, 0, 0))],
        out_specs=pl.BlockSpec((1, 2, blk, blk), lambda h: (h, 0, 0, 0)),
        out_shape=jax.ShapeDtypeStruct((n_h, 2, blk, blk), F32),
        compiler_params=_cparams(1), name="bias_prompt",
    )(rbt), rbt


def _bias_sample_kernel(rb_ref, fl_ref, new_ref, *, n_heads, t, page):
    rb = rb_ref[...]
    col = lambda kk: rb[:, kk:kk + 1]
    for ref, base in ((fl_ref.at[0], 2 * page), (fl_ref.at[1], page), (new_ref, 0)):
        shape = ref.shape
        r = lax.broadcasted_iota(jnp.int32, shape, 0)
        c = lax.broadcasted_iota(jnp.int32, shape, 1)
        if ref is new_ref:
            key_head, key_off = c // t, c % t
        else:
            key_head, key_off = c % n_heads, c // n_heads
        dist = base + r % t - key_off
        ok = (r // t == key_head) & (dist >= 0)
        ref[...] = jnp.where(ok, _bias_lookup(dist, col), NEG)


def _bias_sample(rel_bias, t, page):
    n_b, n_h = rel_bias.shape
    rows = t * n_h
    rb = jnp.pad(jnp.repeat(rel_bias.T, t, axis=0), ((0, 0), (0, LANES - n_b)))
    return pl.pallas_call(
        functools.partial(_bias_sample_kernel, n_heads=n_h, t=t, page=page),
        out_shape=[jax.ShapeDtypeStruct((2, rows, page * n_h), F32),
                   jax.ShapeDtypeStruct((rows, rows), F32)],
        compiler_params=pltpu.CompilerParams(vmem_limit_bytes=VMEM_LIMIT_BYTES), name="bias_sample",
    )(rb)


def _attn_prompt_kernel(q_ref, k_ref, v_ref, bias_ref, rb_ref, o_ref, km_s, kb_s, vt_s, s_s, *, blk, n_blk, scale):
    km_s[...] = jnp.zeros_like(km_s)
    for j in range(n_blk):
        kj = k_ref[0, j * blk:(j + 1) * blk, :]
        km_s[j:j + 1, :] = jnp.mean(kj, axis=0, keepdims=True)
        kb_s[j * blk:(j + 1) * blk, :] = kj.astype(BF16)
        vt_s[:, j * blk:(j + 1) * blk] = v_ref[0, j * blk:(j + 1) * blk, :].T.astype(BF16)

    far = rb_ref[0][:, REL_BUCKETS - 1:REL_BUCKETS] * LOG2E
    km = km_s[...]
    row = lax.broadcasted_iota(jnp.int32, (km.shape[0], blk), 0)
    tile = 0
    for c in range(n_blk):
        qf = q_ref[0, c * blk:(c + 1) * blk, :]
        qb = qf.astype(BF16)
        add = [0.0] * c
        if c > MOBA_TOPK:
            sc = lax.dot_general(km, qf, _NT, precision=HIGHEST, preferred_element_type=F32)
            sc = jnp.where(row < c, sc, -jnp.inf)
            for j in range(c):
                sj = sc[j:j + 1, :]
                ahead = (sc > sj) | ((sc == sj) & (row < j))
                rank = jnp.sum(jnp.where(ahead, 1.0, 0.0), axis=0, keepdims=True)
                add[j] = jnp.where(rank < MOBA_TOPK, 0.0, NEG)
        m = None
        for j in range(c + 1):
            st = lax.dot_general(kb_s[j * blk:(j + 1) * blk, :], qb, _NT, preferred_element_type=F32) * (scale * LOG2E)
            if j == c:
                st = st + bias_ref[0, 0]
            elif j == c - 1:
                st = st + (bias_ref[0, 1] + add[j])
            else:
                st = st + (add[j] + far)
            s_s[tile + j] = st
            mj = jnp.max(st, axis=0, keepdims=True)
            m = mj if m is None else jnp.maximum(m, mj)
        l = jnp.zeros_like(m)
        acc = jnp.zeros((vt_s.shape[0], blk), F32)
        for j in range(c + 1):
            p = jnp.exp2(s_s[tile + j] - m)
            l = l + jnp.sum(p, axis=0, keepdims=True)
            acc = acc + jnp.dot(vt_s[:, j * blk:(j + 1) * blk], p.astype(BF16), preferred_element_type=F32)
        o_ref[0, c * blk:(c + 1) * blk, :] = (acc / l).T.astype(o_ref.dtype)
        tile += c + 1


def _attn_prompt(q3, k3, v3, bias_tiles, rbt, *, out_dtype):
    bn, s, _ = q3.shape
    n_h = bias_tiles.shape[0]
    dh = q3.shape[2] // n_h
    blk = MOBA_BLOCK
    assert s % blk == 0 and dh == LANES and blk >= REL_MAX_DIST
    n_blk = s // blk
    head = lambda b, h: (b, 0, h)
    return pl.pallas_call(
        functools.partial(_attn_prompt_kernel, blk=blk, n_blk=n_blk, scale=dh ** -0.5),
        grid=(bn, n_h),
        in_specs=[pl.BlockSpec((1, s, dh), head), pl.BlockSpec((1, s, dh), head), pl.BlockSpec((1, s, dh), head),
                  pl.BlockSpec((1, 2, blk, blk), lambda b, h: (h, 0, 0, 0)),
                  pl.BlockSpec((1, 1, LANES), lambda b, h: (h, 0, 0))],
        out_specs=pl.BlockSpec((1, s, dh), head),
        out_shape=jax.ShapeDtypeStruct(q3.shape, out_dtype),
        scratch_shapes=[pltpu.VMEM((-(-n_blk // SUBLANES) * SUBLANES, dh), F32), pltpu.VMEM((s, dh), BF16),
                        pltpu.VMEM((dh, s), BF16), pltpu.VMEM((n_blk * (n_blk + 1) // 2, blk, blk), F32)],
        compiler_params=_cparams(2), name="moba_prompt",
    )(q3, k3, v3, bias_tiles, rbt)


def _attn_sample_kernel(pt_ref, q_ref, kn_ref, vn_ref, *refs, n_heads, ppb, pps, scale):
    del pt_ref
    kp_refs, vp_refs = refs[:pps], refs[pps:2 * pps]
    bias_ref, new_ref, o_ref, ks_s, m_s, l_s, o_s = refs[2 * pps:]
    g = pl.program_id(1)
    last = pl.num_programs(1) - 1
    n_tok = q_ref.shape[1]
    dh = q_ref.shape[2] // n_heads

    def head_major(ref):
        return jnp.concatenate([ref[0, :, h * dh:(h + 1) * dh] for h in range(n_heads)], axis=0)

    qf = head_major(q_ref)
    qb = qf.astype(BF16)
    rows = n_heads * n_tok

    def partial_softmax(kk, vv, bias):
        s = lax.dot_general(qb, kk.astype(BF16), _NT, preferred_element_type=F32) * (scale * LOG2E) + bias
        m = jnp.max(s, axis=-1, keepdims=True)
        e = jnp.exp2(s - m)
        return m, jnp.sum(e, axis=-1, keepdims=True), jnp.dot(e.astype(BF16), vv.astype(BF16),
                                                              preferred_element_type=F32)

    for r in range(pps):
        pg = g * pps + r
        kp = kp_refs[r][0]
        ks_s[pg] = jnp.sum(kp.reshape(kp.shape[0] // n_heads, n_heads, dh), axis=0)
        bias = bias_ref[jnp.where(g == last, 1, 0)] if r == pps - 1 else bias_ref[0]
        m, l, o = partial_softmax(kp, vp_refs[r][0], bias)
        m_s[pg] = m
        l_s[pg] = l
        o_s[pg] = o

    @pl.when(g == last)
    def _():
        n_pg = m_s.shape[0]
        n_blk = n_pg // ppb
        inv = 1.0 / (ppb * (kp_refs[0].shape[1] // n_heads))
        scs = []
        for j in range(n_blk):
            km = ks_s[j * ppb]
            for pp in range(1, ppb):
                km = km + ks_s[j * ppb + pp]
            km = jnp.broadcast_to((km * inv)[:, None, :], (n_heads, n_tok, dh)).reshape(rows, dh)
            scs.append(jnp.sum(qf * km, axis=-1, keepdims=True))
        sel = []
        for j in range(n_blk):
            rank = jnp.zeros((rows, 1), F32)
            for jj in range(n_blk):
                if jj != j:
                    ahead = (scs[jj] >= scs[j]) if jj < j else (scs[jj] > scs[j])
                    rank = rank + jnp.where(ahead, 1.0, 0.0)
            sel.append(rank < MOBA_TOPK)
        m_n, l_n, o_n = partial_softmax(head_major(kn_ref), head_major(vn_ref), new_ref[...])
        m_all = m_n
        for pg in range(n_pg):
            m_all = jnp.maximum(m_all, jnp.where(sel[pg // ppb], m_s[pg], NEG))
        w_n = jnp.exp2(m_n - m_all)
        l_all = w_n * l_n
        o_all = w_n * o_n
        for pg in range(n_pg):
            w = jnp.where(sel[pg // ppb], jnp.exp2(m_s[pg] - m_all), 0.0)
            l_all = l_all + w * l_s[pg]
            o_all = o_all + w * o_s[pg]
        o = (o_all / l_all).astype(o_ref.dtype)
        for h in range(n_heads):
            o_ref[0, :, h * dh:(h + 1) * dh] = o[h * n_tok:(h + 1) * n_tok]


def _attn_sample(q, k, v, cache_k, cache_v, page_table, far_last, new, *, out_dtype, pps=8):
    db, t, hd = q.shape
    n_pool, page, n_h, dh = cache_k.shape
    n_pages = page_table.shape[1]
    rows = t * n_h
    assert MOBA_BLOCK % page == 0 and (n_pages * page) % MOBA_BLOCK == 0 and t <= MOBA_BLOCK
    assert page >= REL_MAX_DIST and n_pages * page // MOBA_BLOCK > MOBA_TOPK and dh == LANES
    assert n_pages % pps == 0
    ppb = MOBA_BLOCK // page
    seq = lambda b, g, pt: (b, 0, 0)
    paged = [(lambda b, g, pt, r=r: (pt[b, g * pps + r], 0, 0)) for r in range(pps)]
    page_spec = lambda r: pl.BlockSpec((1, page * n_h, dh), paged[r])
    grid_spec = pltpu.PrefetchScalarGridSpec(
        num_scalar_prefetch=1,
        grid=(db, n_pages // pps),
        in_specs=([pl.BlockSpec((1, t, hd), seq)] * 3 + [page_spec(r) for r in range(pps)] * 2
                  + [pl.BlockSpec(far_last.shape, lambda b, g, pt: (0, 0, 0)),
                     pl.BlockSpec(new.shape, lambda b, g, pt: (0, 0))]),
        out_specs=pl.BlockSpec((1, t, hd), seq),
        scratch_shapes=[pltpu.VMEM((n_pages, n_h, dh), F32), pltpu.VMEM((n_pages, rows, 1), F32),
                        pltpu.VMEM((n_pages, rows, 1), F32), pltpu.VMEM((n_pages, rows, dh), F32)],
    )
    ck = cache_k.reshape(n_pool, page * n_h, dh)
    cv = cache_v.reshape(n_pool, page * n_h, dh)
    return pl.pallas_call(
        functools.partial(_attn_sample_kernel, n_heads=n_h, ppb=ppb, pps=pps, scale=dh ** -0.5),
        grid_spec=grid_spec,
        out_shape=jax.ShapeDtypeStruct((db, t, hd), out_dtype),
        compiler_params=_cparams(2), name="moba_sample",
    )(page_table, q, k, v, *([ck] * pps), *([cv] * pps), far_last, new)


def _trunk(x, mods, mc, mn, mm, cb, fb, wts, attend):
    bn, t, d = x.shape
    depth = wts["ffn_up"].shape[0]
    n_ha = mc.shape[2]
    qkw = n_ha * mc.shape[3]
    aw = n_ha * mc.shape[4]
    bw = wts["conv_w"].shape[2]
    big = t >= 512
    act_dtype = BF16 if big else F32
    out_c, out_n, out_m, out_conv, out_k, out_v, out_ffn = [], [], [], [], [], [], []
    for l in range(depth):
        shift, scale, gate = mods[l][0]
        if l % 2 == 0:
            e = l // 2
            assert qkw == aw == bw
            w_a, w_b = wts["in_even"], wts["in_even_b"]
            z_qk, z_vo, z_ag = (z.reshape(bn, t, 2 * qkw) for z in _nm_matmul(
                x, shift, scale, [(w_a, e, 0, None), (w_a, e, 2 * qkw, None), (w_b, e, 0, None)], 2 * qkw, tn=256,
                name="in_even"))
            g, = _nm_matmul(x, shift, scale, [(wts["in_even_gate"], e, 0, None)], LANES, name="in_even_gates")
            h_a, c_new, n_new, m_new = _mlstm(z_qk, z_vo, g.reshape(bn, t, LANES), wts["b_gates"][e],
                                              wts["mlstm_norm_g"][e], mc[e], mn[e], mm[e], out_dtype=act_dtype,
                                              name="mlstm")
            y_b, cb_new = _conv_module(z_ag, 0, 1, cb[e], wts["conv_w"][e], wts["conv_b"][e],
                                       wts["conv_ln_g"][e], wts["conv_ln_b"][e], out_dtype=act_dtype, name="conv_module")
            x = _mm_res([h_a.reshape(bn * t, aw), y_b.reshape(bn * t, bw)],
                        [(wts["out_even"], e, 0), (wts["out_even"], e, aw)], x, gate, name="out_even")
            out_c.append(c_new); out_n.append(n_new); out_m.append(m_new); out_conv.append(cb_new)
        else:
            o = l // 2
            w = wts["in_odd"]
            qkv = _nm_matmul(x, shift, scale, [(w, o, 0, wts["q_norm_g"][o]), (w, o, d, wts["k_norm_g"][o]),
                                               (w, o, 2 * d, None)], d, tn=256, name="in_odd")
            q, k, v = (a.reshape(bn, t, d) for a in qkv)
            a = attend(o, q, k, v, act_dtype)
            x = _mm_res([a.reshape(bn * t, d)], [(wts["out_odd"], o, 0)], x, gate, name="out_odd")
            out_k.append(k); out_v.append(v)
        shift, scale, gate = mods[l][1]
        hist = fb[l].shape[1]
        assert hist <= min(t, SUBLANES)
        buf8 = jnp.pad(fb[l], ((0, 0), (0, SUBLANES - hist), (0, 0)))
        act, g_last = _ffn_up(x, shift, scale, wts["ffn_up"], l, buf8, wts["ffn_conv_w"][l], wts["ffn_conv_b"][l])
        x = _mm_res([act], [(wts["ffn_down"], l, 0)], x, gate, tn=512, name="ffn_down")
        out_ffn.append(g_last[:, SUBLANES - hist:])
    return (x, jnp.stack(out_c), jnp.stack(out_n), jnp.stack(out_m), jnp.stack(out_conv),
            jnp.stack(out_k), jnp.stack(out_v), jnp.stack(out_ffn))


def kernel(x_prompt, x_sample, state_mlstm_c, state_mlstm_n, state_mlstm_m, state_conv, cache_k, cache_v, page_table, state_ffn, c_prompt, c_sample, ada_w, ada_b, w_in_even, b_gates, mlstm_norm_g, conv_w, conv_b, conv_ln_g, conv_ln_b, w_out_even, w_in_odd, q_norm_g, k_norm_g, rel_bias, w_out_odd, ffn_w_up, ffn_conv_w, ffn_conv_b, ffn_w_down):
    bsz, seq, d = x_prompt.shape
    db, dt, _ = x_sample.shape
    depth = ada_w.shape[0]
    n_even, _, n_ha, dk, dv = state_mlstm_c.shape
    n_hc, dh = cache_k.shape[3], cache_k.shape[4]
    qkw, aw, bw = n_ha * dk, n_ha * dv, conv_w.shape[2]
    n_gate = 2 * n_ha
    o_gate = 2 * qkw + 2 * aw
    dt_x = x_prompt.dtype

    wts = {
        "in_even": w_in_even.astype(BF16),
        "in_even_b": w_in_even[:, :, o_gate + n_gate:].astype(BF16),
        "in_even_gate": jnp.pad(w_in_even[:, :, o_gate:o_gate + n_gate],
                                ((0, 0), (0, 0), (0, LANES - n_gate))).astype(BF16),
        "b_gates": jnp.pad(b_gates, ((0, 0), (0, LANES - n_gate))).reshape(n_even, 1, LANES),
        "mlstm_norm_g": mlstm_norm_g.reshape(n_even, 1, aw),
        "conv_w": conv_w, "conv_b": conv_b, "conv_ln_g": conv_ln_g, "conv_ln_b": conv_ln_b,
        "out_even": w_out_even.astype(BF16),
        "in_odd": w_in_odd.astype(BF16), "q_norm_g": q_norm_g, "k_norm_g": k_norm_g,
        "out_odd": w_out_odd.astype(BF16),
        "ffn_up": ffn_w_up.astype(BF16), "ffn_conv_w": ffn_conv_w, "ffn_conv_b": ffn_conv_b,
        "ffn_down": ffn_w_down.astype(BF16),
    }

    mod = _adaln_mods(jnp.concatenate([c_prompt, c_sample], axis=0), ada_w, ada_b)

    def split_mods(rows):
        m = mod[:, rows].reshape(depth, 2, -1, 3, 1, d)
        return [[tuple(m[l, j, :, s] for s in range(3)) for j in range(2)] for l in range(depth)]

    mods_p = split_mods(slice(0, bsz))
    mods_s = split_mods(slice(bsz, bsz + db))

    bias_tiles, rbt = _bias_prompt(rel_bias, MOBA_BLOCK)
    far_last, new = _bias_sample(rel_bias, dt, cache_k.shape[2])

    def prompt_attend(o, q, k, v, out_dtype):
        return _attn_prompt(q, k, v, bias_tiles, rbt, out_dtype=out_dtype)

    def sample_attend(o, q, k, v, out_dtype):
        return _attn_sample(q, k, v, cache_k[o], cache_v[o], page_table, far_last, new, out_dtype=out_dtype)

    zc = jnp.zeros((n_even, bsz, n_ha, dk, dv), dt_x)
    zn = jnp.zeros((n_even, bsz, n_ha, dk), dt_x)
    zm = jnp.zeros((n_even, bsz, n_ha), dt_x)
    zconv = jnp.zeros((n_even, bsz) + state_conv.shape[2:], dt_x)
    zffn = jnp.zeros((depth, bsz) + state_ffn.shape[2:], dt_x)

    y_p, p_c, p_n, p_m, p_conv, p_k, p_v, p_ffn = _trunk(
        x_prompt, mods_p, zc, zn, zm, zconv, zffn, wts, prompt_attend)
    y_s, s_c, s_n, s_m, s_conv, s_k, s_v, s_ffn = _trunk(
        x_sample, mods_s, state_mlstm_c, state_mlstm_n, state_mlstm_m, state_conv, state_ffn, wts, sample_attend)

    def heads(a):
        return a.reshape(a.shape[:3] + (n_hc, dh))

    return (y_p, y_s, p_c, p_n, p_m, p_conv, heads(p_k), heads(p_v), p_ffn,
            s_c, s_n, s_m, s_conv, heads(s_k), heads(s_v), s_ffn)
```

```python
import functools
import math

import jax
import jax.numpy as jnp
from jax import lax
from jax.experimental import pallas as pl
from jax.experimental.pallas import tpu as pltpu

F32 = jnp.float32
BF16 = jnp.bfloat16
HIGHEST = lax.Precision.HIGHEST

LANES = 128
SUBLANES = 8
VMEM_LIMIT_BYTES = 56 * 1024 * 1024

EPS = 1e-6
LN_EPS = 1e-5
MLSTM_CHUNK = 256
MLSTM_MIN_CHUNK = 32
MOBA_BLOCK = 256
MOBA_TOPK = 3
REL_BUCKETS = 32
REL_MAX_DIST = 128
NEG = -1e30
LOG2E = math.log2(math.e)
CONV_HALO = 32

_NT = (((1,), (1,)), ((), ()))
_TN = (((0,), (0,)), ((), ()))


def _cparams(n_axes):
    return pltpu.CompilerParams(dimension_semantics=("arbitrary",) * n_axes, vmem_limit_bytes=VMEM_LIMIT_BYTES)


def _sigmoid(x):
    return jax.nn.sigmoid(x)


def _col_tile(n, pref):
    tile = min(pref, n)
    while n % tile:
        tile -= LANES
    return tile


def _row_tiling(bn, t, tm):
    if t >= tm:
        assert t % tm == 0
        tpb = t // tm
        return 1, tm, bn * tpb, tpb
    gb = min(bn, tm // t)
    assert bn % gb == 0 and t % SUBLANES == 0
    return gb, t, bn // gb, 1


def _mod_kernel(c_ref, w_ref, b_ref, o_ref):
    c = c_ref[...]
    s = (c * _sigmoid(c)).astype(BF16)
    o_ref[...] = jnp.dot(s, w_ref[...].astype(BF16), preferred_element_type=F32) + b_ref[...]


def _adaln_mods(c_all, ada_w, ada_b, tn=512):
    depth, two, d, n3 = ada_w.shape
    nlj = depth * two
    rows = c_all.shape[0]
    tn = _col_tile(n3, tn)
    return pl.pallas_call(
        _mod_kernel,
        grid=(nlj, n3 // tn),
        in_specs=[pl.BlockSpec((rows, d), lambda l, j: (0, 0)),
                  pl.BlockSpec((None, d, tn), lambda l, j: (l, 0, j)),
                  pl.BlockSpec((None, 1, tn), lambda l, j: (l, 0, j))],
        out_specs=pl.BlockSpec((None, rows, tn), lambda l, j: (l, 0, j)),
        out_shape=jax.ShapeDtypeStruct((nlj, rows, n3), F32),
        compiler_params=_cparams(2), name="adaln_mods",
    )(c_all, ada_w.reshape(nlj, d, n3), ada_b.reshape(nlj, 1, n3))


def _nm_matmul_kernel(x_ref, sh_ref, sc_ref, *rest, normed):
    n_s, n_g = len(normed), sum(normed)
    w_refs, g_refs, o_refs, h_ref = rest[:n_s], rest[n_s:n_s + n_g], rest[n_s + n_g:-1], rest[-1]

    @pl.when(pl.program_id(1) == 0)
    def _():
        x = x_ref[...]
        r = lax.rsqrt(jnp.mean(x * x, axis=-1, keepdims=True) + EPS)
        h = (x * r) * (1.0 + sc_ref[...]) + sh_ref[...]
        h_ref[...] = h.reshape(h_ref.shape).astype(BF16)

    h = h_ref[...]
    g_iter = iter(g_refs)
    for w_ref, o_ref, has_g in zip(w_refs, o_refs, normed):
        acc = jnp.dot(h, w_ref[...], preferred_element_type=F32)
        if not has_g:
            o_ref[...] = acc.astype(o_ref.dtype)
            continue
        g = next(g_iter)[...]
        for s in range(acc.shape[1] // LANES):
            y = acc[:, s * LANES:(s + 1) * LANES]
            y = y * lax.rsqrt(jnp.mean(y * y, axis=-1, keepdims=True) + EPS) * g
            o_ref[:, s * LANES:(s + 1) * LANES] = y.astype(o_ref.dtype)


def _nm_matmul(x3, shift, scale, sections, width, *, tm=1024, tn=1024, name="nm_matmul"):
    bn, t, d = x3.shape
    gb, rows, n_i, tpb = _row_tiling(bn, t, tm)
    tmm = gb * rows
    n_s = len(sections)
    tn = _col_tile(math.gcd(width, *[sec[2] for sec in sections if sec[2]]), tn)
    n_j = width // tn
    gains = [sec[3].reshape(1, LANES) for sec in sections if sec[3] is not None]
    in_specs = ([pl.BlockSpec((gb, rows, d), lambda i, j: (i // tpb, i % tpb, 0)),
                 pl.BlockSpec((gb, 1, d), lambda i, j: (i // tpb, 0, 0)),
                 pl.BlockSpec((gb, 1, d), lambda i, j: (i // tpb, 0, 0))]
                + [pl.BlockSpec((None, d, tn), lambda i, j, l=l, j0=c0 // tn: (l, 0, j + j0))
                   for _, l, c0, _, _ in sections]
                + [pl.BlockSpec((1, LANES), lambda i, j: (0, 0))] * len(gains))
    return pl.pallas_call(
        functools.partial(_nm_matmul_kernel, normed=[sec[3] is not None for sec in sections]),
        grid=(n_i, n_j),
        in_specs=in_specs,
        out_specs=[pl.BlockSpec((tmm, tn), lambda i, j: (i, j))] * n_s,
        out_shape=[jax.ShapeDtypeStruct((bn * t, width), sec[4]) for sec in sections],
        scratch_shapes=[pltpu.VMEM((tmm, d), BF16)],
        compiler_params=_cparams(2), name=name,
    )(x3, shift, scale, *[sec[0] for sec in sections], *gains)


def _mm_res_kernel(*refs, n_in):
    a_refs, w_refs = refs[:n_in], refs[n_in:2 * n_in]
    x_ref, gate_ref, o_ref = refs[2 * n_in:]
    acc = None
    for a_ref, w_ref in zip(a_refs, w_refs):
        part = jnp.dot(a_ref[...].astype(BF16), w_ref[...], preferred_element_type=F32)
        acc = part if acc is None else acc + part
    x = x_ref[...]
    o_ref[...] = x + gate_ref[...] * acc.reshape(x.shape)


def _mm_res(a_list, w_list, x3, gate, *, tm=1024, tn=1024, name="mm_res"):
    bn, t, d = x3.shape
    gb, rows, n_i, tpb = _row_tiling(bn, t, tm)
    tmm = gb * rows
    tn = _col_tile(d, tn)
    n_in = len(a_list)
    in_specs = ([pl.BlockSpec((tmm, a.shape[1]), lambda i, j: (i, 0)) for a in a_list]
                + [pl.BlockSpec((None, a.shape[1], tn), lambda i, j, l=l, r=r0 // a.shape[1]: (l, r, j))
                   for a, (_, l, r0) in zip(a_list, w_list)]
                + [pl.BlockSpec((gb, rows, tn), lambda i, j: (i // tpb, i % tpb, j)),
                   pl.BlockSpec((gb, 1, tn), lambda i, j: (i // tpb, 0, j))])
    return pl.pallas_call(
        functools.partial(_mm_res_kernel, n_in=n_in),
        grid=(n_i, d // tn),
        in_specs=in_specs,
        out_specs=pl.BlockSpec((gb, rows, tn), lambda i, j: (i // tpb, i % tpb, j)),
        out_shape=jax.ShapeDtypeStruct((bn, t, d), F32),
        compiler_params=_cparams(2), name=name,
    )(*a_list, *[w for w, _, _ in w_list], x3, gate)


def _mlstm_kernel(q_ref, k_ref, v_ref, zo_ref, g_ref, bg_ref, gn_ref, c0_ref, n0_ref, m0_ref,
                  h_ref, co_ref, no_ref, mo_ref, c_s, n_s, m_s, *, tb, chunk):
    ck = pl.program_id(1)
    L = chunk
    n_heads, dk, dv = c_s.shape

    @pl.when(ck == 0)
    def _():
        c_s[...] = c0_ref[0]
        n_s[...] = n0_ref[0]
        m_s[...] = m0_ref[0]

    def pad_rows(x):
        if tb == L:
            return x
        return jnp.concatenate([x, jnp.zeros((L - tb, x.shape[1]), x.dtype)], axis=0)

    gates = pad_rows(g_ref[0] + bg_ref[...])
    lane = lax.broadcasted_iota(jnp.int32, (L, LANES), 1)
    li_all = jnp.where(lane < n_heads, gates, 0.0)
    lf_all = jnp.where((lane >= n_heads) & (lane < 2 * n_heads),
                       -(jnp.maximum(-gates, 0.0) + jnp.log1p(jnp.exp(-jnp.abs(gates)))), 0.0)
    if tb != L:
        real = lax.broadcasted_iota(jnp.int32, (L, LANES), 0) < tb
        li_all = jnp.where(real | (lane >= n_heads), li_all, NEG)
        lf_all = jnp.where(real, lf_all, 0.0)
    ri = lax.broadcasted_iota(jnp.int32, (L, L), 0)
    ci = lax.broadcasted_iota(jnp.int32, (L, L), 1)
    tri = ri >= ci
    y = li_all + jnp.dot(tri.astype(F32), lf_all, precision=HIGHEST, preferred_element_type=F32)
    er = lax.broadcasted_iota(jnp.int32, (SUBLANES, LANES), 0)
    el = lax.broadcasted_iota(jnp.int32, (SUBLANES, LANES), 1)
    e = jnp.where(er < n_heads, jnp.where(el == er + n_heads, 1.0, 0.0) - jnp.where(el == er, 1.0, 0.0), 0.0)
    rt = lax.dot_general(e, y, _NT, precision=HIGHEST, preferred_element_type=F32)

    for hd in range(n_heads):
        li = y[:, hd:hd + 1]
        b = y[:, n_heads + hd:n_heads + hd + 1]
        r_row = rt[hd:hd + 1, :]
        q = pad_rows(q_ref[0, :, hd * dk:(hd + 1) * dk]).astype(F32) * (dk ** -0.5)
        k = pad_rows(k_ref[0, :, hd * dk:(hd + 1) * dk]).astype(F32)
        v = pad_rows(v_ref[0, :, hd * dv:(hd + 1) * dv]).astype(F32)
        a = b + m_s[hd][:, :1]
        dmat = jnp.where(tri, b - r_row, NEG)
        mt = jnp.maximum(a, jnp.max(dmat, axis=-1, keepdims=True))
        w_inter = jnp.exp(a - mt)
        qb, kb, vb = q.astype(BF16), k.astype(BF16), v.astype(BF16)
        s = lax.dot_general(qb, kb, _NT, preferred_element_type=F32) * jnp.exp(dmat - mt)
        c_old = c_s[hd]
        n_old = n_s[hd]
        num = (w_inter * jnp.dot(qb, c_old.astype(BF16), preferred_element_type=F32)
               + jnp.dot(s.astype(BF16), vb, preferred_element_type=F32))
        den = w_inter * jnp.sum(q * n_old, axis=-1, keepdims=True) + jnp.sum(s, axis=-1, keepdims=True)
        hh = (num / jnp.maximum(jnp.abs(den), jnp.exp(-mt)))[:tb]
        hn = hh * lax.rsqrt(jnp.mean(hh * hh, axis=-1, keepdims=True) + EPS)
        h_ref[0, :, hd * dv:(hd + 1) * dv] = (hn * gn_ref[:, hd * dv:(hd + 1) * dv]
                                              * _sigmoid(zo_ref[0, :, hd * dv:(hd + 1) * dv])).astype(h_ref.dtype)

        m_l = mt[L - 1:L]
        g_inter = jnp.exp(a[L - 1:L] - m_l)
        g_s = jnp.exp(b[L - 1:L] - b + li - m_l)
        c_s[hd] = g_inter * c_old + lax.dot_general(kb, (g_s * v).astype(BF16), _TN, preferred_element_type=F32)
        n_s[hd] = g_inter * n_old + jnp.sum(g_s * k, axis=0, keepdims=True)
        m_s[hd] = jnp.broadcast_to(m_l, (1, LANES))

    @pl.when(ck == pl.num_programs(1) - 1)
    def _():
        co_ref[0] = c_s[...]
        no_ref[0] = n_s[...]
        mo_ref[0] = m_s[...]


def _mlstm(q3, k3, v3, o3, g3, bg, gn, c0, n0, m0, *, out_dtype, name):
    bn, t, _ = q3.shape
    _, n_heads, dk, dv = c0.shape
    assert dk == dv and dk % LANES == 0 and 2 * n_heads <= LANES and n_heads <= SUBLANES
    if t % MLSTM_CHUNK == 0:
        tb = chunk = MLSTM_CHUNK
    else:
        assert t <= MLSTM_MIN_CHUNK
        tb, chunk = t, MLSTM_MIN_CHUNK
    n_c = t // tb
    hw = n_heads * dk
    tok = lambda b, c: (b, c, 0)
    state = lambda b, c: (b, 0, 0, 0)
    h, c, n, m = pl.pallas_call(
        functools.partial(_mlstm_kernel, tb=tb, chunk=chunk),
        grid=(bn, n_c),
        in_specs=[pl.BlockSpec((1, tb, hw), tok)] * 4 + [
                  pl.BlockSpec((1, tb, LANES), tok),
                  pl.BlockSpec((1, LANES), lambda b, c: (0, 0)),
                  pl.BlockSpec((1, hw), lambda b, c: (0, 0)),
                  pl.BlockSpec((1, n_heads, dk, dv), state), pl.BlockSpec((1, n_heads, 1, dk), state),
                  pl.BlockSpec((1, n_heads, 1, LANES), state)],
        out_specs=[pl.BlockSpec((1, tb, hw), tok),
                   pl.BlockSpec((1, n_heads, dk, dv), state), pl.BlockSpec((1, n_heads, 1, dk), state),
                   pl.BlockSpec((1, n_heads, 1, LANES), state)],
        out_shape=[jax.ShapeDtypeStruct((bn, t, hw), out_dtype),
                   jax.ShapeDtypeStruct((bn, n_heads, dk, dv), F32),
                   jax.ShapeDtypeStruct((bn, n_heads, 1, dk), F32),
                   jax.ShapeDtypeStruct((bn, n_heads, 1, LANES), F32)],
        scratch_shapes=[pltpu.VMEM((n_heads, dk, dv), F32), pltpu.VMEM((n_heads, 1, dk), F32),
                        pltpu.VMEM((n_heads, 1, LANES), F32)],
        compiler_params=_cparams(2), name=name,
    )(q3, k3, v3, o3, g3, bg, gn, c0, n0.reshape(bn, n_heads, 1, dk),
      jnp.broadcast_to(m0[:, :, None, None], (bn, n_heads, 1, LANES)))
    return h, c, n.reshape(bn, n_heads, dk), m[:, :, 0, 0]


def _convmod_kernel(a_ref, g_ref, buf_ref, w_ref, b_ref, lg_ref, lb_ref, y_ref, nb_ref, u_s, *, tt, width, rc):
    t = pl.program_id(1)
    hist = width - 1
    bias = b_ref[...]
    lg = lg_ref[...]
    lb = lb_ref[...]
    n_p = rc // SUBLANES
    for bi in range(a_ref.shape[0]):
        @pl.when(t == 0)
        def _():
            buf = buf_ref[bi]
            for r in range(SUBLANES):
                u_s[bi, r, 0:hist - r, :] = buf[r:, :]

        @pl.when(t > 0)
        def _():
            for r in range(SUBLANES):
                u_s[bi, r, 0:CONV_HALO, :] = u_s[bi, r, tt:tt + CONV_HALO, :]

        u = a_ref[bi] * _sigmoid(g_ref[bi])
        for r in range(SUBLANES):
            u_s[bi, r, hist - r:hist - r + tt, :] = u
        for r0 in range(0, tt, rc):
            accs = [jnp.broadcast_to(bias, (SUBLANES, bias.shape[1]))] * n_p
            for j in range(width):
                r = j % SUBLANES
                wj = w_ref[j]
                accs = [acc + wj * u_s[bi, r, r0 + p * SUBLANES + j - r:r0 + (p + 1) * SUBLANES + j - r, :]
                        for p, acc in enumerate(accs)]
            acc = accs[0] if n_p == 1 else jnp.concatenate(accs, axis=0)
            mu = jnp.mean(acc, axis=-1, keepdims=True)
            xc = acc - mu
            var = jnp.mean(xc * xc, axis=-1, keepdims=True)
            y = xc * lax.rsqrt(var + LN_EPS) * lg + lb
            y_ref[bi, r0:r0 + rc, :] = (y * _sigmoid(y)).astype(y_ref.dtype)

        @pl.when(t == pl.num_programs(1) - 1)
        def _():
            nb_ref[bi] = u_s[bi, 0, tt:tt + hist, :]


def _conv_module(a3, g3, cbuf, conv_w, conv_b, ln_g, ln_b, *, out_dtype, name):
    bn, t, _ = a3.shape
    width, ch = conv_w.shape
    assert width - 1 <= CONV_HALO
    tt = 128 if t % 128 == 0 else t
    n_t = t // tt
    assert n_t == 1 or tt >= CONV_HALO
    gb = 1 if n_t > 1 else math.gcd(bn, SUBLANES)
    y, nb = pl.pallas_call(
        functools.partial(_convmod_kernel, tt=tt, width=width,
                          rc=4 * SUBLANES if tt % (4 * SUBLANES) == 0 else SUBLANES),
        grid=(bn // gb, n_t),
        in_specs=[pl.BlockSpec((gb, tt, ch), lambda b, i: (b, i, 0)),
                  pl.BlockSpec((gb, tt, ch), lambda b, i: (b, i, 0)),
                  pl.BlockSpec((gb, width - 1, ch), lambda b, i: (b, 0, 0)),
                  pl.BlockSpec((width, SUBLANES, ch), lambda b, i: (0, 0, 0)),
                  pl.BlockSpec((1, ch), lambda b, i: (0, 0)),
                  pl.BlockSpec((1, ch), lambda b, i: (0, 0)),
                  pl.BlockSpec((1, ch), lambda b, i: (0, 0))],
        out_specs=[pl.BlockSpec((gb, tt, ch), lambda b, i: (b, i, 0)),
                   pl.BlockSpec((gb, width - 1, ch), lambda b, i: (b, 0, 0))],
        out_shape=[jax.ShapeDtypeStruct((bn, t, ch), out_dtype),
                   jax.ShapeDtypeStruct((bn, width - 1, ch), F32)],
        scratch_shapes=[pltpu.VMEM((gb, SUBLANES, CONV_HALO + tt, ch), F32)],
        compiler_params=_cparams(2), name=name,
    )(a3, g3, cbuf, jnp.broadcast_to(conv_w[:, None, :], (width, SUBLANES, ch)), conv_b.reshape(1, ch),
      ln_g.reshape(1, ch), ln_b.reshape(1, ch))
    return y, nb


def _ffn_up_kernel(x_ref, sh_ref, sc_ref, wg_ref, wu_ref, buf_ref, cw_ref, cb_ref, act_ref, gl_ref, h_s, carry_s,
                   *, tpb, rows, strip):
    ti = pl.program_id(0) % tpb
    j = pl.program_id(1)

    @pl.when(j == 0)
    def _():
        x = x_ref[...]
        r = lax.rsqrt(jnp.mean(x * x, axis=-1, keepdims=True) + EPS)
        h = (x * r) * (1.0 + sc_ref[...]) + sh_ref[...]
        h_s[...] = h.reshape(h_s.shape).astype(BF16)

    hist = buf_ref.shape[1]
    if tpb > 1:
        @pl.when(ti == 0)
        def _():
            carry_s[j] = jnp.zeros(carry_s.shape[1:], F32)
            carry_s[j, 0:hist, :] = buf_ref[0]

    h = h_s[...]
    n = h.shape[0]
    gb = n // rows
    tn = wg_ref.shape[1]
    for c0 in range(0, tn, strip):
        cs = slice(c0, c0 + strip)
        g = jnp.dot(h, wg_ref[:, cs], preferred_element_type=F32)
        u = jnp.dot(h, wu_ref[:, cs], preferred_element_type=F32)
        if tpb > 1:
            prev = carry_s[j, :, cs]
            carry_s[j, :, cs] = pltpu.roll(g[n - SUBLANES:], 2, axis=0)
        r8 = lax.broadcasted_iota(jnp.int32, (gb * SUBLANES, strip), 0) % SUBLANES
        if tpb == 1:
            rows01 = [jnp.broadcast_to(buf_ref[:, i:i + 1, cs], (gb, SUBLANES, strip)).reshape(gb * SUBLANES, strip)
                      for i in range(hist)]
            prev = jnp.where(r8 == 0, rows01[0], rows01[1])
        prev1 = pltpu.roll(prev, gb * SUBLANES - 1, axis=0)
        gm1 = pltpu.roll(g, 1, axis=0)
        gm2 = pltpu.roll(g, 2, axis=0)
        if rows == SUBLANES:
            gm1 = jnp.where(r8 < 1, prev1, gm1)
            gm2 = jnp.where(r8 < 2, prev, gm2)
            gl_ref[:, :, cs] = g.reshape(gb, SUBLANES, strip)
        else:
            gm1 = jnp.concatenate([jnp.where(r8 < 1, prev1, gm1[:SUBLANES]), gm1[SUBLANES:]], axis=0)
            gm2 = jnp.concatenate([jnp.where(r8 < 2, prev, gm2[:SUBLANES]), gm2[SUBLANES:]], axis=0)
            gl_ref[:, :, cs] = g[n - SUBLANES:].reshape(gb, SUBLANES, strip)
        gc = cw_ref[0:1, cs] * gm2 + cw_ref[1:2, cs] * gm1 + cw_ref[2:3, cs] * g + cb_ref[:, cs]
        act_ref[:, cs] = ((gc * _sigmoid(gc)) * u).astype(act_ref.dtype)


def _ffn_up(x3, shift, scale, w, layer, buf, conv_w, conv_b, *, tm=1024, tn=512, name="ffn_up"):
    bn, t, d = x3.shape
    dff = w.shape[2] // 2
    gb, rows, n_i, tpb = _row_tiling(bn, t, tm)
    tmm = gb * rows
    tn = _col_tile(dff, tn)
    n_f = dff // tn
    hist = buf.shape[1]
    assert conv_w.shape[0] == 3 and hist == 2 and (tpb == 1 or gb == 1)
    act, g_tail = pl.pallas_call(
        functools.partial(_ffn_up_kernel, tpb=tpb, rows=rows, strip=min(tn, 256)),
        grid=(n_i, n_f),
        in_specs=[pl.BlockSpec((gb, rows, d), lambda i, j: (i // tpb, i % tpb, 0)),
                  pl.BlockSpec((gb, 1, d), lambda i, j: (i // tpb, 0, 0)),
                  pl.BlockSpec((gb, 1, d), lambda i, j: (i // tpb, 0, 0)),
                  pl.BlockSpec((None, d, tn), lambda i, j: (layer, 0, j)),
                  pl.BlockSpec((None, d, tn), lambda i, j: (layer, 0, j + n_f)),
                  pl.BlockSpec((gb, hist, tn), lambda i, j: (i // tpb, 0, j)),
                  pl.BlockSpec((3, tn), lambda i, j: (0, j)),
                  pl.BlockSpec((1, tn), lambda i, j: (0, j))],
        out_specs=[pl.BlockSpec((tmm, tn), lambda i, j: (i, j)),
                   pl.BlockSpec((gb, SUBLANES, tn), lambda i, j: (i, 0, j))],
        out_shape=[jax.ShapeDtypeStruct((bn * t, dff), BF16),
                   jax.ShapeDtypeStruct((n_i * gb, SUBLANES, dff), F32)],
        scratch_shapes=[pltpu.VMEM((tmm, d), BF16), pltpu.VMEM((n_f, SUBLANES, tn), F32)],
        compiler_params=_cparams(2), name=name,
    )(x3, shift, scale, w, w, buf, conv_w, conv_b.reshape(1, dff))
    return act, g_tail.reshape(bn, tpb, SUBLANES, dff)[:, tpb - 1]


def _t5_bucket(dist):
    n = jnp.maximum(dist, 0)
    exact = REL_BUCKETS // 2
    large = exact + (jnp.log(jnp.maximum(n, 1).astype(F32) / exact)
                     / math.log(REL_MAX_DIST / exact) * (REL_BUCKETS - exact)).astype(jnp.int32)
    return jnp.where(n < exact, n, jnp.minimum(large, REL_BUCKETS - 1))


def _bias_lookup(dist, table_col):
    bucket = _t5_bucket(dist)
    bias = jnp.zeros(dist.shape, F32)
    for kk in range(REL_BUCKETS):
        bias = jnp.where(bucket == kk, table_col(kk), bias)
    return bias * LOG2E


def _bias_prompt_kernel(rb_ref, o_ref, *, blk):
    rb = rb_ref[0]
    key = lax.broadcasted_iota(jnp.int32, (blk, blk), 0)
    qry = lax.broadcasted_iota(jnp.int32, (blk, blk), 1)
    col = lambda kk: rb[:, kk:kk + 1]
    own = _bias_lookup(qry - key, col)
    o_ref[0, 0] = jnp.where(qry >= key, own, NEG)
    o_ref[0, 1] = _bias_lookup(blk + qry - key, col)


def _bias_prompt(rel_bias, blk):
    n_b, n_h = rel_bias.shape
    rbt = jnp.pad(rel_bias.T, ((0, 0), (0, LANES - n_b))).reshape(n_h, 1, LANES)
    return pl.pallas_call(
        functools.partial(_bias_prompt_kernel, blk=blk),
        grid=(n_h,),
        in_specs=[pl.BlockSpec((1, 1, LANES), lambda h: (h, 0, 0))],
        out_specs=pl.BlockSpec((1, 2, blk, blk), lambda h: (h, 0, 0, 0)),
        out_shape=jax.ShapeDtypeStruct((n_h, 2, blk, blk), F32),
        compiler_params=_cparams(1), name="bias_prompt",
    )(rbt), rbt


def _bias_sample_kernel(rb_ref, fl_ref, new_ref, *, n_heads, t, page):
    rb = rb_ref[...]
    col = lambda kk: rb[:, kk:kk + 1]
    for ref, base in ((fl_ref.at[0], 2 * page), (fl_ref.at[1], page), (new_ref, 0)):
        shape = ref.shape
        r = lax.broadcasted_iota(jnp.int32, shape, 0)
        c = lax.broadcasted_iota(jnp.int32, shape, 1)
        if ref is new_ref:
            key_head, key_off = c // t, c % t
        else:
            key_head, key_off = c % n_heads, c // n_heads
        dist = base + r % t - key_off
        ok = (r // t == key_head) & (dist >= 0)
        ref[...] = jnp.where(ok, _bias_lookup(dist, col), NEG)


def _bias_sample(rel_bias, t, page):
    n_b, n_h = rel_bias.shape
    rows = t * n_h
    rb = jnp.pad(jnp.repeat(rel_bias.T, t, axis=0), ((0, 0), (0, LANES - n_b)))
    return pl.pallas_call(
        functools.partial(_bias_sample_kernel, n_heads=n_h, t=t, page=page),
        out_shape=[jax.ShapeDtypeStruct((2, rows, page * n_h), F32),
                   jax.ShapeDtypeStruct((rows, rows), F32)],
        compiler_params=pltpu.CompilerParams(vmem_limit_bytes=VMEM_LIMIT_BYTES), name="bias_sample",
    )(rb)


def _attn_prompt_kernel(q_ref, k_ref, v_ref, bias_ref, rb_ref, o_ref, km_s, kb_s, vt_s, s_s, *, blk, n_blk, scale):
    km_s[...] = jnp.zeros_like(km_s)
    for j in range(n_blk):
        kj = k_ref[0, j * blk:(j + 1) * blk, :]
        km_s[j:j + 1, :] = jnp.mean(kj, axis=0, keepdims=True)
        kb_s[j * blk:(j + 1) * blk, :] = kj.astype(BF16)
        vt_s[:, j * blk:(j + 1) * blk] = v_ref[0, j * blk:(j + 1) * blk, :].T.astype(BF16)

    far = rb_ref[0][:, REL_BUCKETS - 1:REL_BUCKETS] * LOG2E
    km = km_s[...]
    row = lax.broadcasted_iota(jnp.int32, (km.shape[0], blk), 0)
    tile = 0
    for c in range(n_blk):
        qf = q_ref[0, c * blk:(c + 1) * blk, :]
        qb = qf.astype(BF16)
        add = [0.0] * c
        if c > MOBA_TOPK:
            sc = lax.dot_general(km, qf, _NT, precision=HIGHEST, preferred_element_type=F32)
            sc = jnp.where(row < c, sc, -jnp.inf)
            for j in range(c):
                sj = sc[j:j + 1, :]
                ahead = (sc > sj) | ((sc == sj) & (row < j))
                rank = jnp.sum(jnp.where(ahead, 1.0, 0.0), axis=0, keepdims=True)
                add[j] = jnp.where(rank < MOBA_TOPK, 0.0, NEG)
        m = None
        for j in range(c + 1):
            st = lax.dot_general(kb_s[j * blk:(j + 1) * blk, :], qb, _NT, preferred_element_type=F32) * (scale * LOG2E)
            if j == c:
                st = st + bias_ref[0, 0]
            elif j == c - 1:
                st = st + (bias_ref[0, 1] + add[j])
            else:
                st = st + (add[j] + far)
            s_s[tile + j] = st
            mj = jnp.max(st, axis=0, keepdims=True)
            m = mj if m is None else jnp.maximum(m, mj)
        l = jnp.zeros_like(m)
        acc = jnp.zeros((vt_s.shape[0], blk), F32)
        for j in range(c + 1):
            p = jnp.exp2(s_s[tile + j] - m)
            l = l + jnp.sum(p, axis=0, keepdims=True)
            acc = acc + jnp.dot(vt_s[:, j * blk:(j + 1) * blk], p.astype(BF16), preferred_element_type=F32)
        o_ref[0, c * blk:(c + 1) * blk, :] = (acc / l).T.astype(o_ref.dtype)
        tile += c + 1


def _attn_prompt(q3, k3, v3, bias_tiles, rbt, *, out_dtype):
    bn, s, _ = q3.shape
    n_h = bias_tiles.shape[0]
    dh = q3.shape[2] // n_h
    blk = MOBA_BLOCK
    assert s % blk == 0 and dh == LANES and blk >= REL_MAX_DIST
    n_blk = s // blk
    head = lambda b, h: (b, 0, h)
    return pl.pallas_call(
        functools.partial(_attn_prompt_kernel, blk=blk, n_blk=n_blk, scale=dh ** -0.5),
        grid=(bn, n_h),
        in_specs=[pl.BlockSpec((1, s, dh), head), pl.BlockSpec((1, s, dh), head), pl.BlockSpec((1, s, dh), head),
                  pl.BlockSpec((1, 2, blk, blk), lambda b, h: (h, 0, 0, 0)),
                  pl.BlockSpec((1, 1, LANES), lambda b, h: (h, 0, 0))],
        out_specs=pl.BlockSpec((1, s, dh), head),
        out_shape=jax.ShapeDtypeStruct(q3.shape, out_dtype),
        scratch_shapes=[pltpu.VMEM((-(-n_blk // SUBLANES) * SUBLANES, dh), F32), pltpu.VMEM((s, dh), BF16),
                        pltpu.VMEM((dh, s), BF16), pltpu.VMEM((n_blk * (n_blk + 1) // 2, blk, blk), F32)],
        compiler_params=_cparams(2), name="moba_prompt",
    )(q3, k3, v3, bias_tiles, rbt)


def _attn_sample_kernel(pt_ref, q_ref, kn_ref, vn_ref, *refs, n_heads, ppb, pps, scale):
    del pt_ref
    kp_refs, vp_refs = refs[:pps], refs[pps:2 * pps]
    bias_ref, new_ref, o_ref, ks_s, m_s, l_s, o_s = refs[2 * pps:]
    g = pl.program_id(1)
    last = pl.num_programs(1) - 1
    n_tok = q_ref.shape[1]
    dh = q_ref.shape[2] // n_heads

    def head_major(ref):
        return jnp.concatenate([ref[0, :, h * dh:(h + 1) * dh] for h in range(n_heads)], axis=0)

    qf = head_major(q_ref)
    qb = qf.astype(BF16)
    rows = n_heads * n_tok

    def partial_softmax(kk, vv, bias):
        s = lax.dot_general(qb, kk.astype(BF16), _NT, preferred_element_type=F32) * (scale * LOG2E) + bias
        m = jnp.max(s, axis=-1, keepdims=True)
        e = jnp.exp2(s - m)
        return m, jnp.sum(e, axis=-1, keepdims=True), jnp.dot(e.astype(BF16), vv.astype(BF16),
                                                              preferred_element_type=F32)

    for r in range(pps):
        pg = g * pps + r
        kp = kp_refs[r][0]
        ks_s[pg] = jnp.sum(kp.reshape(kp.shape[0] // n_heads, n_heads, dh), axis=0)
        bias = bias_ref[jnp.where(g == last, 1, 0)] if r == pps - 1 else bias_ref[0]
        m, l, o = partial_softmax(kp, vp_refs[r][0], bias)
        m_s[pg] = m
        l_s[pg] = l
        o_s[pg] = o

    @pl.when(g == last)
    def _():
        n_pg = m_s.shape[0]
        n_blk = n_pg // ppb
        inv = 1.0 / (ppb * (kp_refs[0].shape[1] // n_heads))
        scs = []
        for j in range(n_blk):
            km = ks_s[j * ppb]
            for pp in range(1, ppb):
                km = km + ks_s[j * ppb + pp]
            km = jnp.broadcast_to((km * inv)[:, None, :], (n_heads, n_tok, dh)).reshape(rows, dh)
            scs.append(jnp.sum(qf * km, axis=-1, keepdims=True))
        sel = []
        for j in range(n_blk):
            rank = jnp.zeros((rows, 1), F32)
            for jj in range(n_blk):
                if jj != j:
                    ahead = (scs[jj] >= scs[j]) if jj < j else (scs[jj] > scs[j])
                    rank = rank + jnp.where(ahead, 1.0, 0.0)
            sel.append(rank < MOBA_TOPK)
        m_n, l_n, o_n = partial_softmax(head_major(kn_ref), head_major(vn_ref), new_ref[...])
        m_all = m_n
        for pg in range(n_pg):
            m_all = jnp.maximum(m_all, jnp.where(sel[pg // ppb], m_s[pg], NEG))
        w_n = jnp.exp2(m_n - m_all)
        l_all = w_n * l_n
        o_all = w_n * o_n
        for pg in range(n_pg):
            w = jnp.where(sel[pg // ppb], jnp.exp2(m_s[pg] - m_all), 0.0)
            l_all = l_all + w * l_s[pg]
            o_all = o_all + w * o_s[pg]
        o = (o_all / l_all).astype(o_ref.dtype)
        for h in range(n_heads):
            o_ref[0, :, h * dh:(h + 1) * dh] = o[h * n_tok:(h + 1) * n_tok]


def _attn_sample(q, k, v, cache_k, cache_v, page_table, far_last, new, *, out_dtype, pps=8):
    db, t, hd = q.shape
    n_pool, page, n_h, dh = cache_k.shape
    n_pages = page_table.shape[1]
    rows = t * n_h
    assert MOBA_BLOCK % page == 0 and (n_pages * page) % MOBA_BLOCK == 0 and t <= MOBA_BLOCK
    assert page >= REL_MAX_DIST and n_pages * page // MOBA_BLOCK > MOBA_TOPK and dh == LANES
    assert n_pages % pps == 0
    ppb = MOBA_BLOCK // page
    seq = lambda b, g, pt: (b, 0, 0)
    paged = [(lambda b, g, pt, r=r: (pt[b, g * pps + r], 0, 0)) for r in range(pps)]
    page_spec = lambda r: pl.BlockSpec((1, page * n_h, dh), paged[r])
    grid_spec = pltpu.PrefetchScalarGridSpec(
        num_scalar_prefetch=1,
        grid=(db, n_pages // pps),
        in_specs=([pl.BlockSpec((1, t, hd), seq)] * 3 + [page_spec(r) for r in range(pps)] * 2
                  + [pl.BlockSpec(far_last.shape, lambda b, g, pt: (0, 0, 0)),
                     pl.BlockSpec(new.shape, lambda b, g, pt: (0, 0))]),
        out_specs=pl.BlockSpec((1, t, hd), seq),
        scratch_shapes=[pltpu.VMEM((n_pages, n_h, dh), F32), pltpu.VMEM((n_pages, rows, 1), F32),
                        pltpu.VMEM((n_pages, rows, 1), F32), pltpu.VMEM((n_pages, rows, dh), F32)],
    )
    ck = cache_k.reshape(n_pool, page * n_h, dh)
    cv = cache_v.reshape(n_pool, page * n_h, dh)
    return pl.pallas_call(
        functools.partial(_attn_sample_kernel, n_heads=n_h, ppb=ppb, pps=pps, scale=dh ** -0.5),
        grid_spec=grid_spec,
        out_shape=jax.ShapeDtypeStruct((db, t, hd), out_dtype),
        compiler_params=_cparams(2), name="moba_sample",
    )(page_table, q, k, v, *([ck] * pps), *([cv] * pps), far_last, new)


def _trunk(x, mods, mc, mn, mm, cb, fb, wts, attend):
    bn, t, d = x.shape
    depth = wts["ffn_up"].shape[0]
    n_ha = mc.shape[2]
    qkw = n_ha * mc.shape[3]
    aw = n_ha * mc.shape[4]
    bw = wts["conv_w"].shape[2]
    big = t >= 512
    act_dtype = BF16 if big else F32
    out_c, out_n, out_m, out_conv, out_k, out_v, out_ffn = [], [], [], [], [], [], []
    for l in range(depth):
        shift, scale, gate = mods[l][0]
        if l % 2 == 0:
            e = l // 2
            assert qkw == aw == bw
            w_a, w_b = wts["in_even"], wts["in_even_b"]
            secs = ([(w_a, e, i * qkw, None, act_dtype) for i in range(3)] + [(w_a, e, 3 * qkw, None, F32)]
                    + [(w_b, e, i * bw, None, F32) for i in range(2)])
            zq, zk, zv, zo, za, zg = (z.reshape(bn, t, qkw) for z in _nm_matmul(
                x, shift, scale, secs, qkw, tn=256 if big else 128, name="in_even"))
            g, = _nm_matmul(x, shift, scale, [(wts["in_even_gate"], e, 0, None, F32)], LANES, name="in_even_gates")
            h_a, c_new, n_new, m_new = _mlstm(zq, zk, zv, zo, g.reshape(bn, t, LANES), wts["b_gates"][e],
                                              wts["mlstm_norm_g"][e], mc[e], mn[e], mm[e], out_dtype=act_dtype,
                                              name="mlstm")
            y_b, cb_new = _conv_module(za, zg, cb[e], wts["conv_w"][e], wts["conv_b"][e],
                                       wts["conv_ln_g"][e], wts["conv_ln_b"][e], out_dtype=act_dtype, name="conv_module")
            x = _mm_res([h_a.reshape(bn * t, aw), y_b.reshape(bn * t, bw)],
                        [(wts["out_even"], e, 0), (wts["out_even"], e, aw)], x, gate, name="out_even")
            out_c.append(c_new); out_n.append(n_new); out_m.append(m_new); out_conv.append(cb_new)
        else:
            o = l // 2
            w = wts["in_odd"]
            qkv = _nm_matmul(x, shift, scale, [(w, o, 0, wts["q_norm_g"][o], F32), (w, o, d, wts["k_norm_g"][o], F32),
                                               (w, o, 2 * d, None, F32)], d, tn=256, name="in_odd")
            q, k, v = (a.reshape(bn, t, d) for a in qkv)
            a = attend(o, q, k, v, act_dtype)
            x = _mm_res([a.reshape(bn * t, d)], [(wts["out_odd"], o, 0)], x, gate, name="out_odd")
            out_k.append(k); out_v.append(v)
        shift, scale, gate = mods[l][1]
        hist = fb[l].shape[1]
        assert hist <= min(t, SUBLANES)
        act, g_last = _ffn_up(x, shift, scale, wts["ffn_up"], l, fb[l], wts["ffn_conv_w"][l], wts["ffn_conv_b"][l])
        x = _mm_res([act], [(wts["ffn_down"], l, 0)], x, gate, tn=512, name="ffn_down")
        out_ffn.append(g_last[:, SUBLANES - hist:])
    return (x, jnp.stack(out_c), jnp.stack(out_n), jnp.stack(out_m), jnp.stack(out_conv),
            jnp.stack(out_k), jnp.stack(out_v), jnp.stack(out_ffn))


def kernel(x_prompt, x_sample, state_mlstm_c, state_mlstm_n, state_mlstm_m, state_conv, cache_k, cache_v, page_table, state_ffn, c_prompt, c_sample, ada_w, ada_b, w_in_even, b_gates, mlstm_norm_g, conv_w, conv_b, conv_ln_g, conv_ln_b, w_out_even, w_in_odd, q_norm_g, k_norm_g, rel_bias, w_out_odd, ffn_w_up, ffn_conv_w, ffn_conv_b, ffn_w_down):
    bsz, seq, d = x_prompt.shape
    db, dt, _ = x_sample.shape
    depth = ada_w.shape[0]
    n_even, _, n_ha, dk, dv = state_mlstm_c.shape
    n_hc, dh = cache_k.shape[3], cache_k.shape[4]
    qkw, aw, bw = n_ha * dk, n_ha * dv, conv_w.shape[2]
    n_gate = 2 * n_ha
    o_gate = 2 * qkw + 2 * aw
    dt_x = x_prompt.dtype

    wts = {
        "in_even": w_in_even.astype(BF16),
        "in_even_b": w_in_even[:, :, o_gate + n_gate:].astype(BF16),
        "in_even_gate": jnp.pad(w_in_even[:, :, o_gate:o_gate + n_gate],
                                ((0, 0), (0, 0), (0, LANES - n_gate))).astype(BF16),
        "b_gates": jnp.pad(b_gates, ((0, 0), (0, LANES - n_gate))).reshape(n_even, 1, LANES),
        "mlstm_norm_g": mlstm_norm_g.reshape(n_even, 1, aw),
        "conv_w": conv_w, "conv_b": conv_b, "conv_ln_g": conv_ln_g, "conv_ln_b": conv_ln_b,
        "out_even": w_out_even.astype(BF16),
        "in_odd": w_in_odd.astype(BF16), "q_norm_g": q_norm_g, "k_norm_g": k_norm_g,
        "out_odd": w_out_odd.astype(BF16),
        "ffn_up": ffn_w_up.astype(BF16), "ffn_conv_w": ffn_conv_w, "ffn_conv_b": ffn_conv_b,
        "ffn_down": ffn_w_down.astype(BF16),
    }

    mod = _adaln_mods(jnp.concatenate([c_prompt, c_sample], axis=0), ada_w, ada_b)

    def split_mods(rows):
        m = mod[:, rows].reshape(depth, 2, -1, 3, 1, d)
        return [[tuple(m[l, j, :, s] for s in range(3)) for j in range(2)] for l in range(depth)]

    mods_p = split_mods(slice(0, bsz))
    mods_s = split_mods(slice(bsz, bsz + db))

    bias_tiles, rbt = _bias_prompt(rel_bias, MOBA_BLOCK)
    far_last, new = _bias_sample(rel_bias, dt, cache_k.shape[2])

    def prompt_attend(o, q, k, v, out_dtype):
        return _attn_prompt(q, k, v, bias_tiles, rbt, out_dtype=out_dtype)

    def sample_attend(o, q, k, v, out_dtype):
        return _attn_sample(q, k, v, cache_k[o], cache_v[o], page_table, far_last, new, out_dtype=out_dtype)

    zc = jnp.zeros((n_even, bsz, n_ha, dk, dv), dt_x)
    zn = jnp.zeros((n_even, bsz, n_ha, dk), dt_x)
    zm = jnp.zeros((n_even, bsz, n_ha), dt_x)
    zconv = jnp.zeros((n_even, bsz) + state_conv.shape[2:], dt_x)
    zffn = jnp.zeros((depth, bsz) + state_ffn.shape[2:], dt_x)

    y_p, p_c, p_n, p_m, p_conv, p_k, p_v, p_ffn = _trunk(
        x_prompt, mods_p, zc, zn, zm, zconv, zffn, wts, prompt_attend)
    y_s, s_c, s_n, s_m, s_conv, s_k, s_v, s_ffn = _trunk(
        x_sample, mods_s, state_mlstm_c, state_mlstm_n, state_mlstm_m, state_conv, state_ffn, wts, sample_attend)

    def heads(a):
        return a.reshape(a.shape[:3] + (n_hc, dh))

    return (y_p, y_s, p_c, p_n, p_m, p_conv, heads(p_k), heads(p_v), p_ffn,
            s_c, s_n, s_m, s_conv, heads(s_k), heads(s_v), s_ffn)
```

```python
import functools
import math

import jax
import jax.numpy as jnp
from jax import lax
from jax.experimental import pallas as pl
from jax.experimental.pallas import tpu as pltpu

F32 = jnp.float32
BF16 = jnp.bfloat16
HIGHEST = lax.Precision.HIGHEST

LANES = 128
SUBLANES = 8
VMEM_LIMIT_BYTES = 56 * 1024 * 1024

EPS = 1e-6
LN_EPS = 1e-5
MLSTM_CHUNK = 256
MLSTM_MIN_CHUNK = 32
MOBA_BLOCK = 256
MOBA_TOPK = 3
REL_BUCKETS = 32
REL_MAX_DIST = 128
NEG = -1e30
LOG2E = math.log2(math.e)
CONV_HALO = 32

_NT = (((1,), (1,)), ((), ()))
_TN = (((0,), (0,)), ((), ()))


def _cparams(n_axes):
    return pltpu.CompilerParams(dimension_semantics=("arbitrary",) * n_axes, vmem_limit_bytes=VMEM_LIMIT_BYTES)


def _sigmoid(x):
    return jax.nn.sigmoid(x)


def _col_tile(n, pref):
    tile = min(pref, n)
    while n % tile:
        tile -= LANES
    return tile


def _row_tiling(bn, t, tm):
    if t >= tm:
        assert t % tm == 0
        tpb = t // tm
        return 1, tm, bn * tpb, tpb
    gb = min(bn, tm // t)
    assert bn % gb == 0 and t % SUBLANES == 0
    return gb, t, bn // gb, 1


def _mod_kernel(c_ref, w_ref, b_ref, o_ref):
    c = c_ref[...]
    s = (c * _sigmoid(c)).astype(BF16)
    mod = jnp.dot(s, w_ref[...].astype(BF16), preferred_element_type=F32) + b_ref[...]
    o_ref[...] = mod.reshape(o_ref.shape)


def _adaln_mods(c_all, ada_w, ada_b, tn=512):
    depth, two, d, n3 = ada_w.shape
    nlj = depth * two
    n_parts = n3 // d
    rows = c_all.shape[0]
    tn = _col_tile(d, tn)
    npj = d // tn
    return pl.pallas_call(
        _mod_kernel,
        grid=(nlj, n3 // tn),
        in_specs=[pl.BlockSpec((rows, d), lambda l, j: (0, 0)),
                  pl.BlockSpec((None, d, tn), lambda l, j: (l, 0, j)),
                  pl.BlockSpec((None, 1, tn), lambda l, j: (l, 0, j))],
        out_specs=pl.BlockSpec((None, rows, 1, tn), lambda l, j: (l * n_parts + j // npj, 0, 0, j % npj)),
        out_shape=jax.ShapeDtypeStruct((nlj * n_parts, rows, 1, d), F32),
        compiler_params=_cparams(2), name="adaln_mods",
    )(c_all, ada_w.reshape(nlj, d, n3), ada_b.reshape(nlj, 1, n3))


def _mod_spec(mod, part, gb, tpb, width=None, by_col=False):
    mods, _, row0 = mod
    assert row0 % gb == 0
    width = width or mods.shape[3]
    return pl.BlockSpec((None, gb, 1, width), lambda i, j: (part, i // tpb + row0 // gb, 0, j if by_col else 0))


def _nm_matmul_kernel(x_ref, sh_ref, sc_ref, *rest, normed, narrow):
    n_s, n_g = len(normed), sum(normed)
    w_refs, g_refs, rest = rest[:n_s], rest[n_s:n_s + n_g], rest[n_s + n_g:]
    if narrow:
        wn_ref, rest = rest[0], rest[1:]
    o_refs, h_ref = rest[:n_s], rest[-1]

    @pl.when(pl.program_id(1) == 0)
    def _():
        x = x_ref[...]
        r = lax.rsqrt(jnp.mean(x * x, axis=-1, keepdims=True) + EPS)
        h = (x * r) * (1.0 + sc_ref[...]) + sh_ref[...]
        h_ref[...] = h.reshape(h_ref.shape).astype(BF16)
        if narrow:
            rest[n_s][...] = jnp.dot(h_ref[...], wn_ref[...], preferred_element_type=F32)

    h = h_ref[...]
    g_iter = iter(g_refs)
    for w_ref, o_ref, has_g in zip(w_refs, o_refs, normed):
        acc = jnp.dot(h, w_ref[...], preferred_element_type=F32)
        if not has_g:
            o_ref[...] = acc.astype(o_ref.dtype)
            continue
        g = next(g_iter)[...]
        for s in range(acc.shape[1] // LANES):
            y = acc[:, s * LANES:(s + 1) * LANES]
            y = y * lax.rsqrt(jnp.mean(y * y, axis=-1, keepdims=True) + EPS) * g
            o_ref[:, s * LANES:(s + 1) * LANES] = y.astype(o_ref.dtype)


def _nm_matmul(x3, mod, sections, width, *, narrow=None, tm=1024, tn=1024, name="nm_matmul"):
    bn, t, d = x3.shape
    gb, rows, n_i, tpb = _row_tiling(bn, t, tm)
    tmm = gb * rows
    n_s = len(sections)
    tn = _col_tile(math.gcd(width, *[sec[2] for sec in sections if sec[2]]), tn)
    n_j = width // tn
    gains = [sec[3].reshape(1, LANES) for sec in sections if sec[3] is not None]
    in_specs = ([pl.BlockSpec((gb, rows, d), lambda i, j: (i // tpb, i % tpb, 0)),
                 _mod_spec(mod, mod[1], gb, tpb), _mod_spec(mod, mod[1] + 1, gb, tpb)]
                + [pl.BlockSpec((None, d, tn), lambda i, j, l=l, j0=c0 // tn: (l, 0, j + j0))
                   for _, l, c0, _, _ in sections]
                + [pl.BlockSpec((1, LANES), lambda i, j: (0, 0))] * len(gains))
    out_specs = [pl.BlockSpec((tmm, tn), lambda i, j: (i, j))] * n_s
    out_shape = [jax.ShapeDtypeStruct((bn * t, width), sec[4]) for sec in sections]
    extra = []
    if narrow is not None:
        wn, ln = narrow
        in_specs.append(pl.BlockSpec((None, d, LANES), lambda i, j: (ln, 0, 0)))
        out_specs.append(pl.BlockSpec((tmm, LANES), lambda i, j: (i, 0)))
        out_shape.append(jax.ShapeDtypeStruct((bn * t, LANES), F32))
        extra = [wn]
    return pl.pallas_call(
        functools.partial(_nm_matmul_kernel, normed=[sec[3] is not None for sec in sections],
                          narrow=narrow is not None),
        grid=(n_i, n_j),
        in_specs=in_specs,
        out_specs=out_specs,
        out_shape=out_shape,
        scratch_shapes=[pltpu.VMEM((tmm, d), BF16)],
        compiler_params=_cparams(2), name=name,
    )(x3, mod[0], mod[0], *[sec[0] for sec in sections], *gains, *extra)


def _mm_res_kernel(*refs, n_in):
    a_refs, w_refs = refs[:n_in], refs[n_in:2 * n_in]
    x_ref, gate_ref, o_ref = refs[2 * n_in:]
    acc = None
    for a_ref, w_ref in zip(a_refs, w_refs):
        part = jnp.dot(a_ref[...].astype(BF16), w_ref[...], preferred_element_type=F32)
        acc = part if acc is None else acc + part
    x = x_ref[...]
    o_ref[...] = x + gate_ref[...] * acc.reshape(x.shape)


def _mm_res(a_list, w_list, x3, mod, *, tm=1024, tn=1024, name="mm_res"):
    bn, t, d = x3.shape
    gb, rows, n_i, tpb = _row_tiling(bn, t, tm)
    tmm = gb * rows
    tn = _col_tile(d, tn)
    n_in = len(a_list)
    in_specs = ([pl.BlockSpec((tmm, a.shape[1]), lambda i, j: (i, 0)) for a in a_list]
                + [pl.BlockSpec((None, a.shape[1], tn), lambda i, j, l=l, r=r0 // a.shape[1]: (l, r, j))
                   for a, (_, l, r0) in zip(a_list, w_list)]
                + [pl.BlockSpec((gb, rows, tn), lambda i, j: (i // tpb, i % tpb, j)),
                   _mod_spec(mod, mod[1] + 2, gb, tpb, width=tn, by_col=True)])
    return pl.pallas_call(
        functools.partial(_mm_res_kernel, n_in=n_in),
        grid=(n_i, d // tn),
        in_specs=in_specs,
        out_specs=pl.BlockSpec((gb, rows, tn), lambda i, j: (i // tpb, i % tpb, j)),
        out_shape=jax.ShapeDtypeStruct((bn, t, d), F32),
        compiler_params=_cparams(2), name=name,
    )(*a_list, *[w for w, _, _ in w_list], x3, mod[0])


def _mlstm_kernel(q_ref, k_ref, v_ref, zo_ref, g_ref, bg_ref, gn_ref, c0_ref, n0_ref, m0_ref,
                  h_ref, co_ref, no_ref, mo_ref, c_s, n_s, m_s, *, tb, chunk):
    ck = pl.program_id(1)
    L = chunk
    n_heads, dk, dv = c_s.shape

    @pl.when(ck == 0)
    def _():
        c_s[...] = c0_ref[0]
        n_s[...] = n0_ref[0]
        m_s[...] = m0_ref[0]

    def pad_rows(x):
        if tb == L:
            return x
        return jnp.concatenate([x, jnp.zeros((L - tb, x.shape[1]), x.dtype)], axis=0)

    gates = pad_rows(g_ref[0] + bg_ref[...])
    lane = lax.broadcasted_iota(jnp.int32, (L, LANES), 1)
    li_all = jnp.where(lane < n_heads, gates, 0.0)
    lf_all = jnp.where((lane >= n_heads) & (lane < 2 * n_heads),
                       -(jnp.maximum(-gates, 0.0) + jnp.log1p(jnp.exp(-jnp.abs(gates)))), 0.0)
    if tb != L:
        real = lax.broadcasted_iota(jnp.int32, (L, LANES), 0) < tb
        li_all = jnp.where(real | (lane >= n_heads), li_all, NEG)
        lf_all = jnp.where(real, lf_all, 0.0)
    ri = lax.broadcasted_iota(jnp.int32, (L, L), 0)
    ci = lax.broadcasted_iota(jnp.int32, (L, L), 1)
    tri = ri >= ci
    y = li_all + jnp.dot(tri.astype(F32), lf_all, precision=HIGHEST, preferred_element_type=F32)
    er = lax.broadcasted_iota(jnp.int32, (SUBLANES, LANES), 0)
    el = lax.broadcasted_iota(jnp.int32, (SUBLANES, LANES), 1)
    e = jnp.where(er < n_heads, jnp.where(el == er + n_heads, 1.0, 0.0) - jnp.where(el == er, 1.0, 0.0), 0.0)
    rt = lax.dot_general(e, y, _NT, precision=HIGHEST, preferred_element_type=F32)

    for hd in range(n_heads):
        li = y[:, hd:hd + 1]
        b = y[:, n_heads + hd:n_heads + hd + 1]
        r_row = rt[hd:hd + 1, :]
        q = pad_rows(q_ref[0, :, hd * dk:(hd + 1) * dk]).astype(F32) * (dk ** -0.5)
        k = pad_rows(k_ref[0, :, hd * dk:(hd + 1) * dk]).astype(F32)
        v = pad_rows(v_ref[0, :, hd * dv:(hd + 1) * dv]).astype(F32)
        a = b + m_s[hd][:, :1]
        dmat = jnp.where(tri, b - r_row, NEG)
        mt = jnp.maximum(a, jnp.max(dmat, axis=-1, keepdims=True))
        w_inter = jnp.exp(a - mt)
        qb, kb, vb = q.astype(BF16), k.astype(BF16), v.astype(BF16)
        s = lax.dot_general(qb, kb, _NT, preferred_element_type=F32) * jnp.exp(dmat - mt)
        c_old = c_s[hd]
        n_old = n_s[hd]
        num = (w_inter * jnp.dot(qb, c_old.astype(BF16), preferred_element_type=F32)
               + jnp.dot(s.astype(BF16), vb, preferred_element_type=F32))
        den = w_inter * jnp.sum(q * n_old, axis=-1, keepdims=True) + jnp.sum(s, axis=-1, keepdims=True)
        hh = (num / jnp.maximum(jnp.abs(den), jnp.exp(-mt)))[:tb]
        hn = hh * lax.rsqrt(jnp.mean(hh * hh, axis=-1, keepdims=True) + EPS)
        h_ref[0, :, hd * dv:(hd + 1) * dv] = (hn * gn_ref[:, hd * dv:(hd + 1) * dv]
                                              * _sigmoid(zo_ref[0, :, hd * dv:(hd + 1) * dv])).astype(h_ref.dtype)

        m_l = mt[L - 1:L]
        g_inter = jnp.exp(a[L - 1:L] - m_l)
        g_s = jnp.exp(b[L - 1:L] - b + li - m_l)
        c_s[hd] = g_inter * c_old + lax.dot_general(kb, (g_s * v).astype(BF16), _TN, preferred_element_type=F32)
        n_s[hd] = g_inter * n_old + jnp.sum(g_s * k, axis=0, keepdims=True)
        m_s[hd] = jnp.broadcast_to(m_l, (1, LANES))

    @pl.when(ck == pl.num_programs(1) - 1)
    def _():
        co_ref[0] = c_s[...]
        no_ref[0] = n_s[...]
        mo_ref[0] = m_s[...]


def _mlstm(q3, k3, v3, o3, g3, bg, gn, c0, n0, m0, *, out_dtype, name):
    bn, t, _ = q3.shape
    _, n_heads, dk, dv = c0.shape
    assert dk == dv and dk % LANES == 0 and 2 * n_heads <= LANES and n_heads <= SUBLANES
    if t % MLSTM_CHUNK == 0:
        tb = chunk = MLSTM_CHUNK
    else:
        assert t <= MLSTM_MIN_CHUNK
        tb, chunk = t, MLSTM_MIN_CHUNK
    n_c = t // tb
    hw = n_heads * dk
    tok = lambda b, c: (b, c, 0)
    state = lambda b, c: (b, 0, 0, 0)
    h, c, n, m = pl.pallas_call(
        functools.partial(_mlstm_kernel, tb=tb, chunk=chunk),
        grid=(bn, n_c),
        in_specs=[pl.BlockSpec((1, tb, hw), tok)] * 4 + [
                  pl.BlockSpec((1, tb, LANES), tok),
                  pl.BlockSpec((1, LANES), lambda b, c: (0, 0)),
                  pl.BlockSpec((1, hw), lambda b, c: (0, 0)),
                  pl.BlockSpec((1, n_heads, dk, dv), state), pl.BlockSpec((1, n_heads, 1, dk), state),
                  pl.BlockSpec((1, n_heads, 1, LANES), state)],
        out_specs=[pl.BlockSpec((1, tb, hw), tok),
                   pl.BlockSpec((1, n_heads, dk, dv), state), pl.BlockSpec((1, n_heads, 1, dk), state),
                   pl.BlockSpec((1, n_heads, 1, LANES), state)],
        out_shape=[jax.ShapeDtypeStruct((bn, t, hw), out_dtype),
                   jax.ShapeDtypeStruct((bn, n_heads, dk, dv), F32),
                   jax.ShapeDtypeStruct((bn, n_heads, 1, dk), F32),
                   jax.ShapeDtypeStruct((bn, n_heads, 1, LANES), F32)],
        scratch_shapes=[pltpu.VMEM((n_heads, dk, dv), F32), pltpu.VMEM((n_heads, 1, dk), F32),
                        pltpu.VMEM((n_heads, 1, LANES), F32)],
        compiler_params=_cparams(2), name=name,
    )(q3, k3, v3, o3, g3, bg, gn, c0, n0.reshape(bn, n_heads, 1, dk),
      jnp.broadcast_to(m0[:, :, None, None], (bn, n_heads, 1, LANES)))
    return h, c, n.reshape(bn, n_heads, dk), m[:, :, 0, 0]


def _convmod_kernel(a_ref, g_ref, buf_ref, w_ref, b_ref, lg_ref, lb_ref, y_ref, nb_ref, u_s, *, tt, width, rc):
    t = pl.program_id(1)
    hist = width - 1
    bias = b_ref[...]
    lg = lg_ref[...]
    lb = lb_ref[...]
    n_p = rc // SUBLANES
    for bi in range(a_ref.shape[0]):
        @pl.when(t == 0)
        def _():
            buf = buf_ref[bi]
            for r in range(SUBLANES):
                u_s[bi, r, 0:hist - r, :] = buf[r:, :]

        @pl.when(t > 0)
        def _():
            for r in range(SUBLANES):
                u_s[bi, r, 0:CONV_HALO, :] = u_s[bi, r, tt:tt + CONV_HALO, :]

        u = a_ref[bi] * _sigmoid(g_ref[bi])
        for r in range(SUBLANES):
            u_s[bi, r, hist - r:hist - r + tt, :] = u
        for r0 in range(0, tt, rc):
            accs = [jnp.broadcast_to(bias, (SUBLANES, bias.shape[1]))] * n_p
            for j in range(width):
                r = j % SUBLANES
                wj = w_ref[j]
                accs = [acc + wj * u_s[bi, r, r0 + p * SUBLANES + j - r:r0 + (p + 1) * SUBLANES + j - r, :]
                        for p, acc in enumerate(accs)]
            acc = accs[0] if n_p == 1 else jnp.concatenate(accs, axis=0)
            mu = jnp.mean(acc, axis=-1, keepdims=True)
            xc = acc - mu
            var = jnp.mean(xc * xc, axis=-1, keepdims=True)
            y = xc * lax.rsqrt(var + LN_EPS) * lg + lb
            y_ref[bi, r0:r0 + rc, :] = (y * _sigmoid(y)).astype(y_ref.dtype)

        @pl.when(t == pl.num_programs(1) - 1)
        def _():
            nb_ref[bi] = u_s[bi, 0, tt:tt + hist, :]


def _conv_module(a3, g3, cbuf, conv_w, conv_b, ln_g, ln_b, *, out_dtype, name):
    bn, t, _ = a3.shape
    width, ch = conv_w.shape
    assert width - 1 <= CONV_HALO
    tt = 128 if t % 128 == 0 else t
    n_t = t // tt
    assert n_t == 1 or tt >= CONV_HALO
    gb = 1 if n_t > 1 else math.gcd(bn, SUBLANES)
    y, nb = pl.pallas_call(
        functools.partial(_convmod_kernel, tt=tt, width=width,
                          rc=4 * SUBLANES if tt % (4 * SUBLANES) == 0 else SUBLANES),
        grid=(bn // gb, n_t),
        in_specs=[pl.BlockSpec((gb, tt, ch), lambda b, i: (b, i, 0)),
                  pl.BlockSpec((gb, tt, ch), lambda b, i: (b, i, 0)),
                  pl.BlockSpec((gb, width - 1, ch), lambda b, i: (b, 0, 0)),
                  pl.BlockSpec((width, SUBLANES, ch), lambda b, i: (0, 0, 0)),
                  pl.BlockSpec((1, ch), lambda b, i: (0, 0)),
                  pl.BlockSpec((1, ch), lambda b, i: (0, 0)),
                  pl.BlockSpec((1, ch), lambda b, i: (0, 0))],
        out_specs=[pl.BlockSpec((gb, tt, ch), lambda b, i: (b, i, 0)),
                   pl.BlockSpec((gb, width - 1, ch), lambda b, i: (b, 0, 0))],
        out_shape=[jax.ShapeDtypeStruct((bn, t, ch), out_dtype),
                   jax.ShapeDtypeStruct((bn, width - 1, ch), F32)],
        scratch_shapes=[pltpu.VMEM((gb, SUBLANES, CONV_HALO + tt, ch), F32)],
        compiler_params=_cparams(2), name=name,
    )(a3, g3, cbuf, jnp.broadcast_to(conv_w[:, None, :], (width, SUBLANES, ch)), conv_b.reshape(1, ch),
      ln_g.reshape(1, ch), ln_b.reshape(1, ch))
    return y, nb


def _ffn_up_kernel(x_ref, sh_ref, sc_ref, wg_ref, wu_ref, buf_ref, cw_ref, cb_ref, act_ref, gl_ref, h_s, carry_s,
                   *, tpb, rows, strip):
    ti = pl.program_id(0) % tpb
    j = pl.program_id(1)

    @pl.when(j == 0)
    def _():
        x = x_ref[...]
        r = lax.rsqrt(jnp.mean(x * x, axis=-1, keepdims=True) + EPS)
        h = (x * r) * (1.0 + sc_ref[...]) + sh_ref[...]
        h_s[...] = h.reshape(h_s.shape).astype(BF16)

    hist = buf_ref.shape[1]
    if tpb > 1:
        @pl.when(ti == 0)
        def _():
            carry_s[j] = jnp.zeros(carry_s.shape[1:], F32)
            carry_s[j, 0:hist, :] = buf_ref[0]

    h = h_s[...]
    n = h.shape[0]
    gb = n // rows
    tn = wg_ref.shape[1]
    for c0 in range(0, tn, strip):
        cs = slice(c0, c0 + strip)
        g = jnp.dot(h, wg_ref[:, cs], preferred_element_type=F32)
        u = jnp.dot(h, wu_ref[:, cs], preferred_element_type=F32)
        if tpb > 1:
            prev = carry_s[j, :, cs]
            carry_s[j, :, cs] = pltpu.roll(g[n - SUBLANES:], 2, axis=0)
        r8 = lax.broadcasted_iota(jnp.int32, (gb * SUBLANES, strip), 0) % SUBLANES
        if tpb == 1:
            rows01 = [jnp.broadcast_to(buf_ref[:, i:i + 1, cs], (gb, SUBLANES, strip)).reshape(gb * SUBLANES, strip)
                      for i in range(hist)]
            prev = jnp.where(r8 == 0, rows01[0], rows01[1])
        prev1 = pltpu.roll(prev, gb * SUBLANES - 1, axis=0)
        gm1 = pltpu.roll(g, 1, axis=0)
        gm2 = pltpu.roll(g, 2, axis=0)
        if rows == SUBLANES:
            gm1 = jnp.where(r8 < 1, prev1, gm1)
            gm2 = jnp.where(r8 < 2, prev, gm2)
            gl_ref[:, :, cs] = g.reshape(gb, SUBLANES, strip)
        else:
            gm1 = jnp.concatenate([jnp.where(r8 < 1, prev1, gm1[:SUBLANES]), gm1[SUBLANES:]], axis=0)
            gm2 = jnp.concatenate([jnp.where(r8 < 2, prev, gm2[:SUBLANES]), gm2[SUBLANES:]], axis=0)
            gl_ref[:, :, cs] = g[n - SUBLANES:].reshape(gb, SUBLANES, strip)
        gc = cw_ref[0:1, cs] * gm2 + cw_ref[1:2, cs] * gm1 + cw_ref[2:3, cs] * g + cb_ref[:, cs]
        act_ref[:, cs] = ((gc * _sigmoid(gc)) * u).astype(act_ref.dtype)


def _ffn_up(x3, mod, w, layer, buf, conv_w, conv_b, *, tm=1024, tn=512, name="ffn_up"):
    bn, t, d = x3.shape
    dff = w.shape[2] // 2
    gb, rows, n_i, tpb = _row_tiling(bn, t, tm)
    tmm = gb * rows
    tn = _col_tile(dff, tn)
    n_f = dff // tn
    hist = buf.shape[1]
    assert conv_w.shape[0] == 3 and hist == 2 and (tpb == 1 or gb == 1)
    act, g_tail = pl.pallas_call(
        functools.partial(_ffn_up_kernel, tpb=tpb, rows=rows, strip=min(tn, 256)),
        grid=(n_i, n_f),
        in_specs=[pl.BlockSpec((gb, rows, d), lambda i, j: (i // tpb, i % tpb, 0)),
                  _mod_spec(mod, mod[1], gb, tpb), _mod_spec(mod, mod[1] + 1, gb, tpb),
                  pl.BlockSpec((None, d, tn), lambda i, j: (layer, 0, j)),
                  pl.BlockSpec((None, d, tn), lambda i, j: (layer, 0, j + n_f)),
                  pl.BlockSpec((gb, hist, tn), lambda i, j: (i // tpb, 0, j)),
                  pl.BlockSpec((3, tn), lambda i, j: (0, j)),
                  pl.BlockSpec((1, tn), lambda i, j: (0, j))],
        out_specs=[pl.BlockSpec((tmm, tn), lambda i, j: (i, j)),
                   pl.BlockSpec((gb, SUBLANES, tn), lambda i, j: (i, 0, j))],
        out_shape=[jax.ShapeDtypeStruct((bn * t, dff), BF16),
                   jax.ShapeDtypeStruct((n_i * gb, SUBLANES, dff), F32)],
        scratch_shapes=[pltpu.VMEM((tmm, d), BF16), pltpu.VMEM((n_f, SUBLANES, tn), F32)],
        compiler_params=_cparams(2), name=name,
    )(x3, mod[0], mod[0], w, w, buf, conv_w, conv_b.reshape(1, dff))
    return act, g_tail.reshape(bn, tpb, SUBLANES, dff)[:, tpb - 1]


def _t5_bucket(dist):
    n = jnp.maximum(dist, 0)
    exact = REL_BUCKETS // 2
    large = exact + (jnp.log(jnp.maximum(n, 1).astype(F32) / exact)
                     / math.log(REL_MAX_DIST / exact) * (REL_BUCKETS - exact)).astype(jnp.int32)
    return jnp.where(n < exact, n, jnp.minimum(large, REL_BUCKETS - 1))


def _bias_lookup(dist, table_col):
    bucket = _t5_bucket(dist)
    bias = jnp.zeros(dist.shape, F32)
    for kk in range(REL_BUCKETS):
        bias = jnp.where(bucket == kk, table_col(kk), bias)
    return bias * LOG2E


def _bias_prompt_kernel(rb_ref, o_ref, *, blk):
    rb = rb_ref[0]
    key = lax.broadcasted_iota(jnp.int32, (blk, blk), 0)
    qry = lax.broadcasted_iota(jnp.int32, (blk, blk), 1)
    col = lambda kk: rb[:, kk:kk + 1]
    own = _bias_lookup(qry - key, col)
    o_ref[0, 0] = jnp.where(qry >= key, own, NEG)
    o_ref[0, 1] = _bias_lookup(blk + qry - key, col)


def _bias_prompt(rel_bias, blk):
    n_b, n_h = rel_bias.shape
    rbt = jnp.pad(rel_bias.T, ((0, 0), (0, LANES - n_b))).reshape(n_h, 1, LANES)
    return pl.pallas_call(
        functools.partial(_bias_prompt_kernel, blk=blk),
        grid=(n_h,),
        in_specs=[pl.BlockSpec((1, 1, LANES), lambda h: (h, 0, 0))],
        out_specs=pl.BlockSpec((1, 2, blk, blk), lambda h: (h, 0, 0, 0)),
        out_shape=jax.ShapeDtypeStruct((n_h, 2, blk, blk), F32),
        compiler_params=_cparams(1), name="bias_prompt",
    )(rbt), rbt


def _bias_sample_kernel(rb_ref, fl_ref, new_ref, *, n_heads, t, page):
    rb = rb_ref[...]
    col = lambda kk: rb[:, kk:kk + 1]
    for ref, base in ((fl_ref.at[0], 2 * page), (fl_ref.at[1], page), (new_ref, 0)):
        shape = ref.shape
        r = lax.broadcasted_iota(jnp.int32, shape, 0)
        c = lax.broadcasted_iota(jnp.int32, shape, 1)
        if ref is new_ref:
            key_head, key_off = c // t, c % t
        else:
            key_head, key_off = c % n_heads, c // n_heads
        dist = base + r % t - key_off
        ok = (r // t == key_head) & (dist >= 0)
        ref[...] = jnp.where(ok, _bias_lookup(dist, col), NEG)


def _bias_sample(rel_bias, t, page):
    n_b, n_h = rel_bias.shape
    rows = t * n_h
    rb = jnp.pad(jnp.repeat(rel_bias.T, t, axis=0), ((0, 0), (0, LANES - n_b)))
    return pl.pallas_call(
        functools.partial(_bias_sample_kernel, n_heads=n_h, t=t, page=page),
        out_shape=[jax.ShapeDtypeStruct((2, rows, page * n_h), F32),
                   jax.ShapeDtypeStruct((rows, rows), F32)],
        compiler_params=pltpu.CompilerParams(vmem_limit_bytes=VMEM_LIMIT_BYTES), name="bias_sample",
    )(rb)


def _attn_prompt_kernel(q_ref, k_ref, v_ref, bias_ref, rb_ref, o_ref, km_s, kb_s, vt_s, s_s, *, blk, n_blk, scale):
    km_s[...] = jnp.zeros_like(km_s)
    for j in range(n_blk):
        kj = k_ref[0, j * blk:(j + 1) * blk, :]
        km_s[j:j + 1, :] = jnp.mean(kj, axis=0, keepdims=True)
        kb_s[j * blk:(j + 1) * blk, :] = kj.astype(BF16)
        vt_s[:, j * blk:(j + 1) * blk] = v_ref[0, j * blk:(j + 1) * blk, :].T.astype(BF16)

    far = rb_ref[0][:, REL_BUCKETS - 1:REL_BUCKETS] * LOG2E
    km = km_s[...]
    row = lax.broadcasted_iota(jnp.int32, (km.shape[0], blk), 0)
    tile = 0
    for c in range(n_blk):
        qf = q_ref[0, c * blk:(c + 1) * blk, :]
        qb = qf.astype(BF16)
        add = [0.0] * c
        if c > MOBA_TOPK:
            sc = lax.dot_general(km, qf, _NT, precision=HIGHEST, preferred_element_type=F32)
            sc = jnp.where(row < c, sc, -jnp.inf)
            for j in range(c):
                sj = sc[j:j + 1, :]
                ahead = (sc > sj) | ((sc == sj) & (row < j))
                rank = jnp.sum(jnp.where(ahead, 1.0, 0.0), axis=0, keepdims=True)
                add[j] = jnp.where(rank < MOBA_TOPK, 0.0, NEG)
        m = None
        for j in range(c + 1):
            st = lax.dot_general(kb_s[j * blk:(j + 1) * blk, :], qb, _NT, preferred_element_type=F32) * (scale * LOG2E)
            if j == c:
                st = st + bias_ref[0, 0]
            elif j == c - 1:
                st = st + (bias_ref[0, 1] + add[j])
            else:
                st = st + (add[j] + far)
            s_s[tile + j] = st
            mj = jnp.max(st, axis=0, keepdims=True)
            m = mj if m is None else jnp.maximum(m, mj)
        l = jnp.zeros_like(m)
        acc = jnp.zeros((vt_s.shape[0], blk), F32)
        for j in range(c + 1):
            p = jnp.exp2(s_s[tile + j] - m)
            l = l + jnp.sum(p, axis=0, keepdims=True)
            acc = acc + jnp.dot(vt_s[:, j * blk:(j + 1) * blk], p.astype(BF16), preferred_element_type=F32)
        o_ref[0, c * blk:(c + 1) * blk, :] = (acc / l).T.astype(o_ref.dtype)
        tile += c + 1


def _attn_prompt(q3, k3, v3, bias_tiles, rbt, *, out_dtype):
    bn, s, _ = q3.shape
    n_h = bias_tiles.shape[0]
    dh = q3.shape[2] // n_h
    blk = MOBA_BLOCK
    assert s % blk == 0 and dh == LANES and blk >= REL_MAX_DIST
    n_blk = s // blk
    head = lambda b, h: (b, 0, h)
    return pl.pallas_call(
        functools.partial(_attn_prompt_kernel, blk=blk, n_blk=n_blk, scale=dh ** -0.5),
        grid=(bn, n_h),
        in_specs=[pl.BlockSpec((1, s, dh), head), pl.BlockSpec((1, s, dh), head), pl.BlockSpec((1, s, dh), head),
                  pl.BlockSpec((1, 2, blk, blk), lambda b, h: (h, 0, 0, 0)),
                  pl.BlockSpec((1, 1, LANES), lambda b, h: (h, 0, 0))],
        out_specs=pl.BlockSpec((1, s, dh), head),
        out_shape=jax.ShapeDtypeStruct(q3.shape, out_dtype),
        scratch_shapes=[pltpu.VMEM((-(-n_blk // SUBLANES) * SUBLANES, dh), F32), pltpu.VMEM((s, dh), BF16),
                        pltpu.VMEM((dh, s), BF16), pltpu.VMEM((n_blk * (n_blk + 1) // 2, blk, blk), F32)],
        compiler_params=_cparams(2), name="moba_prompt",
    )(q3, k3, v3, bias_tiles, rbt)


def _attn_sample_kernel(pt_ref, q_ref, kn_ref, vn_ref, *refs, n_heads, ppb, pps, scale):
    del pt_ref
    kp_refs, vp_refs = refs[:pps], refs[pps:2 * pps]
    bias_ref, new_ref, o_ref, ks_s, m_s, l_s, o_s = refs[2 * pps:]
    g = pl.program_id(1)
    last = pl.num_programs(1) - 1
    n_tok = q_ref.shape[1]
    dh = q_ref.shape[2] // n_heads

    def head_major(ref):
        return jnp.concatenate([ref[0, :, h * dh:(h + 1) * dh] for h in range(n_heads)], axis=0)

    qf = head_major(q_ref)
    qb = qf.astype(BF16)
    rows = n_heads * n_tok

    def partial_softmax(kk, vv, bias):
        s = lax.dot_general(qb, kk.astype(BF16), _NT, preferred_element_type=F32) * (scale * LOG2E) + bias
        m = jnp.max(s, axis=-1, keepdims=True)
        e = jnp.exp2(s - m)
        return m, jnp.sum(e, axis=-1, keepdims=True), jnp.dot(e.astype(BF16), vv.astype(BF16),
                                                              preferred_element_type=F32)

    for r in range(pps):
        pg = g * pps + r
        kp = kp_refs[r][0]
        ks_s[pg] = jnp.sum(kp.reshape(kp.shape[0] // n_heads, n_heads, dh), axis=0)
        bias = bias_ref[jnp.where(g == last, 1, 0)] if r == pps - 1 else bias_ref[0]
        m, l, o = partial_softmax(kp, vp_refs[r][0], bias)
        m_s[pg] = m
        l_s[pg] = l
        o_s[pg] = o

    @pl.when(g == last)
    def _():
        n_pg = m_s.shape[0]
        n_blk = n_pg // ppb
        inv = 1.0 / (ppb * (kp_refs[0].shape[1] // n_heads))
        scs = []
        for j in range(n_blk):
            km = ks_s[j * ppb]
            for pp in range(1, ppb):
                km = km + ks_s[j * ppb + pp]
            km = jnp.broadcast_to((km * inv)[:, None, :], (n_heads, n_tok, dh)).reshape(rows, dh)
            scs.append(jnp.sum(qf * km, axis=-1, keepdims=True))
        sel = []
        for j in range(n_blk):
            rank = jnp.zeros((rows, 1), F32)
            for jj in range(n_blk):
                if jj != j:
                    ahead = (scs[jj] >= scs[j]) if jj < j else (scs[jj] > scs[j])
                    rank = rank + jnp.where(ahead, 1.0, 0.0)
            sel.append(rank < MOBA_TOPK)
        m_n, l_n, o_n = partial_softmax(head_major(kn_ref), head_major(vn_ref), new_ref[...])
        m_all = m_n
        for pg in range(n_pg):
            m_all = jnp.maximum(m_all, jnp.where(sel[pg // ppb], m_s[pg], NEG))
        w_n = jnp.exp2(m_n - m_all)
        l_all = w_n * l_n
        o_all = w_n * o_n
        for pg in range(n_pg):
            w = jnp.where(sel[pg // ppb], jnp.exp2(m_s[pg] - m_all), 0.0)
            l_all = l_all + w * l_s[pg]
            o_all = o_all + w * o_s[pg]
        o = (o_all / l_all).astype(o_ref.dtype)
        for h in range(n_heads):
            o_ref[0, :, h * dh:(h + 1) * dh] = o[h * n_tok:(h + 1) * n_tok]


def _attn_sample(q, k, v, cache_k, cache_v, page_table, far_last, new, *, out_dtype, pps=8):
    db, t, hd = q.shape
    n_pool, page, n_h, dh = cache_k.shape
    n_pages = page_table.shape[1]
    rows = t * n_h
    assert MOBA_BLOCK % page == 0 and (n_pages * page) % MOBA_BLOCK == 0 and t <= MOBA_BLOCK
    assert page >= REL_MAX_DIST and n_pages * page // MOBA_BLOCK > MOBA_TOPK and dh == LANES
    assert n_pages % pps == 0
    ppb = MOBA_BLOCK // page
    seq = lambda b, g, pt: (b, 0, 0)
    paged = [(lambda b, g, pt, r=r: (pt[b, g * pps + r], 0, 0)) for r in range(pps)]
    page_spec = lambda r: pl.BlockSpec((1, page * n_h, dh), paged[r])
    grid_spec = pltpu.PrefetchScalarGridSpec(
        num_scalar_prefetch=1,
        grid=(db, n_pages // pps),
        in_specs=([pl.BlockSpec((1, t, hd), seq)] * 3 + [page_spec(r) for r in range(pps)] * 2
                  + [pl.BlockSpec(far_last.shape, lambda b, g, pt: (0, 0, 0)),
                     pl.BlockSpec(new.shape, lambda b, g, pt: (0, 0))]),
        out_specs=pl.BlockSpec((1, t, hd), seq),
        scratch_shapes=[pltpu.VMEM((n_pages, n_h, dh), F32), pltpu.VMEM((n_pages, rows, 1), F32),
                        pltpu.VMEM((n_pages, rows, 1), F32), pltpu.VMEM((n_pages, rows, dh), F32)],
    )
    ck = cache_k.reshape(n_pool, page * n_h, dh)
    cv = cache_v.reshape(n_pool, page * n_h, dh)
    return pl.pallas_call(
        functools.partial(_attn_sample_kernel, n_heads=n_h, ppb=ppb, pps=pps, scale=dh ** -0.5),
        grid_spec=grid_spec,
        out_shape=jax.ShapeDtypeStruct((db, t, hd), out_dtype),
        compiler_params=_cparams(2), name="moba_sample",
    )(page_table, q, k, v, *([ck] * pps), *([cv] * pps), far_last, new)


def _trunk(x, mods, row0, mc, mn, mm, cb, fb, wts, attend):
    bn, t, d = x.shape
    depth = wts["ffn_up"].shape[0]
    n_ha = mc.shape[2]
    qkw = n_ha * mc.shape[3]
    aw = n_ha * mc.shape[4]
    bw = wts["conv_w"].shape[2]
    big = t >= 512
    act_dtype = BF16 if big else F32
    out_c, out_n, out_m, out_conv, out_k, out_v, out_ffn = [], [], [], [], [], [], []
    for l in range(depth):
        mod = (mods, 3 * (2 * l), row0)
        if l % 2 == 0:
            e = l // 2
            assert qkw == aw == bw
            w_a, w_b = wts["in_even"], wts["in_even_b"]
            secs = ([(w_a, e, i * qkw, None, act_dtype) for i in range(3)] + [(w_a, e, 3 * qkw, None, F32)]
                    + [(w_b, e, i * bw, None, F32) for i in range(2)])
            *zs, g = _nm_matmul(x, mod, secs, qkw, narrow=(wts["in_even_gate"], e), tn=256 if big else 128,
                                name="in_even")
            zq, zk, zv, zo, za, zg = (z.reshape(bn, t, qkw) for z in zs)
            h_a, c_new, n_new, m_new = _mlstm(zq, zk, zv, zo, g.reshape(bn, t, LANES), wts["b_gates"][e],
                                              wts["mlstm_norm_g"][e], mc[e], mn[e], mm[e], out_dtype=act_dtype,
                                              name="mlstm")
            y_b, cb_new = _conv_module(za, zg, cb[e], wts["conv_w"][e], wts["conv_b"][e],
                                       wts["conv_ln_g"][e], wts["conv_ln_b"][e], out_dtype=act_dtype, name="conv_module")
            x = _mm_res([h_a.reshape(bn * t, aw), y_b.reshape(bn * t, bw)],
                        [(wts["out_even"], e, 0), (wts["out_even"], e, aw)], x, mod, name="out_even")
            out_c.append(c_new); out_n.append(n_new); out_m.append(m_new); out_conv.append(cb_new)
        else:
            o = l // 2
            w = wts["in_odd"]
            qkv = _nm_matmul(x, mod, [(w, o, 0, wts["q_norm_g"][o], F32), (w, o, d, wts["k_norm_g"][o], F32),
                                      (w, o, 2 * d, None, F32)], d, tn=256, name="in_odd")
            q, k, v = (a.reshape(bn, t, d) for a in qkv)
            a = attend(o, q, k, v, act_dtype)
            x = _mm_res([a.reshape(bn * t, d)], [(wts["out_odd"], o, 0)], x, mod, name="out_odd")
            out_k.append(k); out_v.append(v)
        mod = (mods, 3 * (2 * l + 1), row0)
        hist = fb[l].shape[1]
        assert hist <= min(t, SUBLANES)
        act, g_last = _ffn_up(x, mod, wts["ffn_up"], l, fb[l], wts["ffn_conv_w"][l], wts["ffn_conv_b"][l])
        x = _mm_res([act], [(wts["ffn_down"], l, 0)], x, mod, tn=512, name="ffn_down")
        out_ffn.append(g_last[:, SUBLANES - hist:])
    return (x, jnp.stack(out_c), jnp.stack(out_n), jnp.stack(out_m), jnp.stack(out_conv),
            jnp.stack(out_k), jnp.stack(out_v), jnp.stack(out_ffn))


def kernel(x_prompt, x_sample, state_mlstm_c, state_mlstm_n, state_mlstm_m, state_conv, cache_k, cache_v, page_table, state_ffn, c_prompt, c_sample, ada_w, ada_b, w_in_even, b_gates, mlstm_norm_g, conv_w, conv_b, conv_ln_g, conv_ln_b, w_out_even, w_in_odd, q_norm_g, k_norm_g, rel_bias, w_out_odd, ffn_w_up, ffn_conv_w, ffn_conv_b, ffn_w_down):
    bsz, seq, d = x_prompt.shape
    db, dt, _ = x_sample.shape
    depth = ada_w.shape[0]
    n_even, _, n_ha, dk, dv = state_mlstm_c.shape
    n_hc, dh = cache_k.shape[3], cache_k.shape[4]
    qkw, aw, bw = n_ha * dk, n_ha * dv, conv_w.shape[2]
    n_gate = 2 * n_ha
    o_gate = 2 * qkw + 2 * aw
    dt_x = x_prompt.dtype

    wts = {
        "in_even": w_in_even.astype(BF16),
        "in_even_b": w_in_even[:, :, o_gate + n_gate:].astype(BF16),
        "in_even_gate": jnp.pad(w_in_even[:, :, o_gate:o_gate + n_gate],
                                ((0, 0), (0, 0), (0, LANES - n_gate))).astype(BF16),
        "b_gates": jnp.pad(b_gates, ((0, 0), (0, LANES - n_gate))).reshape(n_even, 1, LANES),
        "mlstm_norm_g": mlstm_norm_g.reshape(n_even, 1, aw),
        "conv_w": conv_w, "conv_b": conv_b, "conv_ln_g": conv_ln_g, "conv_ln_b": conv_ln_b,
        "out_even": w_out_even.astype(BF16),
        "in_odd": w_in_odd.astype(BF16), "q_norm_g": q_norm_g, "k_norm_g": k_norm_g,
        "out_odd": w_out_odd.astype(BF16),
        "ffn_up": ffn_w_up.astype(BF16), "ffn_conv_w": ffn_conv_w, "ffn_conv_b": ffn_conv_b,
        "ffn_down": ffn_w_down.astype(BF16),
    }

    mods = _adaln_mods(jnp.concatenate([c_sample, c_prompt], axis=0), ada_w, ada_b)

    bias_tiles, rbt = _bias_prompt(rel_bias, MOBA_BLOCK)
    far_last, new = _bias_sample(rel_bias, dt, cache_k.shape[2])

    def prompt_attend(o, q, k, v, out_dtype):
        return _attn_prompt(q, k, v, bias_tiles, rbt, out_dtype=out_dtype)

    def sample_attend(o, q, k, v, out_dtype):
        return _attn_sample(q, k, v, cache_k[o], cache_v[o], page_table, far_last, new, out_dtype=out_dtype)

    zc = jnp.zeros((n_even, bsz, n_ha, dk, dv), dt_x)
    zn = jnp.zeros((n_even, bsz, n_ha, dk), dt_x)
    zm = jnp.zeros((n_even, bsz, n_ha), dt_x)
    zconv = jnp.zeros((n_even, bsz) + state_conv.shape[2:], dt_x)
    zffn = jnp.zeros((depth, bsz) + state_ffn.shape[2:], dt_x)

    y_p, p_c, p_n, p_m, p_conv, p_k, p_v, p_ffn = _trunk(
        x_prompt, mods, db, zc, zn, zm, zconv, zffn, wts, prompt_attend)
    y_s, s_c, s_n, s_m, s_conv, s_k, s_v, s_ffn = _trunk(
        x_sample, mods, 0, state_mlstm_c, state_mlstm_n, state_mlstm_m, state_conv, state_ffn, wts, sample_attend)

    def heads(a):
        return a.reshape(a.shape[:3] + (n_hc, dh))

    return (y_p, y_s, p_c, p_n, p_m, p_conv, heads(p_k), heads(p_v), p_ffn,
            s_c, s_n, s_m, s_conv, heads(s_k), heads(s_v), s_ffn)
```

```python
import functools
import math

import jax
import jax.numpy as jnp
from jax import lax
from jax.experimental import pallas as pl
from jax.experimental.pallas import tpu as pltpu

F32 = jnp.float32
BF16 = jnp.bfloat16
HIGHEST = lax.Precision.HIGHEST

LANES = 128
SUBLANES = 8
VMEM_LIMIT_BYTES = 56 * 1024 * 1024

EPS = 1e-6
LN_EPS = 1e-5
MLSTM_CHUNK = 256
MLSTM_MIN_CHUNK = 32
MOBA_BLOCK = 256
MOBA_TOPK = 3
REL_BUCKETS = 32
REL_MAX_DIST = 128
NEG = -1e30
LOG2E = math.log2(math.e)
CONV_HALO = 32

_NT = (((1,), (1,)), ((), ()))
_TN = (((0,), (0,)), ((), ()))


def _cparams(n_axes):
    return pltpu.CompilerParams(dimension_semantics=("arbitrary",) * n_axes, vmem_limit_bytes=VMEM_LIMIT_BYTES)


def _sigmoid(x):
    return jax.nn.sigmoid(x)


def _col_tile(n, pref):
    tile = min(pref, n)
    while n % tile:
        tile -= LANES
    return tile


def _row_tiling(bn, t, tm):
    if t >= tm:
        assert t % tm == 0
        tpb = t // tm
        return 1, tm, bn * tpb, tpb
    gb = min(bn, tm // t)
    assert bn % gb == 0 and t % SUBLANES == 0
    return gb, t, bn // gb, 1


def _mod_kernel(c_ref, w_ref, b_ref, o_ref):
    c = c_ref[...]
    s = (c * _sigmoid(c)).astype(BF16)
    mod = jnp.dot(s, w_ref[...].astype(BF16), preferred_element_type=F32) + b_ref[...]
    o_ref[...] = mod.reshape(o_ref.shape)


def _adaln_mods(c_all, ada_w, ada_b, tn=512):
    depth, two, d, n3 = ada_w.shape
    nlj = depth * two
    n_parts = n3 // d
    rows = c_all.shape[0]
    tn = _col_tile(d, tn)
    npj = d // tn
    return pl.pallas_call(
        _mod_kernel,
        grid=(nlj, n3 // tn),
        in_specs=[pl.BlockSpec((rows, d), lambda l, j: (0, 0)),
                  pl.BlockSpec((None, d, tn), lambda l, j: (l, 0, j)),
                  pl.BlockSpec((None, 1, tn), lambda l, j: (l, 0, j))],
        out_specs=pl.BlockSpec((None, rows, 1, tn), lambda l, j: (l * n_parts + j // npj, 0, 0, j % npj)),
        out_shape=jax.ShapeDtypeStruct((nlj * n_parts, rows, 1, d), F32),
        compiler_params=_cparams(2), name="adaln_mods",
    )(c_all, ada_w.reshape(nlj, d, n3), ada_b.reshape(nlj, 1, n3))


def _mod_spec(mod, part, gb, tpb, width=None, by_col=False):
    mods, _, row0 = mod
    assert row0 % gb == 0
    width = width or mods.shape[3]
    return pl.BlockSpec((None, gb, 1, width), lambda i, j: (part, i // tpb + row0 // gb, 0, j if by_col else 0))


def _nm_matmul_kernel(x_ref, sh_ref, sc_ref, *rest, normed, narrow):
    n_s, n_g = len(normed), sum(normed)
    w_refs, g_refs, rest = rest[:n_s], rest[n_s:n_s + n_g], rest[n_s + n_g:]
    if narrow:
        wn_ref, rest = rest[0], rest[1:]
    o_refs, h_ref = rest[:n_s], rest[-1]

    @pl.when(pl.program_id(1) == 0)
    def _():
        x = x_ref[...]
        r = lax.rsqrt(jnp.mean(x * x, axis=-1, keepdims=True) + EPS)
        h = (x * r) * (1.0 + sc_ref[...]) + sh_ref[...]
        h_ref[...] = h.reshape(h_ref.shape).astype(BF16)
        if narrow:
            rest[n_s][...] = jnp.dot(h_ref[...], wn_ref[...], preferred_element_type=F32)

    h = h_ref[...]
    g_iter = iter(g_refs)
    for w_ref, o_ref, has_g in zip(w_refs, o_refs, normed):
        acc = jnp.dot(h, w_ref[...], preferred_element_type=F32)
        if not has_g:
            o_ref[...] = acc.astype(o_ref.dtype)
            continue
        g = next(g_iter)[...]
        for s in range(acc.shape[1] // LANES):
            y = acc[:, s * LANES:(s + 1) * LANES]
            y = y * lax.rsqrt(jnp.mean(y * y, axis=-1, keepdims=True) + EPS) * g
            o_ref[:, s * LANES:(s + 1) * LANES] = y.astype(o_ref.dtype)


def _nm_matmul(x3, mod, sections, width, *, narrow=None, tm=1024, tn=1024, name="nm_matmul"):
    bn, t, d = x3.shape
    gb, rows, n_i, tpb = _row_tiling(bn, t, tm)
    tmm = gb * rows
    n_s = len(sections)
    tn = _col_tile(math.gcd(width, *[sec[2] for sec in sections if sec[2]]), tn)
    n_j = width // tn
    gains = [sec[3].reshape(1, LANES) for sec in sections if sec[3] is not None]
    in_specs = ([pl.BlockSpec((gb, rows, d), lambda i, j: (i // tpb, i % tpb, 0)),
                 _mod_spec(mod, mod[1], gb, tpb), _mod_spec(mod, mod[1] + 1, gb, tpb)]
                + [pl.BlockSpec((None, d, tn), lambda i, j, l=l, j0=c0 // tn: (l, 0, j + j0))
                   for _, l, c0, _, _ in sections]
                + [pl.BlockSpec((1, LANES), lambda i, j: (0, 0))] * len(gains))
    out_specs = [pl.BlockSpec((tmm, tn), lambda i, j: (i, j))] * n_s
    out_shape = [jax.ShapeDtypeStruct((bn * t, width), sec[4]) for sec in sections]
    extra = []
    if narrow is not None:
        wn, ln = narrow
        in_specs.append(pl.BlockSpec((None, d, LANES), lambda i, j: (ln, 0, 0)))
        out_specs.append(pl.BlockSpec((tmm, LANES), lambda i, j: (i, 0)))
        out_shape.append(jax.ShapeDtypeStruct((bn * t, LANES), F32))
        extra = [wn]
    return pl.pallas_call(
        functools.partial(_nm_matmul_kernel, normed=[sec[3] is not None for sec in sections],
                          narrow=narrow is not None),
        grid=(n_i, n_j),
        in_specs=in_specs,
        out_specs=out_specs,
        out_shape=out_shape,
        scratch_shapes=[pltpu.VMEM((tmm, d), BF16)],
        compiler_params=_cparams(2), name=name,
    )(x3, mod[0], mod[0], *[sec[0] for sec in sections], *gains, *extra)


def _mm_res_kernel(*refs, n_in):
    a_refs, w_refs = refs[:n_in], refs[n_in:2 * n_in]
    x_ref, gate_ref, o_ref = refs[2 * n_in:]
    acc = None
    for a_ref, w_ref in zip(a_refs, w_refs):
        part = jnp.dot(a_ref[...].astype(BF16), w_ref[...], preferred_element_type=F32)
        acc = part if acc is None else acc + part
    x = x_ref[...]
    o_ref[...] = x + gate_ref[...] * acc.reshape(x.shape)


def _mm_res(a_list, w_list, x3, mod, *, tm=1024, tn=1024, name="mm_res"):
    bn, t, d = x3.shape
    gb, rows, n_i, tpb = _row_tiling(bn, t, tm)
    tmm = gb * rows
    tn = _col_tile(d, tn)
    n_in = len(a_list)
    in_specs = ([pl.BlockSpec((tmm, a.shape[1]), lambda i, j: (i, 0)) for a in a_list]
                + [pl.BlockSpec((None, a.shape[1], tn), lambda i, j, l=l, r=r0 // a.shape[1]: (l, r, j))
                   for a, (_, l, r0) in zip(a_list, w_list)]
                + [pl.BlockSpec((gb, rows, tn), lambda i, j: (i // tpb, i % tpb, j)),
                   _mod_spec(mod, mod[1] + 2, gb, tpb, width=tn, by_col=True)])
    return pl.pallas_call(
        functools.partial(_mm_res_kernel, n_in=n_in),
        grid=(n_i, d // tn),
        in_specs=in_specs,
        out_specs=pl.BlockSpec((gb, rows, tn), lambda i, j: (i // tpb, i % tpb, j)),
        out_shape=jax.ShapeDtypeStruct((bn, t, d), F32),
        compiler_params=_cparams(2), name=name,
    )(*a_list, *[w for w, _, _ in w_list], x3, mod[0])


def _mlstm_kernel(q_ref, k_ref, v_ref, zo_ref, g_ref, bg_ref, gn_ref, c0_ref, n0_ref, m0_ref,
                  h_ref, co_ref, no_ref, mo_ref, c_s, n_s, m_s, *, tb, chunk):
    ck = pl.program_id(1)
    L = chunk
    n_heads, dk, dv = c_s.shape

    @pl.when(ck == 0)
    def _():
        c_s[...] = c0_ref[0]
        n_s[...] = n0_ref[0]
        m_s[...] = m0_ref[0]

    def pad_rows(x):
        if tb == L:
            return x
        return jnp.concatenate([x, jnp.zeros((L - tb, x.shape[1]), x.dtype)], axis=0)

    gates = pad_rows(g_ref[0] + bg_ref[...])
    lane = lax.broadcasted_iota(jnp.int32, (L, LANES), 1)
    li_all = jnp.where(lane < n_heads, gates, 0.0)
    lf_all = jnp.where((lane >= n_heads) & (lane < 2 * n_heads),
                       -(jnp.maximum(-gates, 0.0) + jnp.log1p(jnp.exp(-jnp.abs(gates)))), 0.0)
    if tb != L:
        real = lax.broadcasted_iota(jnp.int32, (L, LANES), 0) < tb
        li_all = jnp.where(real | (lane >= n_heads), li_all, NEG)
        lf_all = jnp.where(real, lf_all, 0.0)
    ri = lax.broadcasted_iota(jnp.int32, (L, L), 0)
    ci = lax.broadcasted_iota(jnp.int32, (L, L), 1)
    tri = ri >= ci
    y = li_all + jnp.dot(tri.astype(F32), lf_all, precision=HIGHEST, preferred_element_type=F32)
    er = lax.broadcasted_iota(jnp.int32, (SUBLANES, LANES), 0)
    el = lax.broadcasted_iota(jnp.int32, (SUBLANES, LANES), 1)
    e = jnp.where(er < n_heads, jnp.where(el == er + n_heads, 1.0, 0.0) - jnp.where(el == er, 1.0, 0.0), 0.0)
    rt = lax.dot_general(e, y, _NT, precision=HIGHEST, preferred_element_type=F32)

    for hd in range(n_heads):
        li = y[:, hd:hd + 1]
        b = y[:, n_heads + hd:n_heads + hd + 1]
        r_row = rt[hd:hd + 1, :]
        q = pad_rows(q_ref[0, :, hd * dk:(hd + 1) * dk]).astype(F32) * (dk ** -0.5)
        k = pad_rows(k_ref[0, :, hd * dk:(hd + 1) * dk]).astype(F32)
        v = pad_rows(v_ref[0, :, hd * dv:(hd + 1) * dv]).astype(F32)
        a = b + m_s[hd][:, :1]
        dmat = jnp.where(tri, b - r_row, NEG)
        mt = jnp.maximum(a, jnp.max(dmat, axis=-1, keepdims=True))
        w_inter = jnp.exp(a - mt)
        qb, kb, vb = q.astype(BF16), k.astype(BF16), v.astype(BF16)
        s = lax.dot_general(qb, kb, _NT, preferred_element_type=F32) * jnp.exp(dmat - mt)
        c_old = c_s[hd]
        n_old = n_s[hd]
        num = (w_inter * jnp.dot(qb, c_old.astype(BF16), preferred_element_type=F32)
               + jnp.dot(s.astype(BF16), vb, preferred_element_type=F32))
        den = w_inter * jnp.sum(q * n_old, axis=-1, keepdims=True) + jnp.sum(s, axis=-1, keepdims=True)
        hh = (num / jnp.maximum(jnp.abs(den), jnp.exp(-mt)))[:tb]
        hn = hh * lax.rsqrt(jnp.mean(hh * hh, axis=-1, keepdims=True) + EPS)
        h_ref[0, :, hd * dv:(hd + 1) * dv] = (hn * gn_ref[:, hd * dv:(hd + 1) * dv]
                                              * _sigmoid(zo_ref[0, :, hd * dv:(hd + 1) * dv])).astype(h_ref.dtype)

        m_l = mt[L - 1:L]
        g_inter = jnp.exp(a[L - 1:L] - m_l)
        g_s = jnp.exp(b[L - 1:L] - b + li - m_l)
        c_s[hd] = g_inter * c_old + lax.dot_general(kb, (g_s * v).astype(BF16), _TN, preferred_element_type=F32)
        n_s[hd] = g_inter * n_old + jnp.sum(g_s * k, axis=0, keepdims=True)
        m_s[hd] = jnp.broadcast_to(m_l, (1, LANES))

    @pl.when(ck == pl.num_programs(1) - 1)
    def _():
        co_ref[0] = c_s[...]
        no_ref[0] = n_s[...]
        mo_ref[0] = m_s[...]


def _mlstm(q3, k3, v3, o3, g3, bg, gn, c0, n0, m0, *, out_dtype, name):
    bn, t, _ = q3.shape
    _, n_heads, dk, dv = c0.shape
    assert dk == dv and dk % LANES == 0 and 2 * n_heads <= LANES and n_heads <= SUBLANES
    if t % MLSTM_CHUNK == 0:
        tb = chunk = MLSTM_CHUNK
    else:
        assert t <= MLSTM_MIN_CHUNK
        tb, chunk = t, MLSTM_MIN_CHUNK
    n_c = t // tb
    hw = n_heads * dk
    tok = lambda b, c: (b, c, 0)
    state = lambda b, c: (b, 0, 0, 0)
    h, c, n, m = pl.pallas_call(
        functools.partial(_mlstm_kernel, tb=tb, chunk=chunk),
        grid=(bn, n_c),
        in_specs=[pl.BlockSpec((1, tb, hw), tok)] * 4 + [
                  pl.BlockSpec((1, tb, LANES), tok),
                  pl.BlockSpec((1, LANES), lambda b, c: (0, 0)),
                  pl.BlockSpec((1, hw), lambda b, c: (0, 0)),
                  pl.BlockSpec((1, n_heads, dk, dv), state), pl.BlockSpec((1, n_heads, 1, dk), state),
                  pl.BlockSpec((1, n_heads, 1, LANES), state)],
        out_specs=[pl.BlockSpec((1, tb, hw), tok),
                   pl.BlockSpec((1, n_heads, dk, dv), state), pl.BlockSpec((1, n_heads, 1, dk), state),
                   pl.BlockSpec((1, n_heads, 1, LANES), state)],
        out_shape=[jax.ShapeDtypeStruct((bn, t, hw), out_dtype),
                   jax.ShapeDtypeStruct((bn, n_heads, dk, dv), F32),
                   jax.ShapeDtypeStruct((bn, n_heads, 1, dk), F32),
                   jax.ShapeDtypeStruct((bn, n_heads, 1, LANES), F32)],
        scratch_shapes=[pltpu.VMEM((n_heads, dk, dv), F32), pltpu.VMEM((n_heads, 1, dk), F32),
                        pltpu.VMEM((n_heads, 1, LANES), F32)],
        compiler_params=_cparams(2), name=name,
    )(q3, k3, v3, o3, g3, bg, gn, c0, n0.reshape(bn, n_heads, 1, dk),
      jnp.broadcast_to(m0[:, :, None, None], (bn, n_heads, 1, LANES)))
    return h, c, n.reshape(bn, n_heads, dk), m[:, :, 0, 0]


def _convmod_kernel(a_ref, g_ref, buf_ref, w_ref, b_ref, lg_ref, lb_ref, y_ref, nb_ref, u_s, *, tt, width, rc):
    t = pl.program_id(1)
    hist = width - 1
    bias = b_ref[...]
    lg = lg_ref[...]
    lb = lb_ref[...]
    n_p = rc // SUBLANES
    for bi in range(a_ref.shape[0]):
        @pl.when(t == 0)
        def _():
            buf = buf_ref[bi]
            for r in range(SUBLANES):
                u_s[bi, r, 0:hist - r, :] = buf[r:, :]

        @pl.when(t > 0)
        def _():
            for r in range(SUBLANES):
                u_s[bi, r, 0:CONV_HALO, :] = u_s[bi, r, tt:tt + CONV_HALO, :]

        u = a_ref[bi] * _sigmoid(g_ref[bi])
        for r in range(SUBLANES):
            u_s[bi, r, hist - r:hist - r + tt, :] = u
        for r0 in range(0, tt, rc):
            accs = [jnp.broadcast_to(bias, (SUBLANES, bias.shape[1]))] * n_p
            for j in range(width):
                r = j % SUBLANES
                wj = w_ref[j]
                accs = [acc + wj * u_s[bi, r, r0 + p * SUBLANES + j - r:r0 + (p + 1) * SUBLANES + j - r, :]
                        for p, acc in enumerate(accs)]
            acc = accs[0] if n_p == 1 else jnp.concatenate(accs, axis=0)
            mu = jnp.mean(acc, axis=-1, keepdims=True)
            xc = acc - mu
            var = jnp.mean(xc * xc, axis=-1, keepdims=True)
            y = xc * lax.rsqrt(var + LN_EPS) * lg + lb
            y_ref[bi, r0:r0 + rc, :] = (y * _sigmoid(y)).astype(y_ref.dtype)

        @pl.when(t == pl.num_programs(1) - 1)
        def _():
            nb_ref[bi] = u_s[bi, 0, tt:tt + hist, :]


def _conv_module(a3, g3, cbuf, conv_w, conv_b, ln_g, ln_b, *, out_dtype, name):
    bn, t, _ = a3.shape
    width, ch = conv_w.shape
    assert width - 1 <= CONV_HALO
    tt = 128 if t % 128 == 0 else t
    n_t = t // tt
    assert n_t == 1 or tt >= CONV_HALO
    gb = 1 if n_t > 1 else math.gcd(bn, SUBLANES)
    y, nb = pl.pallas_call(
        functools.partial(_convmod_kernel, tt=tt, width=width,
                          rc=4 * SUBLANES if tt % (4 * SUBLANES) == 0 else SUBLANES),
        grid=(bn // gb, n_t),
        in_specs=[pl.BlockSpec((gb, tt, ch), lambda b, i: (b, i, 0)),
                  pl.BlockSpec((gb, tt, ch), lambda b, i: (b, i, 0)),
                  pl.BlockSpec((gb, width - 1, ch), lambda b, i: (b, 0, 0)),
                  pl.BlockSpec((width, SUBLANES, ch), lambda b, i: (0, 0, 0)),
                  pl.BlockSpec((1, ch), lambda b, i: (0, 0)),
                  pl.BlockSpec((1, ch), lambda b, i: (0, 0)),
                  pl.BlockSpec((1, ch), lambda b, i: (0, 0))],
        out_specs=[pl.BlockSpec((gb, tt, ch), lambda b, i: (b, i, 0)),
                   pl.BlockSpec((gb, width - 1, ch), lambda b, i: (b, 0, 0))],
        out_shape=[jax.ShapeDtypeStruct((bn, t, ch), out_dtype),
                   jax.ShapeDtypeStruct((bn, width - 1, ch), F32)],
        scratch_shapes=[pltpu.VMEM((gb, SUBLANES, CONV_HALO + tt, ch), F32)],
        compiler_params=_cparams(2), name=name,
    )(a3, g3, cbuf, jnp.broadcast_to(conv_w[:, None, :], (width, SUBLANES, ch)), conv_b.reshape(1, ch),
      ln_g.reshape(1, ch), ln_b.reshape(1, ch))
    return y, nb


def _ffn_up_kernel(x_ref, sh_ref, sc_ref, wg_ref, wu_ref, buf_ref, cw_ref, cb_ref, act_ref, gl_ref, h_s, carry_s,
                   *, tpb, rows, strip):
    ti = pl.program_id(0) % tpb
    j = pl.program_id(1)

    @pl.when(j == 0)
    def _():
        x = x_ref[...]
        r = lax.rsqrt(jnp.mean(x * x, axis=-1, keepdims=True) + EPS)
        h = (x * r) * (1.0 + sc_ref[...]) + sh_ref[...]
        h_s[...] = h.reshape(h_s.shape).astype(BF16)

    hist = buf_ref.shape[1]
    if tpb > 1:
        @pl.when(ti == 0)
        def _():
            carry_s[j] = jnp.zeros(carry_s.shape[1:], F32)
            carry_s[j, 0:hist, :] = buf_ref[0]

    h = h_s[...]
    n = h.shape[0]
    gb = n // rows
    tn = wg_ref.shape[1]
    strips = [slice(c0, c0 + strip) for c0 in range(0, tn, strip)]
    gs = [jnp.dot(h, wg_ref[:, cs], preferred_element_type=F32) for cs in strips]
    us = [jnp.dot(h, wu_ref[:, cs], preferred_element_type=F32) for cs in strips]
    for cs, g, u in zip(strips, gs, us):
        if tpb > 1:
            prev = carry_s[j, :, cs]
            carry_s[j, :, cs] = pltpu.roll(g[n - SUBLANES:], 2, axis=0)
        r8 = lax.broadcasted_iota(jnp.int32, (gb * SUBLANES, strip), 0) % SUBLANES
        if tpb == 1:
            rows01 = [jnp.broadcast_to(buf_ref[:, i:i + 1, cs], (gb, SUBLANES, strip)).reshape(gb * SUBLANES, strip)
                      for i in range(hist)]
            prev = jnp.where(r8 == 0, rows01[0], rows01[1])
        prev1 = pltpu.roll(prev, gb * SUBLANES - 1, axis=0)
        gm1 = pltpu.roll(g, 1, axis=0)
        gm2 = pltpu.roll(g, 2, axis=0)
        if rows == SUBLANES:
            gm1 = jnp.where(r8 < 1, prev1, gm1)
            gm2 = jnp.where(r8 < 2, prev, gm2)
            gl_ref[:, :, cs] = g.reshape(gb, SUBLANES, strip)
        else:
            gm1 = jnp.concatenate([jnp.where(r8 < 1, prev1, gm1[:SUBLANES]), gm1[SUBLANES:]], axis=0)
            gm2 = jnp.concatenate([jnp.where(r8 < 2, prev, gm2[:SUBLANES]), gm2[SUBLANES:]], axis=0)
            gl_ref[:, :, cs] = g[n - SUBLANES:].reshape(gb, SUBLANES, strip)
        gc = cw_ref[0:1, cs] * gm2 + cw_ref[1:2, cs] * gm1 + cw_ref[2:3, cs] * g + cb_ref[:, cs]
        act_ref[:, cs] = ((gc * _sigmoid(gc)) * u).astype(act_ref.dtype)


def _ffn_up(x3, mod, w, layer, buf, conv_w, conv_b, *, tm=1024, tn=512, name="ffn_up"):
    bn, t, d = x3.shape
    dff = w.shape[2] // 2
    gb, rows, n_i, tpb = _row_tiling(bn, t, tm)
    tmm = gb * rows
    tn = _col_tile(dff, tn)
    n_f = dff // tn
    hist = buf.shape[1]
    assert conv_w.shape[0] == 3 and hist == 2 and (tpb == 1 or gb == 1)
    act, g_tail = pl.pallas_call(
        functools.partial(_ffn_up_kernel, tpb=tpb, rows=rows, strip=min(tn, 256)),
        grid=(n_i, n_f),
        in_specs=[pl.BlockSpec((gb, rows, d), lambda i, j: (i // tpb, i % tpb, 0)),
                  _mod_spec(mod, mod[1], gb, tpb), _mod_spec(mod, mod[1] + 1, gb, tpb),
                  pl.BlockSpec((None, d, tn), lambda i, j: (layer, 0, j)),
                  pl.BlockSpec((None, d, tn), lambda i, j: (layer, 0, j + n_f)),
                  pl.BlockSpec((gb, hist, tn), lambda i, j: (i // tpb, 0, j)),
                  pl.BlockSpec((3, tn), lambda i, j: (0, j)),
                  pl.BlockSpec((1, tn), lambda i, j: (0, j))],
        out_specs=[pl.BlockSpec((tmm, tn), lambda i, j: (i, j)),
                   pl.BlockSpec((gb, SUBLANES, tn), lambda i, j: (i, 0, j))],
        out_shape=[jax.ShapeDtypeStruct((bn * t, dff), BF16),
                   jax.ShapeDtypeStruct((n_i * gb, SUBLANES, dff), F32)],
        scratch_shapes=[pltpu.VMEM((tmm, d), BF16), pltpu.VMEM((n_f, SUBLANES, tn), F32)],
        compiler_params=_cparams(2), name=name,
    )(x3, mod[0], mod[0], w, w, buf, conv_w, conv_b.reshape(1, dff))
    return act, g_tail.reshape(bn, tpb, SUBLANES, dff)[:, tpb - 1]


def _t5_bucket(dist):
    n = jnp.maximum(dist, 0)
    exact = REL_BUCKETS // 2
    large = exact + (jnp.log(jnp.maximum(n, 1).astype(F32) / exact)
                     / math.log(REL_MAX_DIST / exact) * (REL_BUCKETS - exact)).astype(jnp.int32)
    return jnp.where(n < exact, n, jnp.minimum(large, REL_BUCKETS - 1))


def _bias_lookup(dist, table_col):
    bucket = _t5_bucket(dist)
    bias = jnp.zeros(dist.shape, F32)
    for kk in range(REL_BUCKETS):
        bias = jnp.where(bucket == kk, table_col(kk), bias)
    return bias * LOG2E


def _bias_prompt_kernel(rb_ref, o_ref, *, blk):
    rb = rb_ref[0]
    key = lax.broadcasted_iota(jnp.int32, (blk, blk), 0)
    qry = lax.broadcasted_iota(jnp.int32, (blk, blk), 1)
    col = lambda kk: rb[:, kk:kk + 1]
    own = _bias_lookup(qry - key, col)
    o_ref[0, 0] = jnp.where(qry >= key, own, NEG)
    o_ref[0, 1] = _bias_lookup(blk + qry - key, col)


def _bias_prompt(rel_bias, blk):
    n_b, n_h = rel_bias.shape
    rbt = jnp.pad(rel_bias.T, ((0, 0), (0, LANES - n_b))).reshape(n_h, 1, LANES)
    return pl.pallas_call(
        functools.partial(_bias_prompt_kernel, blk=blk),
        grid=(n_h,),
        in_specs=[pl.BlockSpec((1, 1, LANES), lambda h: (h, 0, 0))],
        out_specs=pl.BlockSpec((1, 2, blk, blk), lambda h: (h, 0, 0, 0)),
        out_shape=jax.ShapeDtypeStruct((n_h, 2, blk, blk), F32),
        compiler_params=_cparams(1), name="bias_prompt",
    )(rbt), rbt


def _bias_sample_kernel(rb_ref, fl_ref, new_ref, *, n_heads, t, page):
    rb = rb_ref[...]
    col = lambda kk: rb[:, kk:kk + 1]
    for ref, base in ((fl_ref.at[0], 2 * page), (fl_ref.at[1], page), (new_ref, 0)):
        shape = ref.shape
        r = lax.broadcasted_iota(jnp.int32, shape, 0)
        c = lax.broadcasted_iota(jnp.int32, shape, 1)
        if ref is new_ref:
            key_head, key_off = c // t, c % t
        else:
            key_head, key_off = c % n_heads, c // n_heads
        dist = base + r % t - key_off
        ok = (r // t == key_head) & (dist >= 0)
        ref[...] = jnp.where(ok, _bias_lookup(dist, col), NEG)


def _bias_sample(rel_bias, t, page):
    n_b, n_h = rel_bias.shape
    rows = t * n_h
    rb = jnp.pad(jnp.repeat(rel_bias.T, t, axis=0), ((0, 0), (0, LANES - n_b)))
    return pl.pallas_call(
        functools.partial(_bias_sample_kernel, n_heads=n_h, t=t, page=page),
        out_shape=[jax.ShapeDtypeStruct((2, rows, page * n_h), F32),
                   jax.ShapeDtypeStruct((rows, rows), F32)],
        compiler_params=pltpu.CompilerParams(vmem_limit_bytes=VMEM_LIMIT_BYTES), name="bias_sample",
    )(rb)


def _attn_prompt_kernel(q_ref, k_ref, v_ref, bias_ref, rb_ref, o_ref, km_s, kb_s, vt_s, s_s, *, blk, n_blk, scale):
    km_s[...] = jnp.zeros_like(km_s)
    for j in range(n_blk):
        kj = k_ref[0, j * blk:(j + 1) * blk, :]
        km_s[j:j + 1, :] = jnp.mean(kj, axis=0, keepdims=True)
        kb_s[j * blk:(j + 1) * blk, :] = kj.astype(BF16)
        vt_s[:, j * blk:(j + 1) * blk] = v_ref[0, j * blk:(j + 1) * blk, :].T.astype(BF16)

    far = rb_ref[0][:, REL_BUCKETS - 1:REL_BUCKETS] * LOG2E
    km = km_s[...]
    row = lax.broadcasted_iota(jnp.int32, (km.shape[0], blk), 0)
    tile = 0
    for c in range(n_blk):
        qf = q_ref[0, c * blk:(c + 1) * blk, :]
        qb = qf.astype(BF16)
        add = [0.0] * c
        if c > MOBA_TOPK:
            sc = lax.dot_general(km, qf, _NT, precision=HIGHEST, preferred_element_type=F32)
            sc = jnp.where(row < c, sc, -jnp.inf)
            for j in range(c):
                sj = sc[j:j + 1, :]
                ahead = (sc > sj) | ((sc == sj) & (row < j))
                rank = jnp.sum(jnp.where(ahead, 1.0, 0.0), axis=0, keepdims=True)
                add[j] = jnp.where(rank < MOBA_TOPK, 0.0, NEG)
        m = None
        for j in range(c + 1):
            st = lax.dot_general(kb_s[j * blk:(j + 1) * blk, :], qb, _NT, preferred_element_type=F32) * (scale * LOG2E)
            if j == c:
                st = st + bias_ref[0, 0]
            elif j == c - 1:
                st = st + (bias_ref[0, 1] + add[j])
            else:
                st = st + (add[j] + far)
            s_s[tile + j] = st
            mj = jnp.max(st, axis=0, keepdims=True)
            m = mj if m is None else jnp.maximum(m, mj)
        l = jnp.zeros_like(m)
        acc = jnp.zeros((vt_s.shape[0], blk), F32)
        for j in range(c + 1):
            p = jnp.exp2(s_s[tile + j] - m)
            l = l + jnp.sum(p, axis=0, keepdims=True)
            acc = acc + jnp.dot(vt_s[:, j * blk:(j + 1) * blk], p.astype(BF16), preferred_element_type=F32)
        o_ref[0, c * blk:(c + 1) * blk, :] = (acc / l).T.astype(o_ref.dtype)
        tile += c + 1


def _attn_prompt(q3, k3, v3, bias_tiles, rbt, *, out_dtype):
    bn, s, _ = q3.shape
    n_h = bias_tiles.shape[0]
    dh = q3.shape[2] // n_h
    blk = MOBA_BLOCK
    assert s % blk == 0 and dh == LANES and blk >= REL_MAX_DIST
    n_blk = s // blk
    head = lambda b, h: (b, 0, h)
    return pl.pallas_call(
        functools.partial(_attn_prompt_kernel, blk=blk, n_blk=n_blk, scale=dh ** -0.5),
        grid=(bn, n_h),
        in_specs=[pl.BlockSpec((1, s, dh), head), pl.BlockSpec((1, s, dh), head), pl.BlockSpec((1, s, dh), head),
                  pl.BlockSpec((1, 2, blk, blk), lambda b, h: (h, 0, 0, 0)),
                  pl.BlockSpec((1, 1, LANES), lambda b, h: (h, 0, 0))],
        out_specs=pl.BlockSpec((1, s, dh), head),
        out_shape=jax.ShapeDtypeStruct(q3.shape, out_dtype),
        scratch_shapes=[pltpu.VMEM((-(-n_blk // SUBLANES) * SUBLANES, dh), F32), pltpu.VMEM((s, dh), BF16),
                        pltpu.VMEM((dh, s), BF16), pltpu.VMEM((n_blk * (n_blk + 1) // 2, blk, blk), F32)],
        compiler_params=_cparams(2), name="moba_prompt",
    )(q3, k3, v3, bias_tiles, rbt)


def _attn_sample_kernel(pt_ref, q_ref, kn_ref, vn_ref, *refs, n_heads, ppb, pps, scale):
    del pt_ref
    kp_refs, vp_refs = refs[:pps], refs[pps:2 * pps]
    bias_ref, new_ref, o_ref, ks_s, m_s, l_s, o_s = refs[2 * pps:]
    g = pl.program_id(1)
    last = pl.num_programs(1) - 1
    n_tok = q_ref.shape[1]
    dh = q_ref.shape[2] // n_heads

    def head_major(ref):
        return jnp.concatenate([ref[0, :, h * dh:(h + 1) * dh] for h in range(n_heads)], axis=0)

    qf = head_major(q_ref)
    qb = qf.astype(BF16)
    rows = n_heads * n_tok

    def partial_softmax(kk, vv, bias):
        s = lax.dot_general(qb, kk.astype(BF16), _NT, preferred_element_type=F32) * (scale * LOG2E) + bias
        m = jnp.max(s, axis=-1, keepdims=True)
        e = jnp.exp2(s - m)
        return m, jnp.sum(e, axis=-1, keepdims=True), jnp.dot(e.astype(BF16), vv.astype(BF16),
                                                              preferred_element_type=F32)

    for r in range(pps):
        pg = g * pps + r
        kp = kp_refs[r][0]
        ks_s[pg] = jnp.sum(kp.reshape(kp.shape[0] // n_heads, n_heads, dh), axis=0)
        bias = bias_ref[jnp.where(g == last, 1, 0)] if r == pps - 1 else bias_ref[0]
        m, l, o = partial_softmax(kp, vp_refs[r][0], bias)
        m_s[pg] = m
        l_s[pg] = l
        o_s[pg] = o

    @pl.when(g == last)
    def _():
        n_pg = m_s.shape[0]
        n_blk = n_pg // ppb
        inv = 1.0 / (ppb * (kp_refs[0].shape[1] // n_heads))
        scs = []
        for j in range(n_blk):
            km = ks_s[j * ppb]
            for pp in range(1, ppb):
                km = km + ks_s[j * ppb + pp]
            km = jnp.broadcast_to((km * inv)[:, None, :], (n_heads, n_tok, dh)).reshape(rows, dh)
            scs.append(jnp.sum(qf * km, axis=-1, keepdims=True))
        sel = []
        for j in range(n_blk):
            rank = jnp.zeros((rows, 1), F32)
            for jj in range(n_blk):
                if jj != j:
                    ahead = (scs[jj] >= scs[j]) if jj < j else (scs[jj] > scs[j])
                    rank = rank + jnp.where(ahead, 1.0, 0.0)
            sel.append(rank < MOBA_TOPK)
        m_n, l_n, o_n = partial_softmax(head_major(kn_ref), head_major(vn_ref), new_ref[...])
        m_all = m_n
        for pg in range(n_pg):
            m_all = jnp.maximum(m_all, jnp.where(sel[pg // ppb], m_s[pg], NEG))
        w_n = jnp.exp2(m_n - m_all)
        l_all = w_n * l_n
        o_all = w_n * o_n
        for pg in range(n_pg):
            w = jnp.where(sel[pg // ppb], jnp.exp2(m_s[pg] - m_all), 0.0)
            l_all = l_all + w * l_s[pg]
            o_all = o_all + w * o_s[pg]
        o = (o_all / l_all).astype(o_ref.dtype)
        for h in range(n_heads):
            o_ref[0, :, h * dh:(h + 1) * dh] = o[h * n_tok:(h + 1) * n_tok]


def _attn_sample(q, k, v, cache_k, cache_v, page_table, far_last, new, *, out_dtype, pps=8):
    db, t, hd = q.shape
    n_pool, page, n_h, dh = cache_k.shape
    n_pages = page_table.shape[1]
    rows = t * n_h
    assert MOBA_BLOCK % page == 0 and (n_pages * page) % MOBA_BLOCK == 0 and t <= MOBA_BLOCK
    assert page >= REL_MAX_DIST and n_pages * page // MOBA_BLOCK > MOBA_TOPK and dh == LANES
    assert n_pages % pps == 0
    ppb = MOBA_BLOCK // page
    seq = lambda b, g, pt: (b, 0, 0)
    paged = [(lambda b, g, pt, r=r: (pt[b, g * pps + r], 0, 0)) for r in range(pps)]
    page_spec = lambda r: pl.BlockSpec((1, page * n_h, dh), paged[r])
    grid_spec = pltpu.PrefetchScalarGridSpec(
        num_scalar_prefetch=1,
        grid=(db, n_pages // pps),
        in_specs=([pl.BlockSpec((1, t, hd), seq)] * 3 + [page_spec(r) for r in range(pps)] * 2
                  + [pl.BlockSpec(far_last.shape, lambda b, g, pt: (0, 0, 0)),
                     pl.BlockSpec(new.shape, lambda b, g, pt: (0, 0))]),
        out_specs=pl.BlockSpec((1, t, hd), seq),
        scratch_shapes=[pltpu.VMEM((n_pages, n_h, dh), F32), pltpu.VMEM((n_pages, rows, 1), F32),
                        pltpu.VMEM((n_pages, rows, 1), F32), pltpu.VMEM((n_pages, rows, dh), F32)],
    )
    ck = cache_k.reshape(n_pool, page * n_h, dh)
    cv = cache_v.reshape(n_pool, page * n_h, dh)
    return pl.pallas_call(
        functools.partial(_attn_sample_kernel, n_heads=n_h, ppb=ppb, pps=pps, scale=dh ** -0.5),
        grid_spec=grid_spec,
        out_shape=jax.ShapeDtypeStruct((db, t, hd), out_dtype),
        compiler_params=_cparams(2), name="moba_sample",
    )(page_table, q, k, v, *([ck] * pps), *([cv] * pps), far_last, new)


def _trunk(x, mods, row0, mc, mn, mm, cb, fb, wts, attend):
    bn, t, d = x.shape
    depth = wts["ffn_up"].shape[0]
    n_ha = mc.shape[2]
    qkw = n_ha * mc.shape[3]
    aw = n_ha * mc.shape[4]
    bw = wts["conv_w"].shape[2]
    big = t >= 512
    act_dtype = BF16 if big else F32
    out_c, out_n, out_m, out_conv, out_k, out_v, out_ffn = [], [], [], [], [], [], []
    for l in range(depth):
        mod = (mods, 3 * (2 * l), row0)
        if l % 2 == 0:
            e = l // 2
            assert qkw == aw == bw
            w_a, w_b = wts["in_even"], wts["in_even_b"]
            secs = ([(w_a, e, i * qkw, None, act_dtype) for i in range(3)] + [(w_a, e, 3 * qkw, None, F32)]
                    + [(w_b, e, i * bw, None, F32) for i in range(2)])
            *zs, g = _nm_matmul(x, mod, secs, qkw, narrow=(wts["in_even_gate"], e), tn=256 if big else 128,
                                name="in_even")
            zq, zk, zv, zo, za, zg = (z.reshape(bn, t, qkw) for z in zs)
            h_a, c_new, n_new, m_new = _mlstm(zq, zk, zv, zo, g.reshape(bn, t, LANES), wts["b_gates"][e],
                                              wts["mlstm_norm_g"][e], mc[e], mn[e], mm[e], out_dtype=act_dtype,
                                              name="mlstm")
            y_b, cb_new = _conv_module(za, zg, cb[e], wts["conv_w"][e], wts["conv_b"][e],
                                       wts["conv_ln_g"][e], wts["conv_ln_b"][e], out_dtype=act_dtype, name="conv_module")
            x = _mm_res([h_a.reshape(bn * t, aw), y_b.reshape(bn * t, bw)],
                        [(wts["out_even"], e, 0), (wts["out_even"], e, aw)], x, mod, name="out_even")
            out_c.append(c_new); out_n.append(n_new); out_m.append(m_new); out_conv.append(cb_new)
        else:
            o = l // 2
            w = wts["in_odd"]
            qkv = _nm_matmul(x, mod, [(w, o, 0, wts["q_norm_g"][o], F32), (w, o, d, wts["k_norm_g"][o], F32),
                                      (w, o, 2 * d, None, F32)], d, tn=256, name="in_odd")
            q, k, v = (a.reshape(bn, t, d) for a in qkv)
            a = attend(o, q, k, v, act_dtype)
            x = _mm_res([a.reshape(bn * t, d)], [(wts["out_odd"], o, 0)], x, mod, name="out_odd")
            out_k.append(k); out_v.append(v)
        mod = (mods, 3 * (2 * l + 1), row0)
        hist = fb[l].shape[1]
        assert hist <= min(t, SUBLANES)
        act, g_last = _ffn_up(x, mod, wts["ffn_up"], l, fb[l], wts["ffn_conv_w"][l], wts["ffn_conv_b"][l])
        x = _mm_res([act], [(wts["ffn_down"], l, 0)], x, mod, tn=512, name="ffn_down")
        out_ffn.append(g_last[:, SUBLANES - hist:])
    return (x, jnp.stack(out_c), jnp.stack(out_n), jnp.stack(out_m), jnp.stack(out_conv),
            jnp.stack(out_k), jnp.stack(out_v), jnp.stack(out_ffn))


def kernel(x_prompt, x_sample, state_mlstm_c, state_mlstm_n, state_mlstm_m, state_conv, cache_k, cache_v, page_table, state_ffn, c_prompt, c_sample, ada_w, ada_b, w_in_even, b_gates, mlstm_norm_g, conv_w, conv_b, conv_ln_g, conv_ln_b, w_out_even, w_in_odd, q_norm_g, k_norm_g, rel_bias, w_out_odd, ffn_w_up, ffn_conv_w, ffn_conv_b, ffn_w_down):
    bsz, seq, d = x_prompt.shape
    db, dt, _ = x_sample.shape
    depth = ada_w.shape[0]
    n_even, _, n_ha, dk, dv = state_mlstm_c.shape
    n_hc, dh = cache_k.shape[3], cache_k.shape[4]
    qkw, aw, bw = n_ha * dk, n_ha * dv, conv_w.shape[2]
    n_gate = 2 * n_ha
    o_gate = 2 * qkw + 2 * aw
    dt_x = x_prompt.dtype

    wts = {
        "in_even": w_in_even.astype(BF16),
        "in_even_b": w_in_even[:, :, o_gate + n_gate:].astype(BF16),
        "in_even_gate": jnp.pad(w_in_even[:, :, o_gate:o_gate + n_gate],
                                ((0, 0), (0, 0), (0, LANES - n_gate))).astype(BF16),
        "b_gates": jnp.pad(b_gates, ((0, 0), (0, LANES - n_gate))).reshape(n_even, 1, LANES),
        "mlstm_norm_g": mlstm_norm_g.reshape(n_even, 1, aw),
        "conv_w": conv_w, "conv_b": conv_b, "conv_ln_g": conv_ln_g, "conv_ln_b": conv_ln_b,
        "out_even": w_out_even.astype(BF16),
        "in_odd": w_in_odd.astype(BF16), "q_norm_g": q_norm_g, "k_norm_g": k_norm_g,
        "out_odd": w_out_odd.astype(BF16),
        "ffn_up": ffn_w_up.astype(BF16), "ffn_conv_w": ffn_conv_w, "ffn_conv_b": ffn_conv_b,
        "ffn_down": ffn_w_down.astype(BF16),
    }

    mods = _adaln_mods(jnp.concatenate([c_sample, c_prompt], axis=0), ada_w, ada_b)

    bias_tiles, rbt = _bias_prompt(rel_bias, MOBA_BLOCK)
    far_last, new = _bias_sample(rel_bias, dt, cache_k.shape[2])

    def prompt_attend(o, q, k, v, out_dtype):
        return _attn_prompt(q, k, v, bias_tiles, rbt, out_dtype=out_dtype)

    def sample_attend(o, q, k, v, out_dtype):
        return _attn_sample(q, k, v, cache_k[o], cache_v[o], page_table, far_last, new, out_dtype=out_dtype)

    zc = jnp.zeros((n_even, bsz, n_ha, dk, dv), dt_x)
    zn = jnp.zeros((n_even, bsz, n_ha, dk), dt_x)
    zm = jnp.zeros((n_even, bsz, n_ha), dt_x)
    zconv = jnp.zeros((n_even, bsz) + state_conv.shape[2:], dt_x)
    zffn = jnp.zeros((depth, bsz) + state_ffn.shape[2:], dt_x)

    y_p, p_c, p_n, p_m, p_conv, p_k, p_v, p_ffn = _trunk(
        x_prompt, mods, db, zc, zn, zm, zconv, zffn, wts, prompt_attend)
    y_s, s_c, s_n, s_m, s_conv, s_k, s_v, s_ffn = _trunk(
        x_sample, mods, 0, state_mlstm_c, state_mlstm_n, state_mlstm_m, state_conv, state_ffn, wts, sample_attend)

    def heads(a):
        return a.reshape(a.shape[:3] + (n_hc, dh))

    return (y_p, y_s, p_c, p_n, p_m, p_conv, heads(p_k), heads(p_v), p_ffn,
            s_c, s_n, s_m, s_conv, heads(s_k), heads(s_v), s_ffn)
```

```python
import functools
import math

import jax
import jax.numpy as jnp
from jax import lax
from jax.experimental import pallas as pl
from jax.experimental.pallas import tpu as pltpu

F32 = jnp.float32
BF16 = jnp.bfloat16
HIGHEST = lax.Precision.HIGHEST

LANES = 128
SUBLANES = 8
VMEM_LIMIT_BYTES = 56 * 1024 * 1024

EPS = 1e-6
LN_EPS = 1e-5
MLSTM_CHUNK = 256
MLSTM_MIN_CHUNK = 32
MOBA_BLOCK = 256
MOBA_TOPK = 3
REL_BUCKETS = 32
REL_MAX_DIST = 128
NEG = -1e30
LOG2E = math.log2(math.e)
CONV_HALO = 32

_NT = (((1,), (1,)), ((), ()))
_TN = (((0,), (0,)), ((), ()))


def _cparams(n_axes):
    return pltpu.CompilerParams(dimension_semantics=("arbitrary",) * n_axes, vmem_limit_bytes=VMEM_LIMIT_BYTES)


def _sigmoid(x):
    return jax.nn.sigmoid(x)


def _col_tile(n, pref):
    tile = min(pref, n)
    while n % tile:
        tile -= LANES
    return tile


def _row_tiling(bn, t, tm):
    if t >= tm:
        assert t % tm == 0
        tpb = t // tm
        return 1, tm, bn * tpb, tpb
    gb = min(bn, tm // t)
    assert bn % gb == 0 and t % SUBLANES == 0
    return gb, t, bn // gb, 1


def _mod_kernel(c_ref, w_ref, b_ref, o_ref):
    c = c_ref[...]
    s = (c * _sigmoid(c)).astype(BF16)
    mod = jnp.dot(s, w_ref[...].astype(BF16), preferred_element_type=F32) + b_ref[...]
    o_ref[...] = mod.reshape(o_ref.shape)


def _adaln_mods(c_all, ada_w, ada_b, tn=512):
    depth, two, d, n3 = ada_w.shape
    nlj = depth * two
    n_parts = n3 // d
    rows = c_all.shape[0]
    tn = _col_tile(d, tn)
    npj = d // tn
    return pl.pallas_call(
        _mod_kernel,
        grid=(nlj, n3 // tn),
        in_specs=[pl.BlockSpec((rows, d), lambda l, j: (0, 0)),
                  pl.BlockSpec((None, d, tn), lambda l, j: (l, 0, j)),
                  pl.BlockSpec((None, 1, tn), lambda l, j: (l, 0, j))],
        out_specs=pl.BlockSpec((None, rows, 1, tn), lambda l, j: (l * n_parts + j // npj, 0, 0, j % npj)),
        out_shape=jax.ShapeDtypeStruct((nlj * n_parts, rows, 1, d), F32),
        compiler_params=_cparams(2), name="adaln_mods",
    )(c_all, ada_w.reshape(nlj, d, n3), ada_b.reshape(nlj, 1, n3))


def _mod_spec(mod, part, gb, tpb, width=None, by_col=False):
    mods, _, row0 = mod
    assert row0 % gb == 0
    width = width or mods.shape[3]
    return pl.BlockSpec((None, gb, 1, width), lambda i, j: (part, i // tpb + row0 // gb, 0, j if by_col else 0))


def _nm_matmul_kernel(x_ref, sh_ref, sc_ref, *rest, normed, narrow):
    n_s, n_g = len(normed), sum(normed)
    w_refs, g_refs, rest = rest[:n_s], rest[n_s:n_s + n_g], rest[n_s + n_g:]
    if narrow:
        wn_ref, rest = rest[0], rest[1:]
    o_refs, h_ref = rest[:n_s], rest[-1]

    @pl.when(pl.program_id(1) == 0)
    def _():
        x = x_ref[...]
        r = lax.rsqrt(jnp.mean(x * x, axis=-1, keepdims=True) + EPS)
        h = (x * r) * (1.0 + sc_ref[...]) + sh_ref[...]
        h_ref[...] = h.reshape(h_ref.shape).astype(BF16)
        if narrow:
            rest[n_s][...] = jnp.dot(h_ref[...], wn_ref[...], preferred_element_type=F32)

    h = h_ref[...]
    g_iter = iter(g_refs)
    for w_ref, o_ref, has_g in zip(w_refs, o_refs, normed):
        acc = jnp.dot(h, w_ref[...], preferred_element_type=F32)
        if not has_g:
            o_ref[...] = acc.astype(o_ref.dtype)
            continue
        g = next(g_iter)[...]
        for s in range(acc.shape[1] // LANES):
            y = acc[:, s * LANES:(s + 1) * LANES]
            y = y * lax.rsqrt(jnp.mean(y * y, axis=-1, keepdims=True) + EPS) * g
            o_ref[:, s * LANES:(s + 1) * LANES] = y.astype(o_ref.dtype)


def _nm_matmul(x3, mod, sections, width, *, narrow=None, tm=1024, tn=1024, name="nm_matmul"):
    bn, t, d = x3.shape
    gb, rows, n_i, tpb = _row_tiling(bn, t, tm)
    tmm = gb * rows
    n_s = len(sections)
    tn = _col_tile(math.gcd(width, *[sec[2] for sec in sections if sec[2]]), tn)
    n_j = width // tn
    gains = [sec[3].reshape(1, LANES) for sec in sections if sec[3] is not None]
    in_specs = ([pl.BlockSpec((gb, rows, d), lambda i, j: (i // tpb, i % tpb, 0)),
                 _mod_spec(mod, mod[1], gb, tpb), _mod_spec(mod, mod[1] + 1, gb, tpb)]
                + [pl.BlockSpec((None, d, tn), lambda i, j, l=l, j0=c0 // tn: (l, 0, j + j0))
                   for _, l, c0, _, _ in sections]
                + [pl.BlockSpec((1, LANES), lambda i, j: (0, 0))] * len(gains))
    out_specs = [pl.BlockSpec((tmm, tn), lambda i, j: (i, j))] * n_s
    out_shape = [jax.ShapeDtypeStruct((bn * t, width), sec[4]) for sec in sections]
    extra = []
    if narrow is not None:
        wn, ln = narrow
        in_specs.append(pl.BlockSpec((None, d, LANES), lambda i, j: (ln, 0, 0)))
        out_specs.append(pl.BlockSpec((tmm, LANES), lambda i, j: (i, 0)))
        out_shape.append(jax.ShapeDtypeStruct((bn * t, LANES), F32))
        extra = [wn]
    return pl.pallas_call(
        functools.partial(_nm_matmul_kernel, normed=[sec[3] is not None for sec in sections],
                          narrow=narrow is not None),
        grid=(n_i, n_j),
        in_specs=in_specs,
        out_specs=out_specs,
        out_shape=out_shape,
        scratch_shapes=[pltpu.VMEM((tmm, d), BF16)],
        compiler_params=_cparams(2), name=name,
    )(x3, mod[0], mod[0], *[sec[0] for sec in sections], *gains, *extra)


def _mm_res_kernel(*refs, n_in):
    a_refs, w_refs = refs[:n_in], refs[n_in:2 * n_in]
    x_ref, gate_ref, o_ref = refs[2 * n_in:]
    acc = None
    for a_ref, w_ref in zip(a_refs, w_refs):
        part = jnp.dot(a_ref[...].astype(BF16), w_ref[...], preferred_element_type=F32)
        acc = part if acc is None else acc + part
    x = x_ref[...]
    o_ref[...] = x + gate_ref[...] * acc.reshape(x.shape)


def _mm_res(a_list, w_list, x3, mod, *, tm=1024, tn=1024, name="mm_res"):
    bn, t, d = x3.shape
    gb, rows, n_i, tpb = _row_tiling(bn, t, tm)
    tmm = gb * rows
    tn = _col_tile(d, tn)
    n_in = len(a_list)
    in_specs = ([pl.BlockSpec((tmm, a.shape[1]), lambda i, j: (i, 0)) for a in a_list]
                + [pl.BlockSpec((None, a.shape[1], tn), lambda i, j, l=l, r=r0 // a.shape[1]: (l, r, j))
                   for a, (_, l, r0) in zip(a_list, w_list)]
                + [pl.BlockSpec((gb, rows, tn), lambda i, j: (i // tpb, i % tpb, j)),
                   _mod_spec(mod, mod[1] + 2, gb, tpb, width=tn, by_col=True)])
    return pl.pallas_call(
        functools.partial(_mm_res_kernel, n_in=n_in),
        grid=(n_i, d // tn),
        in_specs=in_specs,
        out_specs=pl.BlockSpec((gb, rows, tn), lambda i, j: (i // tpb, i % tpb, j)),
        out_shape=jax.ShapeDtypeStruct((bn, t, d), F32),
        compiler_params=_cparams(2), name=name,
    )(*a_list, *[w for w, _, _ in w_list], x3, mod[0])


def _mlstm_kernel(q_ref, k_ref, v_ref, zo_ref, g_ref, bg_ref, gn_ref, c0_ref, n0_ref, m0_ref,
                  h_ref, co_ref, no_ref, mo_ref, c_s, n_s, m_s, *, tb, chunk):
    ck = pl.program_id(1)
    L = chunk
    n_seq, n_heads, dk, dv = c_s.shape

    @pl.when(ck == 0)
    def _():
        c_s[...] = c0_ref[...]
        n_s[...] = n0_ref[...]
        m_s[...] = m0_ref[...]

    def pad_rows(x):
        if tb == L:
            return x
        return jnp.concatenate([x, jnp.zeros((L - tb, x.shape[1]), x.dtype)], axis=0)

    for bi in range(n_seq):
        _mlstm_sequence(bi, q_ref, k_ref, v_ref, zo_ref, g_ref, bg_ref, gn_ref, h_ref, c_s, n_s, m_s, pad_rows, tb, L)

    @pl.when(ck == pl.num_programs(1) - 1)
    def _():
        co_ref[...] = c_s[...]
        no_ref[...] = n_s[...]
        mo_ref[...] = m_s[...]


def _mlstm_sequence(bi, q_ref, k_ref, v_ref, zo_ref, g_ref, bg_ref, gn_ref, h_ref, c_s, n_s, m_s, pad_rows, tb, L):
    _, n_heads, dk, dv = c_s.shape
    gates = pad_rows(g_ref[bi] + bg_ref[...])
    lane = lax.broadcasted_iota(jnp.int32, (L, LANES), 1)
    li_all = jnp.where(lane < n_heads, gates, 0.0)
    lf_all = jnp.where((lane >= n_heads) & (lane < 2 * n_heads),
                       -(jnp.maximum(-gates, 0.0) + jnp.log1p(jnp.exp(-jnp.abs(gates)))), 0.0)
    if tb != L:
        real = lax.broadcasted_iota(jnp.int32, (L, LANES), 0) < tb
        li_all = jnp.where(real | (lane >= n_heads), li_all, NEG)
        lf_all = jnp.where(real, lf_all, 0.0)
    ri = lax.broadcasted_iota(jnp.int32, (L, L), 0)
    ci = lax.broadcasted_iota(jnp.int32, (L, L), 1)
    tri = ri >= ci
    y = li_all + jnp.dot(tri.astype(F32), lf_all, precision=HIGHEST, preferred_element_type=F32)
    er = lax.broadcasted_iota(jnp.int32, (SUBLANES, LANES), 0)
    el = lax.broadcasted_iota(jnp.int32, (SUBLANES, LANES), 1)
    e = jnp.where(er < n_heads, jnp.where(el == er + n_heads, 1.0, 0.0) - jnp.where(el == er, 1.0, 0.0), 0.0)
    rt = lax.dot_general(e, y, _NT, precision=HIGHEST, preferred_element_type=F32)

    for hd in range(n_heads):
        li = y[:, hd:hd + 1]
        b = y[:, n_heads + hd:n_heads + hd + 1]
        r_row = rt[hd:hd + 1, :]
        q = pad_rows(q_ref[bi, :, hd * dk:(hd + 1) * dk]).astype(F32) * (dk ** -0.5)
        k = pad_rows(k_ref[bi, :, hd * dk:(hd + 1) * dk]).astype(F32)
        v = pad_rows(v_ref[bi, :, hd * dv:(hd + 1) * dv]).astype(F32)
        a = b + m_s[bi, hd][:, :1]
        dmat = jnp.where(tri, b - r_row, NEG)
        mt = jnp.maximum(a, jnp.max(dmat, axis=-1, keepdims=True))
        w_inter = jnp.exp(a - mt)
        qb, kb, vb = q.astype(BF16), k.astype(BF16), v.astype(BF16)
        s = lax.dot_general(qb, kb, _NT, preferred_element_type=F32) * jnp.exp(dmat - mt)
        c_old = c_s[bi, hd]
        n_old = n_s[bi, hd]
        num = (w_inter * jnp.dot(qb, c_old.astype(BF16), preferred_element_type=F32)
               + jnp.dot(s.astype(BF16), vb, preferred_element_type=F32))
        den = w_inter * jnp.sum(q * n_old, axis=-1, keepdims=True) + jnp.sum(s, axis=-1, keepdims=True)
        hh = (num / jnp.maximum(jnp.abs(den), jnp.exp(-mt)))[:tb]
        hn = hh * lax.rsqrt(jnp.mean(hh * hh, axis=-1, keepdims=True) + EPS)
        h_ref[bi, :, hd * dv:(hd + 1) * dv] = (hn * gn_ref[:, hd * dv:(hd + 1) * dv]
                                               * _sigmoid(zo_ref[bi, :, hd * dv:(hd + 1) * dv])).astype(h_ref.dtype)

        m_l = mt[L - 1:L]
        g_inter = jnp.exp(a[L - 1:L] - m_l)
        g_s = jnp.exp(b[L - 1:L] - b + li - m_l)
        c_s[bi, hd] = g_inter * c_old + lax.dot_general(kb, (g_s * v).astype(BF16), _TN, preferred_element_type=F32)
        n_s[bi, hd] = g_inter * n_old + jnp.sum(g_s * k, axis=0, keepdims=True)
        m_s[bi, hd] = jnp.broadcast_to(m_l, (1, LANES))


def _mlstm(q3, k3, v3, o3, g3, bg, gn, c0, n0, m0, *, out_dtype, name):
    bn, t, _ = q3.shape
    _, n_heads, dk, dv = c0.shape
    assert dk == dv and dk % LANES == 0 and 2 * n_heads <= LANES and n_heads <= SUBLANES
    if t % MLSTM_CHUNK == 0:
        tb = chunk = MLSTM_CHUNK
    else:
        assert t <= MLSTM_MIN_CHUNK
        tb, chunk = t, MLSTM_MIN_CHUNK
    n_c = t // tb
    hw = n_heads * dk
    gb = 1 if n_c > 1 else math.gcd(bn, 4)
    tok = lambda b, c: (b, c, 0)
    state = lambda b, c: (b, 0, 0, 0)
    h, c, n, m = pl.pallas_call(
        functools.partial(_mlstm_kernel, tb=tb, chunk=chunk),
        grid=(bn // gb, n_c),
        in_specs=[pl.BlockSpec((gb, tb, hw), tok)] * 4 + [
                  pl.BlockSpec((gb, tb, LANES), tok),
                  pl.BlockSpec((1, LANES), lambda b, c: (0, 0)),
                  pl.BlockSpec((1, hw), lambda b, c: (0, 0)),
                  pl.BlockSpec((gb, n_heads, dk, dv), state), pl.BlockSpec((gb, n_heads, 1, dk), state),
                  pl.BlockSpec((gb, n_heads, 1, LANES), state)],
        out_specs=[pl.BlockSpec((gb, tb, hw), tok),
                   pl.BlockSpec((gb, n_heads, dk, dv), state), pl.BlockSpec((gb, n_heads, 1, dk), state),
                   pl.BlockSpec((gb, n_heads, 1, LANES), state)],
        out_shape=[jax.ShapeDtypeStruct((bn, t, hw), out_dtype),
                   jax.ShapeDtypeStruct((bn, n_heads, dk, dv), F32),
                   jax.ShapeDtypeStruct((bn, n_heads, 1, dk), F32),
                   jax.ShapeDtypeStruct((bn, n_heads, 1, LANES), F32)],
        scratch_shapes=[pltpu.VMEM((gb, n_heads, dk, dv), F32), pltpu.VMEM((gb, n_heads, 1, dk), F32),
                        pltpu.VMEM((gb, n_heads, 1, LANES), F32)],
        compiler_params=_cparams(2), name=name,
    )(q3, k3, v3, o3, g3, bg, gn, c0, n0.reshape(bn, n_heads, 1, dk),
      jnp.broadcast_to(m0[:, :, None, None], (bn, n_heads, 1, LANES)))
    return h, c, n.reshape(bn, n_heads, dk), m[:, :, 0, 0]


def _convmod_kernel(a_ref, g_ref, buf_ref, w_ref, b_ref, lg_ref, lb_ref, y_ref, nb_ref, u_s, *, tt, width, rc):
    t = pl.program_id(1)
    hist = width - 1
    bias = b_ref[...]
    lg = lg_ref[...]
    lb = lb_ref[...]
    n_p = rc // SUBLANES
    for bi in range(a_ref.shape[0]):
        @pl.when(t == 0)
        def _():
            buf = buf_ref[bi]
            for r in range(SUBLANES):
                u_s[bi, r, 0:hist - r, :] = buf[r:, :]

        @pl.when(t > 0)
        def _():
            for r in range(SUBLANES):
                u_s[bi, r, 0:CONV_HALO, :] = u_s[bi, r, tt:tt + CONV_HALO, :]

        u = a_ref[bi] * _sigmoid(g_ref[bi])
        for r in range(SUBLANES):
            u_s[bi, r, hist - r:hist - r + tt, :] = u
        for r0 in range(0, tt, rc):
            accs = [jnp.broadcast_to(bias, (SUBLANES, bias.shape[1]))] * n_p
            for j in range(width):
                r = j % SUBLANES
                wj = w_ref[j]
                accs = [acc + wj * u_s[bi, r, r0 + p * SUBLANES + j - r:r0 + (p + 1) * SUBLANES + j - r, :]
                        for p, acc in enumerate(accs)]
            acc = accs[0] if n_p == 1 else jnp.concatenate(accs, axis=0)
            mu = jnp.mean(acc, axis=-1, keepdims=True)
            xc = acc - mu
            var = jnp.mean(xc * xc, axis=-1, keepdims=True)
            y = xc * lax.rsqrt(var + LN_EPS) * lg + lb
            y_ref[bi, r0:r0 + rc, :] = (y * _sigmoid(y)).astype(y_ref.dtype)

        @pl.when(t == pl.num_programs(1) - 1)
        def _():
            nb_ref[bi] = u_s[bi, 0, tt:tt + hist, :]


def _conv_module(a3, g3, cbuf, conv_w, conv_b, ln_g, ln_b, *, out_dtype, name):
    bn, t, _ = a3.shape
    width, ch = conv_w.shape
    assert width - 1 <= CONV_HALO
    tt = 128 if t % 128 == 0 else t
    n_t = t // tt
    assert n_t == 1 or tt >= CONV_HALO
    gb = 1 if n_t > 1 else math.gcd(bn, SUBLANES)
    y, nb = pl.pallas_call(
        functools.partial(_convmod_kernel, tt=tt, width=width,
                          rc=4 * SUBLANES if tt % (4 * SUBLANES) == 0 else SUBLANES),
        grid=(bn // gb, n_t),
        in_specs=[pl.BlockSpec((gb, tt, ch), lambda b, i: (b, i, 0)),
                  pl.BlockSpec((gb, tt, ch), lambda b, i: (b, i, 0)),
                  pl.BlockSpec((gb, width - 1, ch), lambda b, i: (b, 0, 0)),
                  pl.BlockSpec((width, SUBLANES, ch), lambda b, i: (0, 0, 0)),
                  pl.BlockSpec((1, ch), lambda b, i: (0, 0)),
                  pl.BlockSpec((1, ch), lambda b, i: (0, 0)),
                  pl.BlockSpec((1, ch), lambda b, i: (0, 0))],
        out_specs=[pl.BlockSpec((gb, tt, ch), lambda b, i: (b, i, 0)),
                   pl.BlockSpec((gb, width - 1, ch), lambda b, i: (b, 0, 0))],
        out_shape=[jax.ShapeDtypeStruct((bn, t, ch), out_dtype),
                   jax.ShapeDtypeStruct((bn, width - 1, ch), F32)],
        scratch_shapes=[pltpu.VMEM((gb, SUBLANES, CONV_HALO + tt, ch), F32)],
        compiler_params=_cparams(2), name=name,
    )(a3, g3, cbuf, jnp.broadcast_to(conv_w[:, None, :], (width, SUBLANES, ch)), conv_b.reshape(1, ch),
      ln_g.reshape(1, ch), ln_b.reshape(1, ch))
    return y, nb


def _ffn_up_kernel(x_ref, sh_ref, sc_ref, wg_ref, wu_ref, buf_ref, cw_ref, cb_ref, act_ref, gl_ref, h_s, carry_s,
                   *, tpb, rows, strip):
    ti = pl.program_id(0) % tpb
    j = pl.program_id(1)

    @pl.when(j == 0)
    def _():
        x = x_ref[...]
        r = lax.rsqrt(jnp.mean(x * x, axis=-1, keepdims=True) + EPS)
        h = (x * r) * (1.0 + sc_ref[...]) + sh_ref[...]
        h_s[...] = h.reshape(h_s.shape).astype(BF16)

    hist = buf_ref.shape[1]
    if tpb > 1:
        @pl.when(ti == 0)
        def _():
            carry_s[j] = jnp.zeros(carry_s.shape[1:], F32)
            carry_s[j, 0:hist, :] = buf_ref[0]

    h = h_s[...]
    n = h.shape[0]
    gb = n // rows
    tn = wg_ref.shape[1]
    strips = [slice(c0, c0 + strip) for c0 in range(0, tn, strip)]
    gs = [jnp.dot(h, wg_ref[:, cs], preferred_element_type=F32) for cs in strips]
    us = [jnp.dot(h, wu_ref[:, cs], preferred_element_type=F32) for cs in strips]
    for cs, g, u in zip(strips, gs, us):
        if tpb > 1:
            prev = carry_s[j, :, cs]
            carry_s[j, :, cs] = pltpu.roll(g[n - SUBLANES:], 2, axis=0)
        r8 = lax.broadcasted_iota(jnp.int32, (gb * SUBLANES, strip), 0) % SUBLANES
        if tpb == 1:
            rows01 = [jnp.broadcast_to(buf_ref[:, i:i + 1, cs], (gb, SUBLANES, strip)).reshape(gb * SUBLANES, strip)
                      for i in range(hist)]
            prev = jnp.where(r8 == 0, rows01[0], rows01[1])
        prev1 = pltpu.roll(prev, gb * SUBLANES - 1, axis=0)
        gm1 = pltpu.roll(g, 1, axis=0)
        gm2 = pltpu.roll(g, 2, axis=0)
        if rows == SUBLANES:
            gm1 = jnp.where(r8 < 1, prev1, gm1)
            gm2 = jnp.where(r8 < 2, prev, gm2)
            gl_ref[:, :, cs] = g.reshape(gb, SUBLANES, strip)
        else:
            gm1 = jnp.concatenate([jnp.where(r8 < 1, prev1, gm1[:SUBLANES]), gm1[SUBLANES:]], axis=0)
            gm2 = jnp.concatenate([jnp.where(r8 < 2, prev, gm2[:SUBLANES]), gm2[SUBLANES:]], axis=0)
            gl_ref[:, :, cs] = g[n - SUBLANES:].reshape(gb, SUBLANES, strip)
        gc = cw_ref[0:1, cs] * gm2 + cw_ref[1:2, cs] * gm1 + cw_ref[2:3, cs] * g + cb_ref[:, cs]
        act_ref[:, cs] = ((gc * _sigmoid(gc)) * u).astype(act_ref.dtype)


def _ffn_up(x3, mod, w, layer, buf, conv_w, conv_b, *, tm=1024, tn=512, name="ffn_up"):
    bn, t, d = x3.shape
    dff = w.shape[2] // 2
    gb, rows, n_i, tpb = _row_tiling(bn, t, tm)
    tmm = gb * rows
    tn = _col_tile(dff, tn)
    n_f = dff // tn
    hist = buf.shape[1]
    assert conv_w.shape[0] == 3 and hist == 2 and (tpb == 1 or gb == 1)
    act, g_tail = pl.pallas_call(
        functools.partial(_ffn_up_kernel, tpb=tpb, rows=rows, strip=min(tn, 256)),
        grid=(n_i, n_f),
        in_specs=[pl.BlockSpec((gb, rows, d), lambda i, j: (i // tpb, i % tpb, 0)),
                  _mod_spec(mod, mod[1], gb, tpb), _mod_spec(mod, mod[1] + 1, gb, tpb),
                  pl.BlockSpec((None, d, tn), lambda i, j: (layer, 0, j)),
                  pl.BlockSpec((None, d, tn), lambda i, j: (layer, 0, j + n_f)),
                  pl.BlockSpec((gb, hist, tn), lambda i, j: (i // tpb, 0, j)),
                  pl.BlockSpec((3, tn), lambda i, j: (0, j)),
                  pl.BlockSpec((1, tn), lambda i, j: (0, j))],
        out_specs=[pl.BlockSpec((tmm, tn), lambda i, j: (i, j)),
                   pl.BlockSpec((gb, SUBLANES, tn), lambda i, j: (i, 0, j))],
        out_shape=[jax.ShapeDtypeStruct((bn * t, dff), BF16),
                   jax.ShapeDtypeStruct((n_i * gb, SUBLANES, dff), F32)],
        scratch_shapes=[pltpu.VMEM((tmm, d), BF16), pltpu.VMEM((n_f, SUBLANES, tn), F32)],
        compiler_params=_cparams(2), name=name,
    )(x3, mod[0], mod[0], w, w, buf, conv_w, conv_b.reshape(1, dff))
    return act, g_tail.reshape(bn, tpb, SUBLANES, dff)[:, tpb - 1]


def _t5_bucket(dist):
    n = jnp.maximum(dist, 0)
    exact = REL_BUCKETS // 2
    large = exact + (jnp.log(jnp.maximum(n, 1).astype(F32) / exact)
                     / math.log(REL_MAX_DIST / exact) * (REL_BUCKETS - exact)).astype(jnp.int32)
    return jnp.where(n < exact, n, jnp.minimum(large, REL_BUCKETS - 1))


def _bias_lookup(dist, table_col):
    bucket = _t5_bucket(dist)
    bias = jnp.zeros(dist.shape, F32)
    for kk in range(REL_BUCKETS):
        bias = jnp.where(bucket == kk, table_col(kk), bias)
    return bias * LOG2E


def _bias_prompt_kernel(rb_ref, o_ref, *, blk):
    rb = rb_ref[0]
    key = lax.broadcasted_iota(jnp.int32, (blk, blk), 0)
    qry = lax.broadcasted_iota(jnp.int32, (blk, blk), 1)
    col = lambda kk: rb[:, kk:kk + 1]
    own = _bias_lookup(qry - key, col)
    o_ref[0, 0] = jnp.where(qry >= key, own, NEG)
    o_ref[0, 1] = _bias_lookup(blk + qry - key, col)


def _bias_prompt(rel_bias, blk):
    n_b, n_h = rel_bias.shape
    rbt = jnp.pad(rel_bias.T, ((0, 0), (0, LANES - n_b))).reshape(n_h, 1, LANES)
    return pl.pallas_call(
        functools.partial(_bias_prompt_kernel, blk=blk),
        grid=(n_h,),
        in_specs=[pl.BlockSpec((1, 1, LANES), lambda h: (h, 0, 0))],
        out_specs=pl.BlockSpec((1, 2, blk, blk), lambda h: (h, 0, 0, 0)),
        out_shape=jax.ShapeDtypeStruct((n_h, 2, blk, blk), F32),
        compiler_params=_cparams(1), name="bias_prompt",
    )(rbt), rbt


def _bias_sample_kernel(rb_ref, fl_ref, new_ref, *, n_heads, t, page):
    rb = rb_ref[...]
    col = lambda kk: rb[:, kk:kk + 1]
    for ref, base in ((fl_ref.at[0], 2 * page), (fl_ref.at[1], page), (new_ref, 0)):
        shape = ref.shape
        r = lax.broadcasted_iota(jnp.int32, shape, 0)
        c = lax.broadcasted_iota(jnp.int32, shape, 1)
        if ref is new_ref:
            key_head, key_off = c // t, c % t
        else:
            key_head, key_off = c % n_heads, c // n_heads
        dist = base + r % t - key_off
        ok = (r // t == key_head) & (dist >= 0)
        ref[...] = jnp.where(ok, _bias_lookup(dist, col), NEG)


def _bias_sample(rel_bias, t, page):
    n_b, n_h = rel_bias.shape
    rows = t * n_h
    rb = jnp.pad(jnp.repeat(rel_bias.T, t, axis=0), ((0, 0), (0, LANES - n_b)))
    return pl.pallas_call(
        functools.partial(_bias_sample_kernel, n_heads=n_h, t=t, page=page),
        out_shape=[jax.ShapeDtypeStruct((2, rows, page * n_h), F32),
                   jax.ShapeDtypeStruct((rows, rows), F32)],
        compiler_params=pltpu.CompilerParams(vmem_limit_bytes=VMEM_LIMIT_BYTES), name="bias_sample",
    )(rb)


def _attn_prompt_kernel(q_ref, k_ref, v_ref, bias_ref, rb_ref, o_ref, km_s, kb_s, vt_s, s_s, *, blk, n_blk, scale):
    km_s[...] = jnp.zeros_like(km_s)
    for j in range(n_blk):
        kj = k_ref[0, j * blk:(j + 1) * blk, :]
        km_s[j:j + 1, :] = jnp.mean(kj, axis=0, keepdims=True)
        kb_s[j * blk:(j + 1) * blk, :] = kj.astype(BF16)
        vt_s[:, j * blk:(j + 1) * blk] = v_ref[0, j * blk:(j + 1) * blk, :].T.astype(BF16)

    far = rb_ref[0][:, REL_BUCKETS - 1:REL_BUCKETS] * LOG2E
    km = km_s[...]
    row = lax.broadcasted_iota(jnp.int32, (km.shape[0], blk), 0)
    tile = 0
    for c in range(n_blk):
        qf = q_ref[0, c * blk:(c + 1) * blk, :]
        qb = qf.astype(BF16)
        add = [0.0] * c
        if c > MOBA_TOPK:
            sc = lax.dot_general(km, qf, _NT, precision=HIGHEST, preferred_element_type=F32)
            sc = jnp.where(row < c, sc, -jnp.inf)
            for j in range(c):
                sj = sc[j:j + 1, :]
                ahead = (sc > sj) | ((sc == sj) & (row < j))
                rank = jnp.sum(jnp.where(ahead, 1.0, 0.0), axis=0, keepdims=True)
                add[j] = jnp.where(rank < MOBA_TOPK, 0.0, NEG)
        m = None
        for j in range(c + 1):
            st = lax.dot_general(kb_s[j * blk:(j + 1) * blk, :], qb, _NT, preferred_element_type=F32) * (scale * LOG2E)
            if j == c:
                st = st + bias_ref[0, 0]
            elif j == c - 1:
                st = st + (bias_ref[0, 1] + add[j])
            else:
                st = st + (add[j] + far)
            s_s[tile + j] = st
            mj = jnp.max(st, axis=0, keepdims=True)
            m = mj if m is None else jnp.maximum(m, mj)
        l = jnp.zeros_like(m)
        acc = jnp.zeros((vt_s.shape[0], blk), F32)
        for j in range(c + 1):
            p = jnp.exp2(s_s[tile + j] - m)
            l = l + jnp.sum(p, axis=0, keepdims=True)
            acc = acc + jnp.dot(vt_s[:, j * blk:(j + 1) * blk], p.astype(BF16), preferred_element_type=F32)
        o_ref[0, c * blk:(c + 1) * blk, :] = (acc / l).T.astype(o_ref.dtype)
        tile += c + 1


def _attn_prompt(q3, k3, v3, bias_tiles, rbt, *, out_dtype):
    bn, s, _ = q3.shape
    n_h = bias_tiles.shape[0]
    dh = q3.shape[2] // n_h
    blk = MOBA_BLOCK
    assert s % blk == 0 and dh == LANES and blk >= REL_MAX_DIST
    n_blk = s // blk
    head = lambda b, h: (b, 0, h)
    return pl.pallas_call(
        functools.partial(_attn_prompt_kernel, blk=blk, n_blk=n_blk, scale=dh ** -0.5),
        grid=(bn, n_h),
        in_specs=[pl.BlockSpec((1, s, dh), head), pl.BlockSpec((1, s, dh), head), pl.BlockSpec((1, s, dh), head),
                  pl.BlockSpec((1, 2, blk, blk), lambda b, h: (h, 0, 0, 0)),
                  pl.BlockSpec((1, 1, LANES), lambda b, h: (h, 0, 0))],
        out_specs=pl.BlockSpec((1, s, dh), head),
        out_shape=jax.ShapeDtypeStruct(q3.shape, out_dtype),
        scratch_shapes=[pltpu.VMEM((-(-n_blk // SUBLANES) * SUBLANES, dh), F32), pltpu.VMEM((s, dh), BF16),
                        pltpu.VMEM((dh, s), BF16), pltpu.VMEM((n_blk * (n_blk + 1) // 2, blk, blk), F32)],
        compiler_params=_cparams(2), name="moba_prompt",
    )(q3, k3, v3, bias_tiles, rbt)


def _attn_sample_kernel(pt_ref, q_ref, kn_ref, vn_ref, *refs, n_heads, ppb, pps, scale):
    del pt_ref
    kp_refs, vp_refs = refs[:pps], refs[pps:2 * pps]
    bias_ref, new_ref, o_ref, ks_s, m_s, l_s, o_s = refs[2 * pps:]
    g = pl.program_id(1)
    last = pl.num_programs(1) - 1
    n_tok = q_ref.shape[1]
    dh = q_ref.shape[2] // n_heads

    def head_major(ref):
        return jnp.concatenate([ref[0, :, h * dh:(h + 1) * dh] for h in range(n_heads)], axis=0)

    qf = head_major(q_ref)
    qb = qf.astype(BF16)
    rows = n_heads * n_tok

    def partial_softmax(kk, vv, bias):
        s = lax.dot_general(qb, kk.astype(BF16), _NT, preferred_element_type=F32) * (scale * LOG2E) + bias
        m = jnp.max(s, axis=-1, keepdims=True)
        e = jnp.exp2(s - m)
        return m, jnp.sum(e, axis=-1, keepdims=True), jnp.dot(e.astype(BF16), vv.astype(BF16),
                                                              preferred_element_type=F32)

    for r in range(pps):
        pg = g * pps + r
        kp = kp_refs[r][0]
        ks_s[pg] = jnp.sum(kp.reshape(kp.shape[0] // n_heads, n_heads, dh), axis=0)
        bias = bias_ref[jnp.where(g == last, 1, 0)] if r == pps - 1 else bias_ref[0]
        m, l, o = partial_softmax(kp, vp_refs[r][0], bias)
        m_s[pg] = m
        l_s[pg] = l
        o_s[pg] = o

    @pl.when(g == last)
    def _():
        n_pg = m_s.shape[0]
        n_blk = n_pg // ppb
        inv = 1.0 / (ppb * (kp_refs[0].shape[1] // n_heads))
        scs = []
        for j in range(n_blk):
            km = ks_s[j * ppb]
            for pp in range(1, ppb):
                km = km + ks_s[j * ppb + pp]
            km = jnp.broadcast_to((km * inv)[:, None, :], (n_heads, n_tok, dh)).reshape(rows, dh)
            scs.append(jnp.sum(qf * km, axis=-1, keepdims=True))
        sel = []
        for j in range(n_blk):
            rank = jnp.zeros((rows, 1), F32)
            for jj in range(n_blk):
                if jj != j:
                    ahead = (scs[jj] >= scs[j]) if jj < j else (scs[jj] > scs[j])
                    rank = rank + jnp.where(ahead, 1.0, 0.0)
            sel.append(rank < MOBA_TOPK)
        m_n, l_n, o_n = partial_softmax(head_major(kn_ref), head_major(vn_ref), new_ref[...])
        m_all = m_n
        for pg in range(n_pg):
            m_all = jnp.maximum(m_all, jnp.where(sel[pg // ppb], m_s[pg], NEG))
        w_n = jnp.exp2(m_n - m_all)
        l_all = w_n * l_n
        o_all = w_n * o_n
        for pg in range(n_pg):
            w = jnp.where(sel[pg // ppb], jnp.exp2(m_s[pg] - m_all), 0.0)
            l_all = l_all + w * l_s[pg]
            o_all = o_all + w * o_s[pg]
        o = (o_all / l_all).astype(o_ref.dtype)
        for h in range(n_heads):
            o_ref[0, :, h * dh:(h + 1) * dh] = o[h * n_tok:(h + 1) * n_tok]


def _attn_sample(q, k, v, cache_k, cache_v, page_table, far_last, new, *, out_dtype, pps=8):
    db, t, hd = q.shape
    n_pool, page, n_h, dh = cache_k.shape
    n_pages = page_table.shape[1]
    rows = t * n_h
    assert MOBA_BLOCK % page == 0 and (n_pages * page) % MOBA_BLOCK == 0 and t <= MOBA_BLOCK
    assert page >= REL_MAX_DIST and n_pages * page // MOBA_BLOCK > MOBA_TOPK and dh == LANES
    assert n_pages % pps == 0
    ppb = MOBA_BLOCK // page
    seq = lambda b, g, pt: (b, 0, 0)
    paged = [(lambda b, g, pt, r=r: (pt[b, g * pps + r], 0, 0)) for r in range(pps)]
    page_spec = lambda r: pl.BlockSpec((1, page * n_h, dh), paged[r])
    grid_spec = pltpu.PrefetchScalarGridSpec(
        num_scalar_prefetch=1,
        grid=(db, n_pages // pps),
        in_specs=([pl.BlockSpec((1, t, hd), seq)] * 3 + [page_spec(r) for r in range(pps)] * 2
                  + [pl.BlockSpec(far_last.shape, lambda b, g, pt: (0, 0, 0)),
                     pl.BlockSpec(new.shape, lambda b, g, pt: (0, 0))]),
        out_specs=pl.BlockSpec((1, t, hd), seq),
        scratch_shapes=[pltpu.VMEM((n_pages, n_h, dh), F32), pltpu.VMEM((n_pages, rows, 1), F32),
                        pltpu.VMEM((n_pages, rows, 1), F32), pltpu.VMEM((n_pages, rows, dh), F32)],
    )
    ck = cache_k.reshape(n_pool, page * n_h, dh)
    cv = cache_v.reshape(n_pool, page * n_h, dh)
    return pl.pallas_call(
        functools.partial(_attn_sample_kernel, n_heads=n_h, ppb=ppb, pps=pps, scale=dh ** -0.5),
        grid_spec=grid_spec,
        out_shape=jax.ShapeDtypeStruct((db, t, hd), out_dtype),
        compiler_params=_cparams(2), name="moba_sample",
    )(page_table, q, k, v, *([ck] * pps), *([cv] * pps), far_last, new)


def _trunk(x, mods, row0, mc, mn, mm, cb, fb, wts, attend):
    bn, t, d = x.shape
    depth = wts["ffn_up"].shape[0]
    n_ha = mc.shape[2]
    qkw = n_ha * mc.shape[3]
    aw = n_ha * mc.shape[4]
    bw = wts["conv_w"].shape[2]
    big = t >= 512
    act_dtype = BF16 if big else F32
    out_c, out_n, out_m, out_conv, out_k, out_v, out_ffn = [], [], [], [], [], [], []
    for l in range(depth):
        mod = (mods, 3 * (2 * l), row0)
        if l % 2 == 0:
            e = l // 2
            assert qkw == aw == bw
            w_a, w_b = wts["in_even"], wts["in_even_b"]
            secs = ([(w_a, e, i * qkw, None, act_dtype) for i in range(3)] + [(w_a, e, 3 * qkw, None, F32)]
                    + [(w_b, e, i * bw, None, F32) for i in range(2)])
            *zs, g = _nm_matmul(x, mod, secs, qkw, narrow=(wts["in_even_gate"], e), tn=256 if big else 128,
                                name="in_even")
            zq, zk, zv, zo, za, zg = (z.reshape(bn, t, qkw) for z in zs)
            h_a, c_new, n_new, m_new = _mlstm(zq, zk, zv, zo, g.reshape(bn, t, LANES), wts["b_gates"][e],
                                              wts["mlstm_norm_g"][e], mc[e], mn[e], mm[e], out_dtype=act_dtype,
                                              name="mlstm")
            y_b, cb_new = _conv_module(za, zg, cb[e], wts["conv_w"][e], wts["conv_b"][e],
                                       wts["conv_ln_g"][e], wts["conv_ln_b"][e], out_dtype=act_dtype, name="conv_module")
            x = _mm_res([h_a.reshape(bn * t, aw), y_b.reshape(bn * t, bw)],
                        [(wts["out_even"], e, 0), (wts["out_even"], e, aw)], x, mod, name="out_even")
            out_c.append(c_new); out_n.append(n_new); out_m.append(m_new); out_conv.append(cb_new)
        else:
            o = l // 2
            w = wts["in_odd"]
            qkv = _nm_matmul(x, mod, [(w, o, 0, wts["q_norm_g"][o], F32), (w, o, d, wts["k_norm_g"][o], F32),
                                      (w, o, 2 * d, None, F32)], d, tn=256, name="in_odd")
            q, k, v = (a.reshape(bn, t, d) for a in qkv)
            a = attend(o, q, k, v, act_dtype)
            x = _mm_res([a.reshape(bn * t, d)], [(wts["out_odd"], o, 0)], x, mod, name="out_odd")
            out_k.append(k); out_v.append(v)
        mod = (mods, 3 * (2 * l + 1), row0)
        hist = fb[l].shape[1]
        assert hist <= min(t, SUBLANES)
        act, g_last = _ffn_up(x, mod, wts["ffn_up"], l, fb[l], wts["ffn_conv_w"][l], wts["ffn_conv_b"][l])
        x = _mm_res([act], [(wts["ffn_down"], l, 0)], x, mod, tn=512, name="ffn_down")
        out_ffn.append(g_last[:, SUBLANES - hist:])
    return (x, jnp.stack(out_c), jnp.stack(out_n), jnp.stack(out_m), jnp.stack(out_conv),
            jnp.stack(out_k), jnp.stack(out_v), jnp.stack(out_ffn))


def kernel(x_prompt, x_sample, state_mlstm_c, state_mlstm_n, state_mlstm_m, state_conv, cache_k, cache_v, page_table, state_ffn, c_prompt, c_sample, ada_w, ada_b, w_in_even, b_gates, mlstm_norm_g, conv_w, conv_b, conv_ln_g, conv_ln_b, w_out_even, w_in_odd, q_norm_g, k_norm_g, rel_bias, w_out_odd, ffn_w_up, ffn_conv_w, ffn_conv_b, ffn_w_down):
    bsz, seq, d = x_prompt.shape
    db, dt, _ = x_sample.shape
    depth = ada_w.shape[0]
    n_even, _, n_ha, dk, dv = state_mlstm_c.shape
    n_hc, dh = cache_k.shape[3], cache_k.shape[4]
    qkw, aw, bw = n_ha * dk, n_ha * dv, conv_w.shape[2]
    n_gate = 2 * n_ha
    o_gate = 2 * qkw + 2 * aw
    dt_x = x_prompt.dtype

    wts = {
        "in_even": w_in_even.astype(BF16),
        "in_even_b": w_in_even[:, :, o_gate + n_gate:].astype(BF16),
        "in_even_gate": jnp.pad(w_in_even[:, :, o_gate:o_gate + n_gate],
                                ((0, 0), (0, 0), (0, LANES - n_gate))).astype(BF16),
        "b_gates": jnp.pad(b_gates, ((0, 0), (0, LANES - n_gate))).reshape(n_even, 1, LANES),
        "mlstm_norm_g": mlstm_norm_g.reshape(n_even, 1, aw),
        "conv_w": conv_w, "conv_b": conv_b, "conv_ln_g": conv_ln_g, "conv_ln_b": conv_ln_b,
        "out_even": w_out_even.astype(BF16),
        "in_odd": w_in_odd.astype(BF16), "q_norm_g": q_norm_g, "k_norm_g": k_norm_g,
        "out_odd": w_out_odd.astype(BF16),
        "ffn_up": ffn_w_up.astype(BF16), "ffn_conv_w": ffn_conv_w, "ffn_conv_b": ffn_conv_b,
        "ffn_down": ffn_w_down.astype(BF16),
    }

    mods = _adaln_mods(jnp.concatenate([c_sample, c_prompt], axis=0), ada_w, ada_b)

    bias_tiles, rbt = _bias_prompt(rel_bias, MOBA_BLOCK)
    far_last, new = _bias_sample(rel_bias, dt, cache_k.shape[2])

    def prompt_attend(o, q, k, v, out_dtype):
        return _attn_prompt(q, k, v, bias_tiles, rbt, out_dtype=out_dtype)

    def sample_attend(o, q, k, v, out_dtype):
        return _attn_sample(q, k, v, cache_k[o], cache_v[o], page_table, far_last, new, out_dtype=out_dtype)

    zc = jnp.zeros((n_even, bsz, n_ha, dk, dv), dt_x)
    zn = jnp.zeros((n_even, bsz, n_ha, dk), dt_x)
    zm = jnp.zeros((n_even, bsz, n_ha), dt_x)
    zconv = jnp.zeros((n_even, bsz) + state_conv.shape[2:], dt_x)
    zffn = jnp.zeros((depth, bsz) + state_ffn.shape[2:], dt_x)

    y_p, p_c, p_n, p_m, p_conv, p_k, p_v, p_ffn = _trunk(
        x_prompt, mods, db, zc, zn, zm, zconv, zffn, wts, prompt_attend)
    y_s, s_c, s_n, s_m, s_conv, s_k, s_v, s_ffn = _trunk(
        x_sample, mods, 0, state_mlstm_c, state_mlstm_n, state_mlstm_m, state_conv, state_ffn, wts, sample_attend)

    def heads(a):
        return a.reshape(a.shape[:3] + (n_hc, dh))

    return (y_p, y_s, p_c, p_n, p_m, p_conv, heads(p_k), heads(p_v), p_ffn,
            s_c, s_n, s_m, s_conv, heads(s_k), heads(s_v), s_ffn)
```

```python
import functools
import math

import jax
import jax.numpy as jnp
from jax import lax
from jax.experimental import pallas as pl
from jax.experimental.pallas import tpu as pltpu

F32 = jnp.float32
BF16 = jnp.bfloat16
HIGHEST = lax.Precision.HIGHEST

LANES = 128
SUBLANES = 8
VMEM_LIMIT_BYTES = 56 * 1024 * 1024

EPS = 1e-6
LN_EPS = 1e-5
MLSTM_CHUNK = 256
MLSTM_MIN_CHUNK = 32
MOBA_BLOCK = 256
MOBA_TOPK = 3
REL_BUCKETS = 32
REL_MAX_DIST = 128
NEG = -1e30
LOG2E = math.log2(math.e)
CONV_HALO = 32

_NT = (((1,), (1,)), ((), ()))
_TN = (((0,), (0,)), ((), ()))


def _cparams(n_axes):
    return pltpu.CompilerParams(dimension_semantics=("arbitrary",) * n_axes, vmem_limit_bytes=VMEM_LIMIT_BYTES)


def _sigmoid(x):
    return jax.nn.sigmoid(x)


def _col_tile(n, pref):
    tile = min(pref, n)
    while n % tile:
        tile -= LANES
    return tile


def _row_tiling(bn, t, tm):
    if t >= tm:
        assert t % tm == 0
        tpb = t // tm
        return 1, tm, bn * tpb, tpb
    gb = min(bn, tm // t)
    assert bn % gb == 0 and t % SUBLANES == 0
    return gb, t, bn // gb, 1


def _mod_kernel(c_ref, w_ref, b_ref, o_ref):
    c = c_ref[...]
    s = (c * _sigmoid(c)).astype(BF16)
    mod = jnp.dot(s, w_ref[...].astype(BF16), preferred_element_type=F32) + b_ref[...]
    o_ref[...] = mod.reshape(o_ref.shape)


def _adaln_mods(c_all, ada_w, ada_b, tn=512):
    depth, two, d, n3 = ada_w.shape
    nlj = depth * two
    n_parts = n3 // d
    rows = c_all.shape[0]
    tn = _col_tile(d, tn)
    npj = d // tn
    return pl.pallas_call(
        _mod_kernel,
        grid=(nlj, n3 // tn),
        in_specs=[pl.BlockSpec((rows, d), lambda l, j: (0, 0)),
                  pl.BlockSpec((None, d, tn), lambda l, j: (l, 0, j)),
                  pl.BlockSpec((None, 1, tn), lambda l, j: (l, 0, j))],
        out_specs=pl.BlockSpec((None, rows, 1, tn), lambda l, j: (l * n_parts + j // npj, 0, 0, j % npj)),
        out_shape=jax.ShapeDtypeStruct((nlj * n_parts, rows, 1, d), F32),
        compiler_params=_cparams(2), name="adaln_mods",
    )(c_all, ada_w.reshape(nlj, d, n3), ada_b.reshape(nlj, 1, n3))


def _mod_spec(mod, part, gb, tpb, width=None, by_col=False):
    mods, _, row0 = mod
    assert row0 % gb == 0
    width = width or mods.shape[3]
    return pl.BlockSpec((None, gb, 1, width), lambda i, j: (part, i // tpb + row0 // gb, 0, j if by_col else 0))


def _nm_matmul_kernel(x_ref, sh_ref, sc_ref, *rest, normed, narrow):
    n_s, n_g = len(normed), sum(normed)
    w_refs, g_refs, rest = rest[:n_s], rest[n_s:n_s + n_g], rest[n_s + n_g:]
    if narrow:
        wn_ref, rest = rest[0], rest[1:]
    o_refs, h_ref = rest[:n_s], rest[-1]

    @pl.when(pl.program_id(1) == 0)
    def _():
        x = x_ref[...]
        r = lax.rsqrt(jnp.mean(x * x, axis=-1, keepdims=True) + EPS)
        h = (x * r) * (1.0 + sc_ref[...]) + sh_ref[...]
        h_ref[...] = h.reshape(h_ref.shape).astype(BF16)
        if narrow:
            rest[n_s][...] = jnp.dot(h_ref[...], wn_ref[...], preferred_element_type=F32)

    h = h_ref[...]
    g_iter = iter(g_refs)
    for w_ref, o_ref, has_g in zip(w_refs, o_refs, normed):
        acc = jnp.dot(h, w_ref[...], preferred_element_type=F32)
        if not has_g:
            o_ref[...] = acc.astype(o_ref.dtype)
            continue
        g = next(g_iter)[...]
        for s in range(acc.shape[1] // LANES):
            y = acc[:, s * LANES:(s + 1) * LANES]
            y = y * lax.rsqrt(jnp.mean(y * y, axis=-1, keepdims=True) + EPS) * g
            o_ref[:, s * LANES:(s + 1) * LANES] = y.astype(o_ref.dtype)


def _nm_matmul(x3, mod, sections, width, *, narrow=None, tm=1024, tn=1024, name="nm_matmul"):
    bn, t, d = x3.shape
    gb, rows, n_i, tpb = _row_tiling(bn, t, tm)
    tmm = gb * rows
    n_s = len(sections)
    tn = _col_tile(math.gcd(width, *[sec[2] for sec in sections if sec[2]]), tn)
    n_j = width // tn
    gains = [sec[3].reshape(1, LANES) for sec in sections if sec[3] is not None]
    in_specs = ([pl.BlockSpec((gb, rows, d), lambda i, j: (i // tpb, i % tpb, 0)),
                 _mod_spec(mod, mod[1], gb, tpb), _mod_spec(mod, mod[1] + 1, gb, tpb)]
                + [pl.BlockSpec((None, d, tn), lambda i, j, l=l, j0=c0 // tn: (l, 0, j + j0))
                   for _, l, c0, _, _ in sections]
                + [pl.BlockSpec((1, LANES), lambda i, j: (0, 0))] * len(gains))
    out_specs = [pl.BlockSpec((tmm, tn), lambda i, j: (i, j))] * n_s
    out_shape = [jax.ShapeDtypeStruct((bn * t, width), sec[4]) for sec in sections]
    extra = []
    if narrow is not None:
        wn, ln = narrow
        in_specs.append(pl.BlockSpec((None, d, LANES), lambda i, j: (ln, 0, 0)))
        out_specs.append(pl.BlockSpec((tmm, LANES), lambda i, j: (i, 0)))
        out_shape.append(jax.ShapeDtypeStruct((bn * t, LANES), F32))
        extra = [wn]
    return pl.pallas_call(
        functools.partial(_nm_matmul_kernel, normed=[sec[3] is not None for sec in sections],
                          narrow=narrow is not None),
        grid=(n_i, n_j),
        in_specs=in_specs,
        out_specs=out_specs,
        out_shape=out_shape,
        scratch_shapes=[pltpu.VMEM((tmm, d), BF16)],
        compiler_params=_cparams(2), name=name,
    )(x3, mod[0], mod[0], *[sec[0] for sec in sections], *gains, *extra)


def _mm_res_kernel(*refs, n_in):
    a_refs, w_refs = refs[:n_in], refs[n_in:2 * n_in]
    x_ref, gate_ref, o_ref = refs[2 * n_in:]
    acc = None
    for a_ref, w_ref in zip(a_refs, w_refs):
        part = jnp.dot(a_ref[...].astype(BF16), w_ref[...], preferred_element_type=F32)
        acc = part if acc is None else acc + part
    x = x_ref[...]
    o_ref[...] = x + gate_ref[...] * acc.reshape(x.shape)


def _mm_res(a_list, w_list, x3, mod, *, tm=1024, tn=1024, name="mm_res"):
    bn, t, d = x3.shape
    gb, rows, n_i, tpb = _row_tiling(bn, t, tm)
    tmm = gb * rows
    tn = _col_tile(d, tn)
    n_in = len(a_list)
    in_specs = ([pl.BlockSpec((tmm, a.shape[1]), lambda i, j: (i, 0)) for a in a_list]
                + [pl.BlockSpec((None, a.shape[1], tn), lambda i, j, l=l, r=r0 // a.shape[1]: (l, r, j))
                   for a, (_, l, r0) in zip(a_list, w_list)]
                + [pl.BlockSpec((gb, rows, tn), lambda i, j: (i // tpb, i % tpb, j)),
                   _mod_spec(mod, mod[1] + 2, gb, tpb, width=tn, by_col=True)])
    return pl.pallas_call(
        functools.partial(_mm_res_kernel, n_in=n_in),
        grid=(n_i, d // tn),
        in_specs=in_specs,
        out_specs=pl.BlockSpec((gb, rows, tn), lambda i, j: (i // tpb, i % tpb, j)),
        out_shape=jax.ShapeDtypeStruct((bn, t, d), F32),
        compiler_params=_cparams(2), name=name,
    )(*a_list, *[w for w, _, _ in w_list], x3, mod[0])


def _mlstm_kernel(q_ref, k_ref, v_ref, zo_ref, g_ref, bg_ref, gn_ref, c0_ref, n0_ref, m0_ref,
                  h_ref, co_ref, no_ref, mo_ref, c_s, n_s, m_s, *, tb, chunk):
    ck = pl.program_id(1)
    L = chunk
    n_seq, n_heads, dk, dv = c_s.shape

    @pl.when(ck == 0)
    def _():
        c_s[...] = c0_ref[...]
        n_s[...] = n0_ref[...]
        m_s[...] = m0_ref[...]

    def pad_rows(x):
        if tb == L:
            return x
        return jnp.concatenate([x, jnp.zeros((L - tb, x.shape[1]), x.dtype)], axis=0)

    for bi in range(n_seq):
        _mlstm_sequence(bi, q_ref, k_ref, v_ref, zo_ref, g_ref, bg_ref, gn_ref, h_ref, c_s, n_s, m_s, pad_rows, tb, L)

    @pl.when(ck == pl.num_programs(1) - 1)
    def _():
        co_ref[...] = c_s[...]
        no_ref[...] = n_s[...]
        mo_ref[...] = m_s[...]


def _mlstm_sequence(bi, q_ref, k_ref, v_ref, zo_ref, g_ref, bg_ref, gn_ref, h_ref, c_s, n_s, m_s, pad_rows, tb, L):
    _, n_heads, dk, dv = c_s.shape
    gates = pad_rows(g_ref[bi] + bg_ref[...])
    lane = lax.broadcasted_iota(jnp.int32, (L, LANES), 1)
    li_all = jnp.where(lane < n_heads, gates, 0.0)
    lf_all = jnp.where((lane >= n_heads) & (lane < 2 * n_heads),
                       -(jnp.maximum(-gates, 0.0) + jnp.log1p(jnp.exp(-jnp.abs(gates)))), 0.0)
    if tb != L:
        real = lax.broadcasted_iota(jnp.int32, (L, LANES), 0) < tb
        li_all = jnp.where(real | (lane >= n_heads), li_all, NEG)
        lf_all = jnp.where(real, lf_all, 0.0)
    ri = lax.broadcasted_iota(jnp.int32, (L, L), 0)
    ci = lax.broadcasted_iota(jnp.int32, (L, L), 1)
    tri = ri >= ci
    y = li_all + jnp.dot(tri.astype(F32), lf_all, precision=HIGHEST, preferred_element_type=F32)
    er = lax.broadcasted_iota(jnp.int32, (SUBLANES, LANES), 0)
    el = lax.broadcasted_iota(jnp.int32, (SUBLANES, LANES), 1)
    e = jnp.where(er < n_heads, jnp.where(el == er + n_heads, 1.0, 0.0) - jnp.where(el == er, 1.0, 0.0), 0.0)
    rt = lax.dot_general(e, y, _NT, precision=HIGHEST, preferred_element_type=F32)

    for hd in range(n_heads):
        li = y[:, hd:hd + 1]
        b = y[:, n_heads + hd:n_heads + hd + 1]
        r_row = rt[hd:hd + 1, :]
        q = pad_rows(q_ref[bi, :, hd * dk:(hd + 1) * dk]).astype(F32) * (dk ** -0.5)
        k = pad_rows(k_ref[bi, :, hd * dk:(hd + 1) * dk]).astype(F32)
        v = pad_rows(v_ref[bi, :, hd * dv:(hd + 1) * dv]).astype(F32)
        a = b + m_s[bi, hd][:, :1]
        dmat = jnp.where(tri, b - r_row, NEG)
        mt = jnp.maximum(a, jnp.max(dmat, axis=-1, keepdims=True))
        w_inter = jnp.exp(a - mt)
        qb, kb, vb = q.astype(BF16), k.astype(BF16), v.astype(BF16)
        s = lax.dot_general(qb, kb, _NT, preferred_element_type=F32) * jnp.exp(dmat - mt)
        c_old = c_s[bi, hd]
        n_old = n_s[bi, hd]
        num = (w_inter * jnp.dot(qb, c_old.astype(BF16), preferred_element_type=F32)
               + jnp.dot(s.astype(BF16), vb, preferred_element_type=F32))
        den = w_inter * jnp.sum(q * n_old, axis=-1, keepdims=True) + jnp.sum(s, axis=-1, keepdims=True)
        hh = (num / jnp.maximum(jnp.abs(den), jnp.exp(-mt)))[:tb]
        hn = hh * lax.rsqrt(jnp.mean(hh * hh, axis=-1, keepdims=True) + EPS)
        h_ref[bi, :, hd * dv:(hd + 1) * dv] = (hn * gn_ref[:, hd * dv:(hd + 1) * dv]
                                               * _sigmoid(zo_ref[bi, :, hd * dv:(hd + 1) * dv])).astype(h_ref.dtype)

        m_l = mt[L - 1:L]
        g_inter = jnp.exp(a[L - 1:L] - m_l)
        g_s = jnp.exp(b[L - 1:L] - b + li - m_l)
        c_s[bi, hd] = g_inter * c_old + lax.dot_general(kb, (g_s * v).astype(BF16), _TN, preferred_element_type=F32)
        n_s[bi, hd] = g_inter * n_old + jnp.sum(g_s * k, axis=0, keepdims=True)
        m_s[bi, hd] = jnp.broadcast_to(m_l, (1, LANES))


def _mlstm(q3, k3, v3, o3, g3, bg, gn, c0, n0, m0, *, out_dtype, name):
    bn, t, _ = q3.shape
    _, n_heads, dk, dv = c0.shape
    assert dk == dv and dk % LANES == 0 and 2 * n_heads <= LANES and n_heads <= SUBLANES
    if t % MLSTM_CHUNK == 0:
        tb = chunk = MLSTM_CHUNK
    else:
        assert t <= MLSTM_MIN_CHUNK
        tb, chunk = t, MLSTM_MIN_CHUNK
    n_c = t // tb
    hw = n_heads * dk
    gb = 1 if n_c > 1 else math.gcd(bn, 4)
    tok = lambda b, c: (b, c, 0)
    state = lambda b, c: (b, 0, 0, 0)
    h, c, n, m = pl.pallas_call(
        functools.partial(_mlstm_kernel, tb=tb, chunk=chunk),
        grid=(bn // gb, n_c),
        in_specs=[pl.BlockSpec((gb, tb, hw), tok)] * 4 + [
                  pl.BlockSpec((gb, tb, LANES), tok),
                  pl.BlockSpec((1, LANES), lambda b, c: (0, 0)),
                  pl.BlockSpec((1, hw), lambda b, c: (0, 0)),
                  pl.BlockSpec((gb, n_heads, dk, dv), state), pl.BlockSpec((gb, n_heads, 1, dk), state),
                  pl.BlockSpec((gb, n_heads, 1, LANES), state)],
        out_specs=[pl.BlockSpec((gb, tb, hw), tok),
                   pl.BlockSpec((gb, n_heads, dk, dv), state), pl.BlockSpec((gb, n_heads, 1, dk), state),
                   pl.BlockSpec((gb, n_heads, 1, LANES), state)],
        out_shape=[jax.ShapeDtypeStruct((bn, t, hw), out_dtype),
                   jax.ShapeDtypeStruct((bn, n_heads, dk, dv), F32),
                   jax.ShapeDtypeStruct((bn, n_heads, 1, dk), F32),
                   jax.ShapeDtypeStruct((bn, n_heads, 1, LANES), F32)],
        scratch_shapes=[pltpu.VMEM((gb, n_heads, dk, dv), F32), pltpu.VMEM((gb, n_heads, 1, dk), F32),
                        pltpu.VMEM((gb, n_heads, 1, LANES), F32)],
        compiler_params=_cparams(2), name=name,
    )(q3, k3, v3, o3, g3, bg, gn, c0, n0.reshape(bn, n_heads, 1, dk),
      jnp.broadcast_to(m0[:, :, None, None], (bn, n_heads, 1, LANES)))
    return h, c, n.reshape(bn, n_heads, dk), m[:, :, 0, 0]


def _convmod_kernel(a_ref, g_ref, buf_ref, w_ref, b_ref, lg_ref, lb_ref, y_ref, nb_ref, u_s, *, tt, width, rc):
    t = pl.program_id(1)
    hist = width - 1
    bias = b_ref[...]
    lg = lg_ref[...]
    lb = lb_ref[...]
    n_p = rc // SUBLANES
    for bi in range(a_ref.shape[0]):
        @pl.when(t == 0)
        def _():
            buf = buf_ref[bi]
            for r in range(SUBLANES):
                u_s[bi, r, 0:hist - r, :] = buf[r:, :]

        @pl.when(t > 0)
        def _():
            for r in range(SUBLANES):
                u_s[bi, r, 0:CONV_HALO, :] = u_s[bi, r, tt:tt + CONV_HALO, :]

        u = a_ref[bi] * _sigmoid(g_ref[bi])
        for r in range(SUBLANES):
            u_s[bi, r, hist - r:hist - r + tt, :] = u
        for r0 in range(0, tt, rc):
            accs = [jnp.broadcast_to(bias, (SUBLANES, bias.shape[1]))] * n_p
            for j in range(width):
                r = j % SUBLANES
                wj = w_ref[j]
                accs = [acc + wj * u_s[bi, r, r0 + p * SUBLANES + j - r:r0 + (p + 1) * SUBLANES + j - r, :]
                        for p, acc in enumerate(accs)]
            acc = accs[0] if n_p == 1 else jnp.concatenate(accs, axis=0)
            mu = jnp.mean(acc, axis=-1, keepdims=True)
            xc = acc - mu
            var = jnp.mean(xc * xc, axis=-1, keepdims=True)
            y = xc * lax.rsqrt(var + LN_EPS) * lg + lb
            y_ref[bi, r0:r0 + rc, :] = (y * _sigmoid(y)).astype(y_ref.dtype)

        @pl.when(t == pl.num_programs(1) - 1)
        def _():
            nb_ref[bi] = u_s[bi, 0, tt:tt + hist, :]


def _conv_module(a3, g3, cbuf, conv_w, conv_b, ln_g, ln_b, *, out_dtype, name):
    bn, t, _ = a3.shape
    width, ch = conv_w.shape
    assert width - 1 <= CONV_HALO
    tt = 256 if t % 256 == 0 else t
    n_t = t // tt
    assert n_t == 1 or tt >= CONV_HALO
    gb = 1 if n_t > 1 else math.gcd(bn, SUBLANES)
    y, nb = pl.pallas_call(
        functools.partial(_convmod_kernel, tt=tt, width=width,
                          rc=4 * SUBLANES if tt % (4 * SUBLANES) == 0 else SUBLANES),
        grid=(bn // gb, n_t),
        in_specs=[pl.BlockSpec((gb, tt, ch), lambda b, i: (b, i, 0)),
                  pl.BlockSpec((gb, tt, ch), lambda b, i: (b, i, 0)),
                  pl.BlockSpec((gb, width - 1, ch), lambda b, i: (b, 0, 0)),
                  pl.BlockSpec((width, SUBLANES, ch), lambda b, i: (0, 0, 0)),
                  pl.BlockSpec((1, ch), lambda b, i: (0, 0)),
                  pl.BlockSpec((1, ch), lambda b, i: (0, 0)),
                  pl.BlockSpec((1, ch), lambda b, i: (0, 0))],
        out_specs=[pl.BlockSpec((gb, tt, ch), lambda b, i: (b, i, 0)),
                   pl.BlockSpec((gb, width - 1, ch), lambda b, i: (b, 0, 0))],
        out_shape=[jax.ShapeDtypeStruct((bn, t, ch), out_dtype),
                   jax.ShapeDtypeStruct((bn, width - 1, ch), F32)],
        scratch_shapes=[pltpu.VMEM((gb, SUBLANES, CONV_HALO + tt, ch), F32)],
        compiler_params=_cparams(2), name=name,
    )(a3, g3, cbuf, jnp.broadcast_to(conv_w[:, None, :], (width, SUBLANES, ch)), conv_b.reshape(1, ch),
      ln_g.reshape(1, ch), ln_b.reshape(1, ch))
    return y, nb


def _ffn_up_kernel(x_ref, sh_ref, sc_ref, wg_ref, wu_ref, buf_ref, cw_ref, cb_ref, act_ref, gl_ref, h_s, carry_s,
                   *, tpb, rows, strip):
    ti = pl.program_id(0) % tpb
    j = pl.program_id(1)

    @pl.when(j == 0)
    def _():
        x = x_ref[...]
        r = lax.rsqrt(jnp.mean(x * x, axis=-1, keepdims=True) + EPS)
        h = (x * r) * (1.0 + sc_ref[...]) + sh_ref[...]
        h_s[...] = h.reshape(h_s.shape).astype(BF16)

    hist = buf_ref.shape[1]
    if tpb > 1:
        @pl.when(ti == 0)
        def _():
            carry_s[j] = jnp.zeros(carry_s.shape[1:], F32)
            carry_s[j, 0:hist, :] = buf_ref[0]

    h = h_s[...]
    n = h.shape[0]
    gb = n // rows
    tn = wg_ref.shape[1]
    strips = [slice(c0, c0 + strip) for c0 in range(0, tn, strip)]
    gs = [jnp.dot(h, wg_ref[:, cs], preferred_element_type=F32) for cs in strips]
    us = [jnp.dot(h, wu_ref[:, cs], preferred_element_type=F32) for cs in strips]
    for cs, g, u in zip(strips, gs, us):
        if tpb > 1:
            prev = carry_s[j, :, cs]
            carry_s[j, :, cs] = pltpu.roll(g[n - SUBLANES:], 2, axis=0)
        r8 = lax.broadcasted_iota(jnp.int32, (gb * SUBLANES, strip), 0) % SUBLANES
        if tpb == 1:
            rows01 = [jnp.broadcast_to(buf_ref[:, i:i + 1, cs], (gb, SUBLANES, strip)).reshape(gb * SUBLANES, strip)
                      for i in range(hist)]
            prev = jnp.where(r8 == 0, rows01[0], rows01[1])
        prev1 = pltpu.roll(prev, gb * SUBLANES - 1, axis=0)
        gm1 = pltpu.roll(g, 1, axis=0)
        gm2 = pltpu.roll(g, 2, axis=0)
        if rows == SUBLANES:
            gm1 = jnp.where(r8 < 1, prev1, gm1)
            gm2 = jnp.where(r8 < 2, prev, gm2)
            gl_ref[:, :, cs] = g.reshape(gb, SUBLANES, strip)
        else:
            gm1 = jnp.concatenate([jnp.where(r8 < 1, prev1, gm1[:SUBLANES]), gm1[SUBLANES:]], axis=0)
            gm2 = jnp.concatenate([jnp.where(r8 < 2, prev, gm2[:SUBLANES]), gm2[SUBLANES:]], axis=0)
            gl_ref[:, :, cs] = g[n - SUBLANES:].reshape(gb, SUBLANES, strip)
        gc = cw_ref[0:1, cs] * gm2 + cw_ref[1:2, cs] * gm1 + cw_ref[2:3, cs] * g + cb_ref[:, cs]
        act_ref[:, cs] = ((gc * _sigmoid(gc)) * u).astype(act_ref.dtype)


def _ffn_up(x3, mod, w, layer, buf, conv_w, conv_b, *, tm=1024, tn=512, name="ffn_up"):
    bn, t, d = x3.shape
    dff = w.shape[2] // 2
    gb, rows, n_i, tpb = _row_tiling(bn, t, tm)
    tmm = gb * rows
    tn = _col_tile(dff, tn)
    n_f = dff // tn
    hist = buf.shape[1]
    assert conv_w.shape[0] == 3 and hist == 2 and (tpb == 1 or gb == 1)
    act, g_tail = pl.pallas_call(
        functools.partial(_ffn_up_kernel, tpb=tpb, rows=rows, strip=min(tn, 256)),
        grid=(n_i, n_f),
        in_specs=[pl.BlockSpec((gb, rows, d), lambda i, j: (i // tpb, i % tpb, 0)),
                  _mod_spec(mod, mod[1], gb, tpb), _mod_spec(mod, mod[1] + 1, gb, tpb),
                  pl.BlockSpec((None, d, tn), lambda i, j: (layer, 0, j)),
                  pl.BlockSpec((None, d, tn), lambda i, j: (layer, 0, j + n_f)),
                  pl.BlockSpec((gb, hist, tn), lambda i, j: (i // tpb, 0, j)),
                  pl.BlockSpec((3, tn), lambda i, j: (0, j)),
                  pl.BlockSpec((1, tn), lambda i, j: (0, j))],
        out_specs=[pl.BlockSpec((tmm, tn), lambda i, j: (i, j)),
                   pl.BlockSpec((gb, SUBLANES, tn), lambda i, j: (i, 0, j))],
        out_shape=[jax.ShapeDtypeStruct((bn * t, dff), BF16),
                   jax.ShapeDtypeStruct((n_i * gb, SUBLANES, dff), F32)],
        scratch_shapes=[pltpu.VMEM((tmm, d), BF16), pltpu.VMEM((n_f, SUBLANES, tn), F32)],
        compiler_params=_cparams(2), name=name,
    )(x3, mod[0], mod[0], w, w, buf, conv_w, conv_b.reshape(1, dff))
    return act, g_tail.reshape(bn, tpb, SUBLANES, dff)[:, tpb - 1]


def _t5_bucket(dist):
    n = jnp.maximum(dist, 0)
    exact = REL_BUCKETS // 2
    large = exact + (jnp.log(jnp.maximum(n, 1).astype(F32) / exact)
                     / math.log(REL_MAX_DIST / exact) * (REL_BUCKETS - exact)).astype(jnp.int32)
    return jnp.where(n < exact, n, jnp.minimum(large, REL_BUCKETS - 1))


def _bias_lookup(dist, table_col):
    bucket = _t5_bucket(dist)
    bias = jnp.zeros(dist.shape, F32)
    for kk in range(REL_BUCKETS):
        bias = jnp.where(bucket == kk, table_col(kk), bias)
    return bias * LOG2E


def _bias_prompt_kernel(rb_ref, o_ref, *, blk):
    rb = rb_ref[0]
    key = lax.broadcasted_iota(jnp.int32, (blk, blk), 0)
    qry = lax.broadcasted_iota(jnp.int32, (blk, blk), 1)
    col = lambda kk: rb[:, kk:kk + 1]
    own = _bias_lookup(qry - key, col)
    o_ref[0, 0] = jnp.where(qry >= key, own, NEG)
    o_ref[0, 1] = _bias_lookup(blk + qry - key, col)


def _bias_prompt(rel_bias, blk):
    n_b, n_h = rel_bias.shape
    rbt = jnp.pad(rel_bias.T, ((0, 0), (0, LANES - n_b))).reshape(n_h, 1, LANES)
    return pl.pallas_call(
        functools.partial(_bias_prompt_kernel, blk=blk),
        grid=(n_h,),
        in_specs=[pl.BlockSpec((1, 1, LANES), lambda h: (h, 0, 0))],
        out_specs=pl.BlockSpec((1, 2, blk, blk), lambda h: (h, 0, 0, 0)),
        out_shape=jax.ShapeDtypeStruct((n_h, 2, blk, blk), F32),
        compiler_params=_cparams(1), name="bias_prompt",
    )(rbt), rbt


def _bias_sample_kernel(rb_ref, fl_ref, new_ref, *, n_heads, t, page):
    rb = rb_ref[...]
    col = lambda kk: rb[:, kk:kk + 1]
    for ref, base in ((fl_ref.at[0], 2 * page), (fl_ref.at[1], page), (new_ref, 0)):
        shape = ref.shape
        r = lax.broadcasted_iota(jnp.int32, shape, 0)
        c = lax.broadcasted_iota(jnp.int32, shape, 1)
        if ref is new_ref:
            key_head, key_off = c // t, c % t
        else:
            key_head, key_off = c % n_heads, c // n_heads
        dist = base + r % t - key_off
        ok = (r // t == key_head) & (dist >= 0)
        ref[...] = jnp.where(ok, _bias_lookup(dist, col), NEG)


def _bias_sample(rel_bias, t, page):
    n_b, n_h = rel_bias.shape
    rows = t * n_h
    rb = jnp.pad(jnp.repeat(rel_bias.T, t, axis=0), ((0, 0), (0, LANES - n_b)))
    return pl.pallas_call(
        functools.partial(_bias_sample_kernel, n_heads=n_h, t=t, page=page),
        out_shape=[jax.ShapeDtypeStruct((2, rows, page * n_h), F32),
                   jax.ShapeDtypeStruct((rows, rows), F32)],
        compiler_params=pltpu.CompilerParams(vmem_limit_bytes=VMEM_LIMIT_BYTES), name="bias_sample",
    )(rb)


def _attn_prompt_kernel(q_ref, k_ref, v_ref, bias_ref, rb_ref, o_ref, km_s, kb_s, vt_s, s_s, *, blk, n_blk, scale):
    km_s[...] = jnp.zeros_like(km_s)
    for j in range(n_blk):
        kj = k_ref[0, j * blk:(j + 1) * blk, :]
        km_s[j:j + 1, :] = jnp.mean(kj, axis=0, keepdims=True)
        kb_s[j * blk:(j + 1) * blk, :] = kj.astype(BF16)
        vt_s[:, j * blk:(j + 1) * blk] = v_ref[0, j * blk:(j + 1) * blk, :].T.astype(BF16)

    far = rb_ref[0][:, REL_BUCKETS - 1:REL_BUCKETS] * LOG2E
    km = km_s[...]
    row = lax.broadcasted_iota(jnp.int32, (km.shape[0], blk), 0)
    tile = 0
    for c in range(n_blk):
        qf = q_ref[0, c * blk:(c + 1) * blk, :]
        qb = qf.astype(BF16)
        add = [0.0] * c
        if c > MOBA_TOPK:
            sc = lax.dot_general(km, qf, _NT, precision=HIGHEST, preferred_element_type=F32)
            sc = jnp.where(row < c, sc, -jnp.inf)
            for j in range(c):
                sj = sc[j:j + 1, :]
                ahead = (sc > sj) | ((sc == sj) & (row < j))
                rank = jnp.sum(jnp.where(ahead, 1.0, 0.0), axis=0, keepdims=True)
                add[j] = jnp.where(rank < MOBA_TOPK, 0.0, NEG)
        m = None
        for j in range(c + 1):
            st = lax.dot_general(kb_s[j * blk:(j + 1) * blk, :], qb, _NT, preferred_element_type=F32) * (scale * LOG2E)
            if j == c:
                st = st + bias_ref[0, 0]
            elif j == c - 1:
                st = st + (bias_ref[0, 1] + add[j])
            else:
                st = st + (add[j] + far)
            s_s[tile + j] = st
            mj = jnp.max(st, axis=0, keepdims=True)
            m = mj if m is None else jnp.maximum(m, mj)
        l = jnp.zeros_like(m)
        acc = jnp.zeros((vt_s.shape[0], blk), F32)
        for j in range(c + 1):
            p = jnp.exp2(s_s[tile + j] - m)
            l = l + jnp.sum(p, axis=0, keepdims=True)
            acc = acc + jnp.dot(vt_s[:, j * blk:(j + 1) * blk], p.astype(BF16), preferred_element_type=F32)
        o_ref[0, c * blk:(c + 1) * blk, :] = (acc / l).T.astype(o_ref.dtype)
        tile += c + 1


def _attn_prompt(q3, k3, v3, bias_tiles, rbt, *, out_dtype):
    bn, s, _ = q3.shape
    n_h = bias_tiles.shape[0]
    dh = q3.shape[2] // n_h
    blk = MOBA_BLOCK
    assert s % blk == 0 and dh == LANES and blk >= REL_MAX_DIST
    n_blk = s // blk
    head = lambda b, h: (b, 0, h)
    return pl.pallas_call(
        functools.partial(_attn_prompt_kernel, blk=blk, n_blk=n_blk, scale=dh ** -0.5),
        grid=(bn, n_h),
        in_specs=[pl.BlockSpec((1, s, dh), head), pl.BlockSpec((1, s, dh), head), pl.BlockSpec((1, s, dh), head),
                  pl.BlockSpec((1, 2, blk, blk), lambda b, h: (h, 0, 0, 0)),
                  pl.BlockSpec((1, 1, LANES), lambda b, h: (h, 0, 0))],
        out_specs=pl.BlockSpec((1, s, dh), head),
        out_shape=jax.ShapeDtypeStruct(q3.shape, out_dtype),
        scratch_shapes=[pltpu.VMEM((-(-n_blk // SUBLANES) * SUBLANES, dh), F32), pltpu.VMEM((s, dh), BF16),
                        pltpu.VMEM((dh, s), BF16), pltpu.VMEM((n_blk * (n_blk + 1) // 2, blk, blk), F32)],
        compiler_params=_cparams(2), name="moba_prompt",
    )(q3, k3, v3, bias_tiles, rbt)


def _attn_sample_kernel(pt_ref, q_ref, kn_ref, vn_ref, *refs, n_heads, ppb, pps, scale):
    del pt_ref
    kp_refs, vp_refs = refs[:pps], refs[pps:2 * pps]
    bias_ref, new_ref, o_ref, ks_s, m_s, l_s, o_s = refs[2 * pps:]
    g = pl.program_id(1)
    last = pl.num_programs(1) - 1
    n_tok = q_ref.shape[1]
    dh = q_ref.shape[2] // n_heads

    def head_major(ref):
        return jnp.concatenate([ref[0, :, h * dh:(h + 1) * dh] for h in range(n_heads)], axis=0)

    qf = head_major(q_ref)
    qb = qf.astype(BF16)
    rows = n_heads * n_tok

    def partial_softmax(kk, vv, bias):
        s = lax.dot_general(qb, kk.astype(BF16), _NT, preferred_element_type=F32) * (scale * LOG2E) + bias
        m = jnp.max(s, axis=-1, keepdims=True)
        e = jnp.exp2(s - m)
        return m, jnp.sum(e, axis=-1, keepdims=True), jnp.dot(e.astype(BF16), vv.astype(BF16),
                                                              preferred_element_type=F32)

    for r in range(pps):
        pg = g * pps + r
        kp = kp_refs[r][0]
        ks_s[pg] = jnp.sum(kp.reshape(kp.shape[0] // n_heads, n_heads, dh), axis=0)
        bias = bias_ref[jnp.where(g == last, 1, 0)] if r == pps - 1 else bias_ref[0]
        m, l, o = partial_softmax(kp, vp_refs[r][0], bias)
        m_s[pg] = m
        l_s[pg] = l
        o_s[pg] = o

    @pl.when(g == last)
    def _():
        n_pg = m_s.shape[0]
        n_blk = n_pg // ppb
        inv = 1.0 / (ppb * (kp_refs[0].shape[1] // n_heads))
        scs = []
        for j in range(n_blk):
            km = ks_s[j * ppb]
            for pp in range(1, ppb):
                km = km + ks_s[j * ppb + pp]
            km = jnp.broadcast_to((km * inv)[:, None, :], (n_heads, n_tok, dh)).reshape(rows, dh)
            scs.append(jnp.sum(qf * km, axis=-1, keepdims=True))
        sel = []
        for j in range(n_blk):
            rank = jnp.zeros((rows, 1), F32)
            for jj in range(n_blk):
                if jj != j:
                    ahead = (scs[jj] >= scs[j]) if jj < j else (scs[jj] > scs[j])
                    rank = rank + jnp.where(ahead, 1.0, 0.0)
            sel.append(rank < MOBA_TOPK)
        m_n, l_n, o_n = partial_softmax(head_major(kn_ref), head_major(vn_ref), new_ref[...])
        m_all = m_n
        for pg in range(n_pg):
            m_all = jnp.maximum(m_all, jnp.where(sel[pg // ppb], m_s[pg], NEG))
        w_n = jnp.exp2(m_n - m_all)
        l_all = w_n * l_n
        o_all = w_n * o_n
        for pg in range(n_pg):
            w = jnp.where(sel[pg // ppb], jnp.exp2(m_s[pg] - m_all), 0.0)
            l_all = l_all + w * l_s[pg]
            o_all = o_all + w * o_s[pg]
        o = (o_all / l_all).astype(o_ref.dtype)
        for h in range(n_heads):
            o_ref[0, :, h * dh:(h + 1) * dh] = o[h * n_tok:(h + 1) * n_tok]


def _attn_sample(q, k, v, cache_k, cache_v, page_table, far_last, new, *, out_dtype, pps=8):
    db, t, hd = q.shape
    n_pool, page, n_h, dh = cache_k.shape
    n_pages = page_table.shape[1]
    rows = t * n_h
    assert MOBA_BLOCK % page == 0 and (n_pages * page) % MOBA_BLOCK == 0 and t <= MOBA_BLOCK
    assert page >= REL_MAX_DIST and n_pages * page // MOBA_BLOCK > MOBA_TOPK and dh == LANES
    assert n_pages % pps == 0
    ppb = MOBA_BLOCK // page
    seq = lambda b, g, pt: (b, 0, 0)
    paged = [(lambda b, g, pt, r=r: (pt[b, g * pps + r], 0, 0)) for r in range(pps)]
    page_spec = lambda r: pl.BlockSpec((1, page * n_h, dh), paged[r])
    grid_spec = pltpu.PrefetchScalarGridSpec(
        num_scalar_prefetch=1,
        grid=(db, n_pages // pps),
        in_specs=([pl.BlockSpec((1, t, hd), seq)] * 3 + [page_spec(r) for r in range(pps)] * 2
                  + [pl.BlockSpec(far_last.shape, lambda b, g, pt: (0, 0, 0)),
                     pl.BlockSpec(new.shape, lambda b, g, pt: (0, 0))]),
        out_specs=pl.BlockSpec((1, t, hd), seq),
        scratch_shapes=[pltpu.VMEM((n_pages, n_h, dh), F32), pltpu.VMEM((n_pages, rows, 1), F32),
                        pltpu.VMEM((n_pages, rows, 1), F32), pltpu.VMEM((n_pages, rows, dh), F32)],
    )
    ck = cache_k.reshape(n_pool, page * n_h, dh)
    cv = cache_v.reshape(n_pool, page * n_h, dh)
    return pl.pallas_call(
        functools.partial(_attn_sample_kernel, n_heads=n_h, ppb=ppb, pps=pps, scale=dh ** -0.5),
        grid_spec=grid_spec,
        out_shape=jax.ShapeDtypeStruct((db, t, hd), out_dtype),
        compiler_params=_cparams(2), name="moba_sample",
    )(page_table, q, k, v, *([ck] * pps), *([cv] * pps), far_last, new)


def _trunk(x, mods, row0, mc, mn, mm, cb, fb, wts, attend):
    bn, t, d = x.shape
    depth = wts["ffn_up"].shape[0]
    n_ha = mc.shape[2]
    qkw = n_ha * mc.shape[3]
    aw = n_ha * mc.shape[4]
    bw = wts["conv_w"].shape[2]
    big = t >= 512
    act_dtype = BF16 if big else F32
    out_c, out_n, out_m, out_conv, out_k, out_v, out_ffn = [], [], [], [], [], [], []
    for l in range(depth):
        mod = (mods, 3 * (2 * l), row0)
        if l % 2 == 0:
            e = l // 2
            assert qkw == aw == bw
            w_a, w_b = wts["in_even"], wts["in_even_b"]
            secs = ([(w_a, e, i * qkw, None, act_dtype) for i in range(3)] + [(w_a, e, 3 * qkw, None, F32)]
                    + [(w_b, e, i * bw, None, F32) for i in range(2)])
            *zs, g = _nm_matmul(x, mod, secs, qkw, narrow=(wts["in_even_gate"], e), tn=256 if big else 128,
                                name="in_even")
            zq, zk, zv, zo, za, zg = (z.reshape(bn, t, qkw) for z in zs)
            h_a, c_new, n_new, m_new = _mlstm(zq, zk, zv, zo, g.reshape(bn, t, LANES), wts["b_gates"][e],
                                              wts["mlstm_norm_g"][e], mc[e], mn[e], mm[e], out_dtype=act_dtype,
                                              name="mlstm")
            y_b, cb_new = _conv_module(za, zg, cb[e], wts["conv_w"][e], wts["conv_b"][e],
                                       wts["conv_ln_g"][e], wts["conv_ln_b"][e], out_dtype=act_dtype, name="conv_module")
            x = _mm_res([h_a.reshape(bn * t, aw), y_b.reshape(bn * t, bw)],
                        [(wts["out_even"], e, 0), (wts["out_even"], e, aw)], x, mod, name="out_even")
            out_c.append(c_new); out_n.append(n_new); out_m.append(m_new); out_conv.append(cb_new)
        else:
            o = l // 2
            w = wts["in_odd"]
            qkv = _nm_matmul(x, mod, [(w, o, 0, wts["q_norm_g"][o], F32), (w, o, d, wts["k_norm_g"][o], F32),
                                      (w, o, 2 * d, None, F32)], d, tn=512 if big else 256, name="in_odd")
            q, k, v = (a.reshape(bn, t, d) for a in qkv)
            a = attend(o, q, k, v, act_dtype)
            x = _mm_res([a.reshape(bn * t, d)], [(wts["out_odd"], o, 0)], x, mod, name="out_odd")
            out_k.append(k); out_v.append(v)
        mod = (mods, 3 * (2 * l + 1), row0)
        hist = fb[l].shape[1]
        assert hist <= min(t, SUBLANES)
        act, g_last = _ffn_up(x, mod, wts["ffn_up"], l, fb[l], wts["ffn_conv_w"][l], wts["ffn_conv_b"][l])
        x = _mm_res([act], [(wts["ffn_down"], l, 0)], x, mod, tn=512, name="ffn_down")
        out_ffn.append(g_last[:, SUBLANES - hist:])
    return (x, jnp.stack(out_c), jnp.stack(out_n), jnp.stack(out_m), jnp.stack(out_conv),
            jnp.stack(out_k), jnp.stack(out_v), jnp.stack(out_ffn))


def kernel(x_prompt, x_sample, state_mlstm_c, state_mlstm_n, state_mlstm_m, state_conv, cache_k, cache_v, page_table, state_ffn, c_prompt, c_sample, ada_w, ada_b, w_in_even, b_gates, mlstm_norm_g, conv_w, conv_b, conv_ln_g, conv_ln_b, w_out_even, w_in_odd, q_norm_g, k_norm_g, rel_bias, w_out_odd, ffn_w_up, ffn_conv_w, ffn_conv_b, ffn_w_down):
    bsz, seq, d = x_prompt.shape
    db, dt, _ = x_sample.shape
    depth = ada_w.shape[0]
    n_even, _, n_ha, dk, dv = state_mlstm_c.shape
    n_hc, dh = cache_k.shape[3], cache_k.shape[4]
    qkw, aw, bw = n_ha * dk, n_ha * dv, conv_w.shape[2]
    n_gate = 2 * n_ha
    o_gate = 2 * qkw + 2 * aw
    dt_x = x_prompt.dtype

    wts = {
        "in_even": w_in_even.astype(BF16),
        "in_even_b": w_in_even[:, :, o_gate + n_gate:].astype(BF16),
        "in_even_gate": jnp.pad(w_in_even[:, :, o_gate:o_gate + n_gate],
                                ((0, 0), (0, 0), (0, LANES - n_gate))).astype(BF16),
        "b_gates": jnp.pad(b_gates, ((0, 0), (0, LANES - n_gate))).reshape(n_even, 1, LANES),
        "mlstm_norm_g": mlstm_norm_g.reshape(n_even, 1, aw),
        "conv_w": conv_w, "conv_b": conv_b, "conv_ln_g": conv_ln_g, "conv_ln_b": conv_ln_b,
        "out_even": w_out_even.astype(BF16),
        "in_odd": w_in_odd.astype(BF16), "q_norm_g": q_norm_g, "k_norm_g": k_norm_g,
        "out_odd": w_out_odd.astype(BF16),
        "ffn_up": ffn_w_up.astype(BF16), "ffn_conv_w": ffn_conv_w, "ffn_conv_b": ffn_conv_b,
        "ffn_down": ffn_w_down.astype(BF16),
    }

    mods = _adaln_mods(jnp.concatenate([c_sample, c_prompt], axis=0), ada_w, ada_b)

    bias_tiles, rbt = _bias_prompt(rel_bias, MOBA_BLOCK)
    far_last, new = _bias_sample(rel_bias, dt, cache_k.shape[2])

    def prompt_attend(o, q, k, v, out_dtype):
        return _attn_prompt(q, k, v, bias_tiles, rbt, out_dtype=out_dtype)

    def sample_attend(o, q, k, v, out_dtype):
        return _attn_sample(q, k, v, cache_k[o], cache_v[o], page_table, far_last, new, out_dtype=out_dtype)

    zc = jnp.zeros((n_even, bsz, n_ha, dk, dv), dt_x)
    zn = jnp.zeros((n_even, bsz, n_ha, dk), dt_x)
    zm = jnp.zeros((n_even, bsz, n_ha), dt_x)
    zconv = jnp.zeros((n_even, bsz) + state_conv.shape[2:], dt_x)
    zffn = jnp.zeros((depth, bsz) + state_ffn.shape[2:], dt_x)

    y_p, p_c, p_n, p_m, p_conv, p_k, p_v, p_ffn = _trunk(
        x_prompt, mods, db, zc, zn, zm, zconv, zffn, wts, prompt_attend)
    y_s, s_c, s_n, s_m, s_conv, s_k, s_v, s_ffn = _trunk(
        x_sample, mods, 0, state_mlstm_c, state_mlstm_n, state_mlstm_m, state_conv, state_ffn, wts, sample_attend)

    def heads(a):
        return a.reshape(a.shape[:3] + (n_hc, dh))

    return (y_p, y_s, p_c, p_n, p_m, p_conv, heads(p_k), heads(p_v), p_ffn,
            s_c, s_n, s_m, s_conv, heads(s_k), heads(s_v), s_ffn)
```

```python
import functools
import math

import jax
import jax.numpy as jnp
from jax import lax
from jax.experimental import pallas as pl
from jax.experimental.pallas import tpu as pltpu

F32 = jnp.float32
BF16 = jnp.bfloat16
HIGHEST = lax.Precision.HIGHEST

LANES = 128
SUBLANES = 8
VMEM_LIMIT_BYTES = 56 * 1024 * 1024

EPS = 1e-6
LN_EPS = 1e-5
MLSTM_CHUNK = 256
MLSTM_MIN_CHUNK = 32
MOBA_BLOCK = 256
MOBA_TOPK = 3
REL_BUCKETS = 32
REL_MAX_DIST = 128
NEG = -1e30
LOG2E = math.log2(math.e)
CONV_HALO = 32

_NT = (((1,), (1,)), ((), ()))
_TN = (((0,), (0,)), ((), ()))


def _cparams(n_axes):
    return pltpu.CompilerParams(dimension_semantics=("arbitrary",) * n_axes, vmem_limit_bytes=VMEM_LIMIT_BYTES)


def _sigmoid(x):
    return jax.nn.sigmoid(x)


def _col_tile(n, pref):
    tile = min(pref, n)
    while n % tile:
        tile -= LANES
    return tile


def _row_tiling(bn, t, tm):
    if t >= tm:
        assert t % tm == 0
        tpb = t // tm
        return 1, tm, bn * tpb, tpb
    gb = min(bn, tm // t)
    assert bn % gb == 0 and t % SUBLANES == 0
    return gb, t, bn // gb, 1


def _mod_kernel(c_ref, w_ref, b_ref, o_ref):
    c = c_ref[...]
    s = (c * _sigmoid(c)).astype(BF16)
    mod = jnp.dot(s, w_ref[...].astype(BF16), preferred_element_type=F32) + b_ref[...]
    o_ref[...] = mod.reshape(o_ref.shape)


def _adaln_mods(c_all, ada_w, ada_b, tn=512):
    depth, two, d, n3 = ada_w.shape
    nlj = depth * two
    n_parts = n3 // d
    rows = c_all.shape[0]
    tn = _col_tile(d, tn)
    npj = d // tn
    return pl.pallas_call(
        _mod_kernel,
        grid=(nlj, n3 // tn),
        in_specs=[pl.BlockSpec((rows, d), lambda l, j: (0, 0)),
                  pl.BlockSpec((None, d, tn), lambda l, j: (l, 0, j)),
                  pl.BlockSpec((None, 1, tn), lambda l, j: (l, 0, j))],
        out_specs=pl.BlockSpec((None, rows, 1, tn), lambda l, j: (l * n_parts + j // npj, 0, 0, j % npj)),
        out_shape=jax.ShapeDtypeStruct((nlj * n_parts, rows, 1, d), F32),
        compiler_params=_cparams(2), name="adaln_mods",
    )(c_all, ada_w.reshape(nlj, d, n3), ada_b.reshape(nlj, 1, n3))


def _mod_spec(mod, part, gb, tpb, width=None, by_col=False):
    mods, _, row0 = mod
    assert row0 % gb == 0
    width = width or mods.shape[3]
    return pl.BlockSpec((None, gb, 1, width), lambda i, j: (part, i // tpb + row0 // gb, 0, j if by_col else 0))


def _nm_matmul_kernel(x_ref, sh_ref, sc_ref, *rest, normed, narrow):
    n_s, n_g = len(normed), sum(normed)
    w_refs, g_refs, rest = rest[:n_s], rest[n_s:n_s + n_g], rest[n_s + n_g:]
    if narrow:
        wn_ref, rest = rest[0], rest[1:]
    o_refs, h_ref = rest[:n_s], rest[-1]

    @pl.when(pl.program_id(1) == 0)
    def _():
        x = x_ref[...]
        r = lax.rsqrt(jnp.mean(x * x, axis=-1, keepdims=True) + EPS)
        h = (x * r) * (1.0 + sc_ref[...]) + sh_ref[...]
        h_ref[...] = h.reshape(h_ref.shape).astype(BF16)
        if narrow:
            rest[n_s][...] = jnp.dot(h_ref[...], wn_ref[...], preferred_element_type=F32)

    h = h_ref[...]
    g_iter = iter(g_refs)
    for w_ref, o_ref, has_g in zip(w_refs, o_refs, normed):
        acc = jnp.dot(h, w_ref[...], preferred_element_type=F32)
        if not has_g:
            o_ref[...] = acc.astype(o_ref.dtype)
            continue
        g = next(g_iter)[...]
        for s in range(acc.shape[1] // LANES):
            y = acc[:, s * LANES:(s + 1) * LANES]
            y = y * lax.rsqrt(jnp.mean(y * y, axis=-1, keepdims=True) + EPS) * g
            o_ref[:, s * LANES:(s + 1) * LANES] = y.astype(o_ref.dtype)


def _nm_matmul(x3, mod, sections, width, *, narrow=None, tm=1024, tn=1024, name="nm_matmul"):
    bn, t, d = x3.shape
    gb, rows, n_i, tpb = _row_tiling(bn, t, tm)
    tmm = gb * rows
    n_s = len(sections)
    tn = _col_tile(math.gcd(width, *[sec[2] for sec in sections if sec[2]]), tn)
    n_j = width // tn
    gains = [sec[3].reshape(1, LANES) for sec in sections if sec[3] is not None]
    in_specs = ([pl.BlockSpec((gb, rows, d), lambda i, j: (i // tpb, i % tpb, 0)),
                 _mod_spec(mod, mod[1], gb, tpb), _mod_spec(mod, mod[1] + 1, gb, tpb)]
                + [pl.BlockSpec((None, d, tn), lambda i, j, l=l, j0=c0 // tn: (l, 0, j + j0))
                   for _, l, c0, _, _ in sections]
                + [pl.BlockSpec((1, LANES), lambda i, j: (0, 0))] * len(gains))
    out_specs = [pl.BlockSpec((tmm, tn), lambda i, j: (i, j))] * n_s
    out_shape = [jax.ShapeDtypeStruct((bn * t, width), sec[4]) for sec in sections]
    extra = []
    if narrow is not None:
        wn, ln = narrow
        in_specs.append(pl.BlockSpec((None, d, LANES), lambda i, j: (ln, 0, 0)))
        out_specs.append(pl.BlockSpec((tmm, LANES), lambda i, j: (i, 0)))
        out_shape.append(jax.ShapeDtypeStruct((bn * t, LANES), F32))
        extra = [wn]
    return pl.pallas_call(
        functools.partial(_nm_matmul_kernel, normed=[sec[3] is not None for sec in sections],
                          narrow=narrow is not None),
        grid=(n_i, n_j),
        in_specs=in_specs,
        out_specs=out_specs,
        out_shape=out_shape,
        scratch_shapes=[pltpu.VMEM((tmm, d), BF16)],
        compiler_params=_cparams(2), name=name,
    )(x3, mod[0], mod[0], *[sec[0] for sec in sections], *gains, *extra)


def _mm_res_kernel(*refs, n_in):
    a_refs, w_refs = refs[:n_in], refs[n_in:2 * n_in]
    x_ref, gate_ref, o_ref = refs[2 * n_in:]
    acc = None
    for a_ref, w_ref in zip(a_refs, w_refs):
        part = jnp.dot(a_ref[...].astype(BF16), w_ref[...], preferred_element_type=F32)
        acc = part if acc is None else acc + part
    x = x_ref[...]
    o_ref[...] = x + gate_ref[...] * acc.reshape(x.shape)


def _mm_res(a_list, w_list, x3, mod, *, tm=1024, tn=1024, name="mm_res"):
    bn, t, d = x3.shape
    gb, rows, n_i, tpb = _row_tiling(bn, t, tm)
    tmm = gb * rows
    tn = _col_tile(d, tn)
    n_in = len(a_list)
    in_specs = ([pl.BlockSpec((tmm, a.shape[1]), lambda i, j: (i, 0)) for a in a_list]
                + [pl.BlockSpec((None, a.shape[1], tn), lambda i, j, l=l, r=r0 // a.shape[1]: (l, r, j))
                   for a, (_, l, r0) in zip(a_list, w_list)]
                + [pl.BlockSpec((gb, rows, tn), lambda i, j: (i // tpb, i % tpb, j)),
                   _mod_spec(mod, mod[1] + 2, gb, tpb, width=tn, by_col=True)])
    return pl.pallas_call(
        functools.partial(_mm_res_kernel, n_in=n_in),
        grid=(n_i, d // tn),
        in_specs=in_specs,
        out_specs=pl.BlockSpec((gb, rows, tn), lambda i, j: (i // tpb, i % tpb, j)),
        out_shape=jax.ShapeDtypeStruct((bn, t, d), F32),
        compiler_params=_cparams(2), name=name,
    )(*a_list, *[w for w, _, _ in w_list], x3, mod[0])


def _mlstm_kernel(q_ref, k_ref, v_ref, zo_ref, g_ref, bg_ref, gn_ref, c0_ref, n0_ref, m0_ref,
                  h_ref, co_ref, no_ref, mo_ref, c_s, n_s, m_s, *, tb, chunk):
    ck = pl.program_id(1)
    L = chunk
    n_seq, n_heads, dk, dv = c_s.shape

    @pl.when(ck == 0)
    def _():
        c_s[...] = c0_ref[...]
        n_s[...] = n0_ref[...]
        m_s[...] = m0_ref[...]

    def pad_rows(x):
        if tb == L:
            return x
        return jnp.concatenate([x, jnp.zeros((L - tb, x.shape[1]), x.dtype)], axis=0)

    for bi in range(n_seq):
        _mlstm_sequence(bi, q_ref, k_ref, v_ref, zo_ref, g_ref, bg_ref, gn_ref, h_ref, c_s, n_s, m_s, pad_rows, tb, L)

    @pl.when(ck == pl.num_programs(1) - 1)
    def _():
        co_ref[...] = c_s[...]
        no_ref[...] = n_s[...]
        mo_ref[...] = m_s[...]


def _mlstm_sequence(bi, q_ref, k_ref, v_ref, zo_ref, g_ref, bg_ref, gn_ref, h_ref, c_s, n_s, m_s, pad_rows, tb, L):
    _, n_heads, dk, dv = c_s.shape
    gates = pad_rows(g_ref[bi] + bg_ref[...])
    lane = lax.broadcasted_iota(jnp.int32, (L, LANES), 1)
    li_all = jnp.where(lane < n_heads, gates, 0.0)
    lf_all = jnp.where((lane >= n_heads) & (lane < 2 * n_heads),
                       -(jnp.maximum(-gates, 0.0) + jnp.log1p(jnp.exp(-jnp.abs(gates)))), 0.0)
    if tb != L:
        real = lax.broadcasted_iota(jnp.int32, (L, LANES), 0) < tb
        li_all = jnp.where(real | (lane >= n_heads), li_all, NEG)
        lf_all = jnp.where(real, lf_all, 0.0)
    ri = lax.broadcasted_iota(jnp.int32, (L, L), 0)
    ci = lax.broadcasted_iota(jnp.int32, (L, L), 1)
    tri = ri >= ci
    y = li_all + jnp.dot(tri.astype(F32), lf_all, precision=HIGHEST, preferred_element_type=F32)
    er = lax.broadcasted_iota(jnp.int32, (SUBLANES, LANES), 0)
    el = lax.broadcasted_iota(jnp.int32, (SUBLANES, LANES), 1)
    e = jnp.where(er < n_heads, jnp.where(el == er + n_heads, 1.0, 0.0) - jnp.where(el == er, 1.0, 0.0), 0.0)
    rt = lax.dot_general(e, y, _NT, precision=HIGHEST, preferred_element_type=F32)

    for hd in range(n_heads):
        li = y[:, hd:hd + 1]
        b = y[:, n_heads + hd:n_heads + hd + 1]
        r_row = rt[hd:hd + 1, :]
        q = pad_rows(q_ref[bi, :, hd * dk:(hd + 1) * dk]).astype(F32) * (dk ** -0.5)
        k = pad_rows(k_ref[bi, :, hd * dk:(hd + 1) * dk]).astype(F32)
        v = pad_rows(v_ref[bi, :, hd * dv:(hd + 1) * dv]).astype(F32)
        a = b + m_s[bi, hd][:, :1]
        dmat = jnp.where(tri, b - r_row, NEG)
        mt = jnp.maximum(a, jnp.max(dmat, axis=-1, keepdims=True))
        w_inter = jnp.exp(a - mt)
        qb, kb, vb = q.astype(BF16), k.astype(BF16), v.astype(BF16)
        s = lax.dot_general(qb, kb, _NT, preferred_element_type=F32) * jnp.exp(dmat - mt)
        c_old = c_s[bi, hd]
        n_old = n_s[bi, hd]
        num = (w_inter * jnp.dot(qb, c_old.astype(BF16), preferred_element_type=F32)
               + jnp.dot(s.astype(BF16), vb, preferred_element_type=F32))
        den = w_inter * jnp.sum(q * n_old, axis=-1, keepdims=True) + jnp.sum(s, axis=-1, keepdims=True)
        hh = (num / jnp.maximum(jnp.abs(den), jnp.exp(-mt)))[:tb]
        hn = hh * lax.rsqrt(jnp.mean(hh * hh, axis=-1, keepdims=True) + EPS)
        h_ref[bi, :, hd * dv:(hd + 1) * dv] = (hn * gn_ref[:, hd * dv:(hd + 1) * dv]
                                               * _sigmoid(zo_ref[bi, :, hd * dv:(hd + 1) * dv])).astype(h_ref.dtype)

        m_l = mt[L - 1:L]
        g_inter = jnp.exp(a[L - 1:L] - m_l)
        g_s = jnp.exp(b[L - 1:L] - b + li - m_l)
        c_s[bi, hd] = g_inter * c_old + lax.dot_general(kb, (g_s * v).astype(BF16), _TN, preferred_element_type=F32)
        n_s[bi, hd] = g_inter * n_old + jnp.sum(g_s * k, axis=0, keepdims=True)
        m_s[bi, hd] = jnp.broadcast_to(m_l, (1, LANES))


def _mlstm(q3, k3, v3, o3, g3, bg, gn, c0, n0, m0, *, out_dtype, name):
    bn, t, _ = q3.shape
    _, n_heads, dk, dv = c0.shape
    assert dk == dv and dk % LANES == 0 and 2 * n_heads <= LANES and n_heads <= SUBLANES
    if t % MLSTM_CHUNK == 0:
        tb = chunk = MLSTM_CHUNK
    else:
        assert t <= MLSTM_MIN_CHUNK
        tb, chunk = t, MLSTM_MIN_CHUNK
    n_c = t // tb
    hw = n_heads * dk
    gb = 1 if n_c > 1 else math.gcd(bn, 4)
    tok = lambda b, c: (b, c, 0)
    state = lambda b, c: (b, 0, 0, 0)
    h, c, n, m = pl.pallas_call(
        functools.partial(_mlstm_kernel, tb=tb, chunk=chunk),
        grid=(bn // gb, n_c),
        in_specs=[pl.BlockSpec((gb, tb, hw), tok)] * 4 + [
                  pl.BlockSpec((gb, tb, LANES), tok),
                  pl.BlockSpec((1, LANES), lambda b, c: (0, 0)),
                  pl.BlockSpec((1, hw), lambda b, c: (0, 0)),
                  pl.BlockSpec((gb, n_heads, dk, dv), state), pl.BlockSpec((gb, n_heads, 1, dk), state),
                  pl.BlockSpec((gb, n_heads, 1, LANES), state)],
        out_specs=[pl.BlockSpec((gb, tb, hw), tok),
                   pl.BlockSpec((gb, n_heads, dk, dv), state), pl.BlockSpec((gb, n_heads, 1, dk), state),
                   pl.BlockSpec((gb, n_heads, 1, LANES), state)],
        out_shape=[jax.ShapeDtypeStruct((bn, t, hw), out_dtype),
                   jax.ShapeDtypeStruct((bn, n_heads, dk, dv), F32),
                   jax.ShapeDtypeStruct((bn, n_heads, 1, dk), F32),
                   jax.ShapeDtypeStruct((bn, n_heads, 1, LANES), F32)],
        scratch_shapes=[pltpu.VMEM((gb, n_heads, dk, dv), F32), pltpu.VMEM((gb, n_heads, 1, dk), F32),
                        pltpu.VMEM((gb, n_heads, 1, LANES), F32)],
        compiler_params=_cparams(2), name=name,
    )(q3, k3, v3, o3, g3, bg, gn, c0, n0.reshape(bn, n_heads, 1, dk),
      jnp.broadcast_to(m0[:, :, None, None], (bn, n_heads, 1, LANES)))
    return h, c, n.reshape(bn, n_heads, dk), m[:, :, 0, 0]


def _convmod_kernel(a_ref, g_ref, buf_ref, w_ref, b_ref, lg_ref, lb_ref, y_ref, nb_ref, u_s, *, tt, width, rc):
    t = pl.program_id(1)
    hist = width - 1
    bias = b_ref[...]
    lg = lg_ref[...]
    lb = lb_ref[...]
    n_p = rc // SUBLANES
    for bi in range(a_ref.shape[0]):
        @pl.when(t == 0)
        def _():
            buf = buf_ref[bi]
            for r in range(SUBLANES):
                u_s[bi, r, 0:hist - r, :] = buf[r:, :]

        @pl.when(t > 0)
        def _():
            for r in range(SUBLANES):
                u_s[bi, r, 0:CONV_HALO, :] = u_s[bi, r, tt:tt + CONV_HALO, :]

        u = a_ref[bi] * _sigmoid(g_ref[bi])
        for r in range(SUBLANES):
            u_s[bi, r, hist - r:hist - r + tt, :] = u
        for r0 in range(0, tt, rc):
            accs = [jnp.broadcast_to(bias, (SUBLANES, bias.shape[1]))] * n_p
            for j in range(width):
                r = j % SUBLANES
                wj = w_ref[j]
                accs = [acc + wj * u_s[bi, r, r0 + p * SUBLANES + j - r:r0 + (p + 1) * SUBLANES + j - r, :]
                        for p, acc in enumerate(accs)]
            acc = accs[0] if n_p == 1 else jnp.concatenate(accs, axis=0)
            mu = jnp.mean(acc, axis=-1, keepdims=True)
            xc = acc - mu
            var = jnp.mean(xc * xc, axis=-1, keepdims=True)
            y = xc * lax.rsqrt(var + LN_EPS) * lg + lb
            y_ref[bi, r0:r0 + rc, :] = (y * _sigmoid(y)).astype(y_ref.dtype)

        @pl.when(t == pl.num_programs(1) - 1)
        def _():
            nb_ref[bi] = u_s[bi, 0, tt:tt + hist, :]


def _conv_module(a3, g3, cbuf, conv_w, conv_b, ln_g, ln_b, *, out_dtype, name):
    bn, t, _ = a3.shape
    width, ch = conv_w.shape
    assert width - 1 <= CONV_HALO
    tt = 256 if t % 256 == 0 else t
    n_t = t // tt
    assert n_t == 1 or tt >= CONV_HALO
    gb = 1 if n_t > 1 else math.gcd(bn, SUBLANES)
    y, nb = pl.pallas_call(
        functools.partial(_convmod_kernel, tt=tt, width=width,
                          rc=4 * SUBLANES if tt % (4 * SUBLANES) == 0 else SUBLANES),
        grid=(bn // gb, n_t),
        in_specs=[pl.BlockSpec((gb, tt, ch), lambda b, i: (b, i, 0)),
                  pl.BlockSpec((gb, tt, ch), lambda b, i: (b, i, 0)),
                  pl.BlockSpec((gb, width - 1, ch), lambda b, i: (b, 0, 0)),
                  pl.BlockSpec((width, SUBLANES, ch), lambda b, i: (0, 0, 0)),
                  pl.BlockSpec((1, ch), lambda b, i: (0, 0)),
                  pl.BlockSpec((1, ch), lambda b, i: (0, 0)),
                  pl.BlockSpec((1, ch), lambda b, i: (0, 0))],
        out_specs=[pl.BlockSpec((gb, tt, ch), lambda b, i: (b, i, 0)),
                   pl.BlockSpec((gb, width - 1, ch), lambda b, i: (b, 0, 0))],
        out_shape=[jax.ShapeDtypeStruct((bn, t, ch), out_dtype),
                   jax.ShapeDtypeStruct((bn, width - 1, ch), F32)],
        scratch_shapes=[pltpu.VMEM((gb, SUBLANES, CONV_HALO + tt, ch), F32)],
        compiler_params=_cparams(2), name=name,
    )(a3, g3, cbuf, jnp.broadcast_to(conv_w[:, None, :], (width, SUBLANES, ch)), conv_b.reshape(1, ch),
      ln_g.reshape(1, ch), ln_b.reshape(1, ch))
    return y, nb


def _ffn_up_kernel(x_ref, sh_ref, sc_ref, wg_ref, wu_ref, buf_ref, cw_ref, cb_ref, act_ref, gl_ref, h_s, carry_s,
                   *, tpb, rows, strip):
    ti = pl.program_id(0) % tpb
    j = pl.program_id(1)

    @pl.when(j == 0)
    def _():
        x = x_ref[...]
        r = lax.rsqrt(jnp.mean(x * x, axis=-1, keepdims=True) + EPS)
        h = (x * r) * (1.0 + sc_ref[...]) + sh_ref[...]
        h_s[...] = h.reshape(h_s.shape).astype(BF16)

    hist = buf_ref.shape[1]
    if tpb > 1:
        @pl.when(ti == 0)
        def _():
            carry_s[j] = jnp.zeros(carry_s.shape[1:], F32)
            carry_s[j, 0:hist, :] = buf_ref[0]

    h = h_s[...]
    n = h.shape[0]
    gb = n // rows
    tn = wg_ref.shape[1]
    strips = [slice(c0, c0 + strip) for c0 in range(0, tn, strip)]
    gs = [jnp.dot(h, wg_ref[:, cs], preferred_element_type=F32) for cs in strips]
    us = [jnp.dot(h, wu_ref[:, cs], preferred_element_type=F32) for cs in strips]
    for cs, g, u in zip(strips, gs, us):
        if tpb > 1:
            prev = carry_s[j, :, cs]
            carry_s[j, :, cs] = pltpu.roll(g[n - SUBLANES:], 2, axis=0)
        r8 = lax.broadcasted_iota(jnp.int32, (gb * SUBLANES, strip), 0) % SUBLANES
        if tpb == 1:
            rows01 = [jnp.broadcast_to(buf_ref[:, i:i + 1, cs], (gb, SUBLANES, strip)).reshape(gb * SUBLANES, strip)
                      for i in range(hist)]
            prev = jnp.where(r8 == 0, rows01[0], rows01[1])
        prev1 = pltpu.roll(prev, gb * SUBLANES - 1, axis=0)
        gm1 = pltpu.roll(g, 1, axis=0)
        gm2 = pltpu.roll(g, 2, axis=0)
        if rows == SUBLANES:
            gm1 = jnp.where(r8 < 1, prev1, gm1)
            gm2 = jnp.where(r8 < 2, prev, gm2)
            gl_ref[:, :, cs] = g.reshape(gb, SUBLANES, strip)
        else:
            gm1 = jnp.concatenate([jnp.where(r8 < 1, prev1, gm1[:SUBLANES]), gm1[SUBLANES:]], axis=0)
            gm2 = jnp.concatenate([jnp.where(r8 < 2, prev, gm2[:SUBLANES]), gm2[SUBLANES:]], axis=0)
            gl_ref[:, :, cs] = g[n - SUBLANES:].reshape(gb, SUBLANES, strip)
        gc = cw_ref[0:1, cs] * gm2 + cw_ref[1:2, cs] * gm1 + cw_ref[2:3, cs] * g + cb_ref[:, cs]
        act_ref[:, cs] = ((gc * _sigmoid(gc)) * u).astype(act_ref.dtype)


def _ffn_up(x3, mod, w, layer, buf, conv_w, conv_b, *, tm=1024, tn=512, name="ffn_up"):
    bn, t, d = x3.shape
    dff = w.shape[2] // 2
    gb, rows, n_i, tpb = _row_tiling(bn, t, tm)
    tmm = gb * rows
    tn = _col_tile(dff, tn)
    n_f = dff // tn
    hist = buf.shape[1]
    assert conv_w.shape[0] == 3 and hist == 2 and (tpb == 1 or gb == 1)
    act, g_tail = pl.pallas_call(
        functools.partial(_ffn_up_kernel, tpb=tpb, rows=rows, strip=min(tn, 256)),
        grid=(n_i, n_f),
        in_specs=[pl.BlockSpec((gb, rows, d), lambda i, j: (i // tpb, i % tpb, 0)),
                  _mod_spec(mod, mod[1], gb, tpb), _mod_spec(mod, mod[1] + 1, gb, tpb),
                  pl.BlockSpec((None, d, tn), lambda i, j: (layer, 0, j)),
                  pl.BlockSpec((None, d, tn), lambda i, j: (layer, 0, j + n_f)),
                  pl.BlockSpec((gb, hist, tn), lambda i, j: (i // tpb, 0, j)),
                  pl.BlockSpec((3, tn), lambda i, j: (0, j)),
                  pl.BlockSpec((1, tn), lambda i, j: (0, j))],
        out_specs=[pl.BlockSpec((tmm, tn), lambda i, j: (i, j)),
                   pl.BlockSpec((gb, SUBLANES, tn), lambda i, j: (i, 0, j))],
        out_shape=[jax.ShapeDtypeStruct((bn * t, dff), BF16),
                   jax.ShapeDtypeStruct((n_i * gb, SUBLANES, dff), F32)],
        scratch_shapes=[pltpu.VMEM((tmm, d), BF16), pltpu.VMEM((n_f, SUBLANES, tn), F32)],
        compiler_params=_cparams(2), name=name,
    )(x3, mod[0], mod[0], w, w, buf, conv_w, conv_b.reshape(1, dff))
    return act, g_tail.reshape(bn, tpb, SUBLANES, dff)[:, tpb - 1]


def _t5_bucket(dist):
    n = jnp.maximum(dist, 0)
    exact = REL_BUCKETS // 2
    large = exact + (jnp.log(jnp.maximum(n, 1).astype(F32) / exact)
                     / math.log(REL_MAX_DIST / exact) * (REL_BUCKETS - exact)).astype(jnp.int32)
    return jnp.where(n < exact, n, jnp.minimum(large, REL_BUCKETS - 1))


def _bias_lookup(dist, table_col):
    bucket = _t5_bucket(dist)
    bias = jnp.zeros(dist.shape, F32)
    for kk in range(REL_BUCKETS):
        bias = jnp.where(bucket == kk, table_col(kk), bias)
    return bias * LOG2E


def _bias_prompt_kernel(rb_ref, o_ref, *, blk):
    rb = rb_ref[0]
    key = lax.broadcasted_iota(jnp.int32, (blk, blk), 0)
    qry = lax.broadcasted_iota(jnp.int32, (blk, blk), 1)
    col = lambda kk: rb[:, kk:kk + 1]
    own = _bias_lookup(qry - key, col)
    o_ref[0, 0] = jnp.where(qry >= key, own, NEG)
    o_ref[0, 1] = _bias_lookup(blk + qry - key, col)


def _bias_prompt(rel_bias, blk):
    n_b, n_h = rel_bias.shape
    rbt = jnp.pad(rel_bias.T, ((0, 0), (0, LANES - n_b))).reshape(n_h, 1, LANES)
    return pl.pallas_call(
        functools.partial(_bias_prompt_kernel, blk=blk),
        grid=(n_h,),
        in_specs=[pl.BlockSpec((1, 1, LANES), lambda h: (h, 0, 0))],
        out_specs=pl.BlockSpec((1, 2, blk, blk), lambda h: (h, 0, 0, 0)),
        out_shape=jax.ShapeDtypeStruct((n_h, 2, blk, blk), F32),
        compiler_params=_cparams(1), name="bias_prompt",
    )(rbt), rbt


def _bias_sample_kernel(rb_ref, fl_ref, new_ref, *, n_heads, t, page):
    rb = rb_ref[...]
    col = lambda kk: rb[:, kk:kk + 1]
    for ref, base in ((fl_ref.at[0], 2 * page), (fl_ref.at[1], page), (new_ref, 0)):
        shape = ref.shape
        r = lax.broadcasted_iota(jnp.int32, shape, 0)
        c = lax.broadcasted_iota(jnp.int32, shape, 1)
        if ref is new_ref:
            key_head, key_off = c // t, c % t
        else:
            key_head, key_off = c % n_heads, c // n_heads
        dist = base + r % t - key_off
        ok = (r // t == key_head) & (dist >= 0)
        ref[...] = jnp.where(ok, _bias_lookup(dist, col), NEG)


def _bias_sample(rel_bias, t, page):
    n_b, n_h = rel_bias.shape
    rows = t * n_h
    rb = jnp.pad(jnp.repeat(rel_bias.T, t, axis=0), ((0, 0), (0, LANES - n_b)))
    return pl.pallas_call(
        functools.partial(_bias_sample_kernel, n_heads=n_h, t=t, page=page),
        out_shape=[jax.ShapeDtypeStruct((2, rows, page * n_h), F32),
                   jax.ShapeDtypeStruct((rows, rows), F32)],
        compiler_params=pltpu.CompilerParams(vmem_limit_bytes=VMEM_LIMIT_BYTES), name="bias_sample",
    )(rb)


def _attn_prompt_kernel(q_ref, k_ref, v_ref, bias_ref, rb_ref, o_ref, km_s, kb_s, vt_s, s_s, *, blk, n_blk, scale):
    km_s[...] = jnp.zeros_like(km_s)
    for j in range(n_blk):
        kj = k_ref[0, j * blk:(j + 1) * blk, :]
        km_s[j:j + 1, :] = jnp.mean(kj, axis=0, keepdims=True)
        kb_s[j * blk:(j + 1) * blk, :] = kj.astype(BF16)
        vt_s[:, j * blk:(j + 1) * blk] = v_ref[0, j * blk:(j + 1) * blk, :].T.astype(BF16)

    far = rb_ref[0][:, REL_BUCKETS - 1:REL_BUCKETS] * LOG2E
    km = km_s[...]
    row = lax.broadcasted_iota(jnp.int32, (km.shape[0], blk), 0)
    tile = 0
    for c in range(n_blk):
        qf = q_ref[0, c * blk:(c + 1) * blk, :]
        qb = qf.astype(BF16)
        add = [0.0] * c
        if c > MOBA_TOPK:
            sc = lax.dot_general(km, qf, _NT, precision=HIGHEST, preferred_element_type=F32)
            sc = jnp.where(row < c, sc, -jnp.inf)
            for j in range(c):
                sj = sc[j:j + 1, :]
                ahead = (sc > sj) | ((sc == sj) & (row < j))
                rank = jnp.sum(jnp.where(ahead, 1.0, 0.0), axis=0, keepdims=True)
                add[j] = jnp.where(rank < MOBA_TOPK, 0.0, NEG)
        m = None
        for j in range(c + 1):
            st = lax.dot_general(kb_s[j * blk:(j + 1) * blk, :], qb, _NT, preferred_element_type=F32) * (scale * LOG2E)
            if j == c:
                st = st + bias_ref[0, 0]
            elif j == c - 1:
                st = st + (bias_ref[0, 1] + add[j])
            else:
                st = st + (add[j] + far)
            s_s[tile + j] = st
            mj = jnp.max(st, axis=0, keepdims=True)
            m = mj if m is None else jnp.maximum(m, mj)
        l = jnp.zeros_like(m)
        acc = jnp.zeros((vt_s.shape[0], blk), F32)
        for j in range(c + 1):
            p = jnp.exp2(s_s[tile + j] - m)
            l = l + jnp.sum(p, axis=0, keepdims=True)
            acc = acc + jnp.dot(vt_s[:, j * blk:(j + 1) * blk], p.astype(BF16), preferred_element_type=F32)
        o_ref[0, c * blk:(c + 1) * blk, :] = (acc / l).T.astype(o_ref.dtype)
        tile += c + 1


def _attn_prompt(q3, k3, v3, bias_tiles, rbt, *, out_dtype):
    bn, s, _ = q3.shape
    n_h = bias_tiles.shape[0]
    dh = q3.shape[2] // n_h
    blk = MOBA_BLOCK
    assert s % blk == 0 and dh == LANES and blk >= REL_MAX_DIST
    n_blk = s // blk
    head = lambda b, h: (b, 0, h)
    return pl.pallas_call(
        functools.partial(_attn_prompt_kernel, blk=blk, n_blk=n_blk, scale=dh ** -0.5),
        grid=(bn, n_h),
        in_specs=[pl.BlockSpec((1, s, dh), head), pl.BlockSpec((1, s, dh), head), pl.BlockSpec((1, s, dh), head),
                  pl.BlockSpec((1, 2, blk, blk), lambda b, h: (h, 0, 0, 0)),
                  pl.BlockSpec((1, 1, LANES), lambda b, h: (h, 0, 0))],
        out_specs=pl.BlockSpec((1, s, dh), head),
        out_shape=jax.ShapeDtypeStruct(q3.shape, out_dtype),
        scratch_shapes=[pltpu.VMEM((-(-n_blk // SUBLANES) * SUBLANES, dh), F32), pltpu.VMEM((s, dh), BF16),
                        pltpu.VMEM((dh, s), BF16), pltpu.VMEM((n_blk * (n_blk + 1) // 2, blk, blk), F32)],
        compiler_params=_cparams(2), name="moba_prompt",
    )(q3, k3, v3, bias_tiles, rbt)


def _attn_sample_kernel(pt_ref, q_ref, kn_ref, vn_ref, *refs, n_heads, ppb, pps, scale):
    del pt_ref
    kp_refs, vp_refs = refs[:pps], refs[pps:2 * pps]
    bias_ref, new_ref, o_ref, ks_s, m_s, l_s, o_s = refs[2 * pps:]
    g = pl.program_id(1)
    last = pl.num_programs(1) - 1
    n_tok = q_ref.shape[1]
    dh = q_ref.shape[2] // n_heads

    def head_major(ref):
        return jnp.concatenate([ref[0, :, h * dh:(h + 1) * dh] for h in range(n_heads)], axis=0)

    qf = head_major(q_ref)
    qb = qf.astype(BF16)
    rows = n_heads * n_tok

    def partial_softmax(kk, vv, bias):
        s = lax.dot_general(qb, kk.astype(BF16), _NT, preferred_element_type=F32) * (scale * LOG2E) + bias
        m = jnp.max(s, axis=-1, keepdims=True)
        e = jnp.exp2(s - m)
        return m, jnp.sum(e, axis=-1, keepdims=True), jnp.dot(e.astype(BF16), vv.astype(BF16),
                                                              preferred_element_type=F32)

    for r in range(pps):
        pg = g * pps + r
        kp = kp_refs[r][0]
        ks_s[pg] = jnp.sum(kp.reshape(kp.shape[0] // n_heads, n_heads, dh), axis=0)
        bias = bias_ref[jnp.where(g == last, 1, 0)] if r == pps - 1 else bias_ref[0]
        m, l, o = partial_softmax(kp, vp_refs[r][0], bias)
        m_s[pg] = m
        l_s[pg] = l
        o_s[pg] = o

    @pl.when(g == last)
    def _():
        n_pg = m_s.shape[0]
        n_blk = n_pg // ppb
        inv = 1.0 / (ppb * (kp_refs[0].shape[1] // n_heads))
        scs = []
        for j in range(n_blk):
            km = ks_s[j * ppb]
            for pp in range(1, ppb):
                km = km + ks_s[j * ppb + pp]
            km = jnp.broadcast_to((km * inv)[:, None, :], (n_heads, n_tok, dh)).reshape(rows, dh)
            scs.append(jnp.sum(qf * km, axis=-1, keepdims=True))
        sel = []
        for j in range(n_blk):
            rank = jnp.zeros((rows, 1), F32)
            for jj in range(n_blk):
                if jj != j:
                    ahead = (scs[jj] >= scs[j]) if jj < j else (scs[jj] > scs[j])
                    rank = rank + jnp.where(ahead, 1.0, 0.0)
            sel.append(rank < MOBA_TOPK)
        m_n, l_n, o_n = partial_softmax(head_major(kn_ref), head_major(vn_ref), new_ref[...])
        m_all = m_n
        for pg in range(n_pg):
            m_all = jnp.maximum(m_all, jnp.where(sel[pg // ppb], m_s[pg], NEG))
        w_n = jnp.exp2(m_n - m_all)
        l_all = w_n * l_n
        o_all = w_n * o_n
        for pg in range(n_pg):
            w = jnp.where(sel[pg // ppb], jnp.exp2(m_s[pg] - m_all), 0.0)
            l_all = l_all + w * l_s[pg]
            o_all = o_all + w * o_s[pg]
        o = (o_all / l_all).astype(o_ref.dtype)
        for h in range(n_heads):
            o_ref[0, :, h * dh:(h + 1) * dh] = o[h * n_tok:(h + 1) * n_tok]


def _attn_sample(q, k, v, cache_k, cache_v, page_table, far_last, new, *, out_dtype, pps=8):
    db, t, hd = q.shape
    n_pool, page, n_h, dh = cache_k.shape
    n_pages = page_table.shape[1]
    rows = t * n_h
    assert MOBA_BLOCK % page == 0 and (n_pages * page) % MOBA_BLOCK == 0 and t <= MOBA_BLOCK
    assert page >= REL_MAX_DIST and n_pages * page // MOBA_BLOCK > MOBA_TOPK and dh == LANES
    assert n_pages % pps == 0
    ppb = MOBA_BLOCK // page
    seq = lambda b, g, pt: (b, 0, 0)
    paged = [(lambda b, g, pt, r=r: (pt[b, g * pps + r], 0, 0)) for r in range(pps)]
    page_spec = lambda r: pl.BlockSpec((1, page * n_h, dh), paged[r])
    grid_spec = pltpu.PrefetchScalarGridSpec(
        num_scalar_prefetch=1,
        grid=(db, n_pages // pps),
        in_specs=([pl.BlockSpec((1, t, hd), seq)] * 3 + [page_spec(r) for r in range(pps)] * 2
                  + [pl.BlockSpec(far_last.shape, lambda b, g, pt: (0, 0, 0)),
                     pl.BlockSpec(new.shape, lambda b, g, pt: (0, 0))]),
        out_specs=pl.BlockSpec((1, t, hd), seq),
        scratch_shapes=[pltpu.VMEM((n_pages, n_h, dh), F32), pltpu.VMEM((n_pages, rows, 1), F32),
                        pltpu.VMEM((n_pages, rows, 1), F32), pltpu.VMEM((n_pages, rows, dh), F32)],
    )
    ck = cache_k.reshape(n_pool, page * n_h, dh)
    cv = cache_v.reshape(n_pool, page * n_h, dh)
    return pl.pallas_call(
        functools.partial(_attn_sample_kernel, n_heads=n_h, ppb=ppb, pps=pps, scale=dh ** -0.5),
        grid_spec=grid_spec,
        out_shape=jax.ShapeDtypeStruct((db, t, hd), out_dtype),
        compiler_params=_cparams(2), name="moba_sample",
    )(page_table, q, k, v, *([ck] * pps), *([cv] * pps), far_last, new)


def _trunk(x, mods, row0, mc, mn, mm, cb, fb, wts, attend):
    bn, t, d = x.shape
    depth = wts["ffn_up"].shape[0]
    n_ha = mc.shape[2]
    qkw = n_ha * mc.shape[3]
    aw = n_ha * mc.shape[4]
    bw = wts["conv_w"].shape[2]
    big = t >= 512
    act_dtype = BF16 if big else F32
    out_c, out_n, out_m, out_conv, out_k, out_v, out_ffn = [], [], [], [], [], [], []
    for l in range(depth):
        mod = (mods, 3 * (2 * l), row0)
        if l % 2 == 0:
            e = l // 2
            assert qkw == aw == bw
            w_a, w_b = wts["in_even"], wts["in_even_b"]
            secs = ([(w_a, e, i * qkw, None, act_dtype) for i in range(3)] + [(w_a, e, 3 * qkw, None, F32)]
                    + [(w_b, e, i * bw, None, F32) for i in range(2)])
            *zs, g = _nm_matmul(x, mod, secs, qkw, narrow=(wts["in_even_gate"], e), tn=256 if big else 128,
                                name="in_even")
            zq, zk, zv, zo, za, zg = (z.reshape(bn, t, qkw) for z in zs)
            h_a, c_new, n_new, m_new = _mlstm(zq, zk, zv, zo, g.reshape(bn, t, LANES), wts["b_gates"][e],
                                              wts["mlstm_norm_g"][e], mc[e], mn[e], mm[e], out_dtype=act_dtype,
                                              name="mlstm")
            y_b, cb_new = _conv_module(za, zg, cb[e], wts["conv_w"][e], wts["conv_b"][e],
                                       wts["conv_ln_g"][e], wts["conv_ln_b"][e], out_dtype=act_dtype, name="conv_module")
            x = _mm_res([h_a.reshape(bn * t, aw), y_b.reshape(bn * t, bw)],
                        [(wts["out_even"], e, 0), (wts["out_even"], e, aw)], x, mod, tm=2048, tn=512, name="out_even")
            out_c.append(c_new); out_n.append(n_new); out_m.append(m_new); out_conv.append(cb_new)
        else:
            o = l // 2
            w = wts["in_odd"]
            qkv = _nm_matmul(x, mod, [(w, o, 0, wts["q_norm_g"][o], F32), (w, o, d, wts["k_norm_g"][o], F32),
                                      (w, o, 2 * d, None, F32)], d, tn=512 if big else 256, name="in_odd")
            q, k, v = (a.reshape(bn, t, d) for a in qkv)
            a = attend(o, q, k, v, act_dtype)
            x = _mm_res([a.reshape(bn * t, d)], [(wts["out_odd"], o, 0)], x, mod, tm=2048, tn=512, name="out_odd")
            out_k.append(k); out_v.append(v)
        mod = (mods, 3 * (2 * l + 1), row0)
        hist = fb[l].shape[1]
        assert hist <= min(t, SUBLANES)
        act, g_last = _ffn_up(x, mod, wts["ffn_up"], l, fb[l], wts["ffn_conv_w"][l], wts["ffn_conv_b"][l])
        x = _mm_res([act], [(wts["ffn_down"], l, 0)], x, mod, tn=512, name="ffn_down")
        out_ffn.append(g_last[:, SUBLANES - hist:])
    return (x, jnp.stack(out_c), jnp.stack(out_n), jnp.stack(out_m), jnp.stack(out_conv),
            jnp.stack(out_k), jnp.stack(out_v), jnp.stack(out_ffn))


def kernel(x_prompt, x_sample, state_mlstm_c, state_mlstm_n, state_mlstm_m, state_conv, cache_k, cache_v, page_table, state_ffn, c_prompt, c_sample, ada_w, ada_b, w_in_even, b_gates, mlstm_norm_g, conv_w, conv_b, conv_ln_g, conv_ln_b, w_out_even, w_in_odd, q_norm_g, k_norm_g, rel_bias, w_out_odd, ffn_w_up, ffn_conv_w, ffn_conv_b, ffn_w_down):
    bsz, seq, d = x_prompt.shape
    db, dt, _ = x_sample.shape
    depth = ada_w.shape[0]
    n_even, _, n_ha, dk, dv = state_mlstm_c.shape
    n_hc, dh = cache_k.shape[3], cache_k.shape[4]
    qkw, aw, bw = n_ha * dk, n_ha * dv, conv_w.shape[2]
    n_gate = 2 * n_ha
    o_gate = 2 * qkw + 2 * aw
    dt_x = x_prompt.dtype

    wts = {
        "in_even": w_in_even.astype(BF16),
        "in_even_b": w_in_even[:, :, o_gate + n_gate:].astype(BF16),
        "in_even_gate": jnp.pad(w_in_even[:, :, o_gate:o_gate + n_gate],
                                ((0, 0), (0, 0), (0, LANES - n_gate))).astype(BF16),
        "b_gates": jnp.pad(b_gates, ((0, 0), (0, LANES - n_gate))).reshape(n_even, 1, LANES),
        "mlstm_norm_g": mlstm_norm_g.reshape(n_even, 1, aw),
        "conv_w": conv_w, "conv_b": conv_b, "conv_ln_g": conv_ln_g, "conv_ln_b": conv_ln_b,
        "out_even": w_out_even.astype(BF16),
        "in_odd": w_in_odd.astype(BF16), "q_norm_g": q_norm_g, "k_norm_g": k_norm_g,
        "out_odd": w_out_odd.astype(BF16),
        "ffn_up": ffn_w_up.astype(BF16), "ffn_conv_w": ffn_conv_w, "ffn_conv_b": ffn_conv_b,
        "ffn_down": ffn_w_down.astype(BF16),
    }

    mods = _adaln_mods(jnp.concatenate([c_sample, c_prompt], axis=0), ada_w, ada_b)

    bias_tiles, rbt = _bias_prompt(rel_bias, MOBA_BLOCK)
    far_last, new = _bias_sample(rel_bias, dt, cache_k.shape[2])

    def prompt_attend(o, q, k, v, out_dtype):
        return _attn_prompt(q, k, v, bias_tiles, rbt, out_dtype=out_dtype)

    def sample_attend(o, q, k, v, out_dtype):
        return _attn_sample(q, k, v, cache_k[o], cache_v[o], page_table, far_last, new, out_dtype=out_dtype)

    zc = jnp.zeros((n_even, bsz, n_ha, dk, dv), dt_x)
    zn = jnp.zeros((n_even, bsz, n_ha, dk), dt_x)
    zm = jnp.zeros((n_even, bsz, n_ha), dt_x)
    zconv = jnp.zeros((n_even, bsz) + state_conv.shape[2:], dt_x)
    zffn = jnp.zeros((depth, bsz) + state_ffn.shape[2:], dt_x)

    y_p, p_c, p_n, p_m, p_conv, p_k, p_v, p_ffn = _trunk(
        x_prompt, mods, db, zc, zn, zm, zconv, zffn, wts, prompt_attend)
    y_s, s_c, s_n, s_m, s_conv, s_k, s_v, s_ffn = _trunk(
        x_sample, mods, 0, state_mlstm_c, state_mlstm_n, state_mlstm_m, state_conv, state_ffn, wts, sample_attend)

    def heads(a):
        return a.reshape(a.shape[:3] + (n_hc, dh))

    return (y_p, y_s, p_c, p_n, p_m, p_conv, heads(p_k), heads(p_v), p_ffn,
            s_c, s_n, s_m, s_conv, heads(s_k), heads(s_v), s_ffn)
```
